```python
import jax, jax.numpy as jnp
from jax import lax
import numpy as np

D_MODEL = 1024
BATCH = 8
SEQ = 4096
DEPTH = 2

CTX_LEN = 256
GRID_W = 64
EPS = 1e-6
NA_HEADS = 8
NA_HEAD_DIM = 64
NA_WIDTH = NA_HEADS * NA_HEAD_DIM
NA_WIN_ROWS = 8
NA_WIN_COLS = 16
ROPE_THETA = 10000.0
ROT_HALF = NA_HEAD_DIM // 2
HG_HEADS = 4
HG_KEY_DIM = 128
HG_VAL_DIM = 128
HG_KEY_WIDTH = HG_HEADS * HG_KEY_DIM
HG_VAL_WIDTH = HG_HEADS * HG_VAL_DIM
HG_CHUNK = 64
GATE_FLOOR = 1e-30
IN_WIDTH = 3 * NA_WIDTH + 3 * HG_KEY_WIDTH + 2 * HG_VAL_WIDTH + 2 * D_MODEL
N_EXPERTS = 16
N_GROUPS = 4
EXPERTS_PER_GROUP = N_EXPERTS // N_GROUPS
TOP_K = 2
EXPERT_FF = 512
MASK_VALUE = -1e9

kernel_name = "hybrid_na_hgrn2_grouped_moe_dit"


def rms_norm(x, g):
    xf = x.astype(jnp.float32)
    y = xf * lax.rsqrt(jnp.mean(xf * xf, axis=-1, keepdims=True) + EPS)
    return (y * g.astype(jnp.float32)).astype(x.dtype)


def modulate(h, shift, scale):
    return h * (1 + scale) + shift


def to_heads(t, n):
    B, L, _ = t.shape
    return t.reshape(B, L, n, -1).transpose(0, 2, 1, 3)


def from_heads(t):
    B, H, L, d = t.shape
    return t.transpose(0, 2, 1, 3).reshape(B, L, H * d)


def split_columns(p):
    widths = (NA_WIDTH,) * 3 + (HG_KEY_WIDTH,) * 3 + (HG_VAL_WIDTH,) * 2 + (D_MODEL,) * 2
    offs = np.cumsum(widths)[:-1].tolist()
    return jnp.split(p, offs, axis=-1)


def axial_rope(L):
    t = jnp.arange(L, dtype=jnp.int32)
    row = (t // GRID_W).astype(jnp.float32)
    col = (t % GRID_W).astype(jnp.float32)
    inv = ROPE_THETA ** (-jnp.arange(0, ROT_HALF, 2, dtype=jnp.float32) / ROT_HALF)
    ang = jnp.concatenate([row[:, None] * inv, col[:, None] * inv], axis=-1)
    return jnp.cos(ang), jnp.sin(ang)


def apply_rope(x, cos, sin):
    xp = x.astype(jnp.float32).reshape(*x.shape[:-1], NA_HEAD_DIM // 2, 2)
    x1, x2 = xp[..., 0], xp[..., 1]
    out = jnp.stack([x1 * cos - x2 * sin, x1 * sin + x2 * cos], axis=-1)
    return out.reshape(x.shape).astype(x.dtype)


def na_indices(rows):
    wr = min(NA_WIN_ROWS, rows)
    L = rows * GRID_W
    t = jnp.arange(L, dtype=jnp.int32)
    r, col = t // GRID_W, t % GRID_W
    rs = jnp.clip(r - wr // 2, 0, rows - wr)
    cs = jnp.clip(col - NA_WIN_COLS // 2, 0, GRID_W - NA_WIN_COLS)
    kr = rs[:, None, None] + jnp.arange(wr, dtype=jnp.int32)[None, :, None]
    kc = cs[:, None, None] + jnp.arange(NA_WIN_COLS, dtype=jnp.int32)[None, None, :]
    full = (L, wr, NA_WIN_COLS)
    shape = (rows, GRID_W, wr * NA_WIN_COLS)
    idx = jnp.broadcast_to(kr * GRID_W + kc, full).reshape(shape)
    dr = jnp.broadcast_to(kr - r[:, None, None] + NA_WIN_ROWS - 1, full).reshape(shape)
    dc = jnp.broadcast_to(kc - col[:, None, None] + NA_WIN_COLS - 1, full).reshape(shape)
    return idx, dr, dc


def neighbourhood_attention(q_l, k_l, v_l, k_c, v_c, rel_bias, rows):
    B, H, L, dh = q_l.shape
    idx, dr, dc = na_indices(rows)
    scale = dh ** -0.5
    q_blocks = jnp.moveaxis(q_l.reshape(B, H, rows, GRID_W, dh), 2, 0)

    def block(args):
        q_b, idx_b, dr_b, dc_b = args
        k_g = k_l[:, :, idx_b]
        v_g = v_l[:, :, idx_b]
        s_win = jnp.einsum('bhqd,bhqkd->bhqk', q_b, k_g).astype(jnp.float32) * scale \
            + rel_bias[:, dr_b, dc_b].astype(jnp.float32)[None]
        s_ctx = jnp.einsum('bhqd,bhkd->bhqk', q_b, k_c).astype(jnp.float32) * scale
        p = jax.nn.softmax(jnp.concatenate([s_win, s_ctx], axis=-1), axis=-1).astype(v_l.dtype)
        nk = idx_b.shape[-1]
        return jnp.einsum('bhqk,bhqkd->bhqd', p[..., :nk], v_g) + jnp.einsum('bhqk,bhkd->bhqd', p[..., nk:], v_c)

    o = lax.map(block, (q_blocks, idx, dr, dc))
    return jnp.moveaxis(o, 0, 2).reshape(B, H, L, dh)


def context_attention(q_c, k_c, v_c):
    s = jnp.einsum('bhqd,bhkd->bhqk', q_c, k_c).astype(jnp.float32) * (q_c.shape[-1] ** -0.5)
    p = jax.nn.softmax(s, axis=-1).astype(v_c.dtype)
    return jnp.einsum('bhqk,bhkd->bhqd', p, v_c)


def layer_lower_bounds(raw):
    p = jax.nn.softmax(raw.astype(jnp.float32), axis=0)
    return jnp.cumsum(p, axis=0) - p[0:1]


def hgrn2_gates(f, lb):
    lb = lb.astype(jnp.float32).reshape(HG_HEADS, 1, HG_KEY_DIM)
    f = f.astype(jnp.float32)
    g = lb + (1 - lb) * jax.nn.sigmoid(f)
    log_g = jnp.log(jnp.maximum(g, GATE_FLOOR))
    k = (1 - lb) * jax.nn.sigmoid(-f)
    return k, log_g


def gla_chunked(q, k, v, log_g, s0):
    B, H, L, dk = q.shape
    n = L // HG_CHUNK
    out_dtype = v.dtype

    def chunks(t):
        return jnp.moveaxis(t.astype(jnp.float32).reshape(B, H, n, HG_CHUNK, t.shape[-1]), 2, 0)

    incl = jnp.tril(jnp.ones((HG_CHUNK, HG_CHUNK), dtype=bool))[..., None]

    def step(S, xs):
        qc, kc, vc, gc = xs
        b = jnp.cumsum(gc, axis=2)
        diff = b[:, :, :, None, :] - b[:, :, None, :, :]
        decay = jnp.where(incl, jnp.exp(jnp.where(incl, diff, 0.0)), 0.0)
        A = jnp.einsum('bhtsk,bhsk->bhts', decay * qc[:, :, :, None, :], kc)
        o = jnp.einsum('bhts,bhsv->bhtv', A, vc) + jnp.einsum('bhtk,bhkv->bhtv', qc * jnp.exp(b), S)
        b_last = b[:, :, -1:, :]
        S_new = jnp.exp(b_last[:, :, 0, :])[..., None] * S \
            + jnp.einsum('bhsk,bhsv->bhkv', kc * jnp.exp(b_last - b), vc)
        return S_new, o

    s_fin, o = lax.scan(step, s0, (chunks(q), chunks(k), chunks(v), chunks(log_g)))
    o = jnp.moveaxis(o, 0, 2).reshape(B, H, L, v.shape[-1])
    return o.astype(out_dtype), s_fin


def hgrn2_bidirectional(q_l, i_l, q_c, i_c, dirs):
    B = q_l.shape[0]
    o_lat, o_ctx = [], []
    for f_l, f_c, lb, flip in dirs:
        rev = (lambda t: t[:, :, ::-1]) if flip else (lambda t: t)
        k_c, lg_c = hgrn2_gates(f_c, lb)
        k_l, lg_l = hgrn2_gates(f_l, lb)
        s0 = jnp.zeros((B, HG_HEADS, HG_KEY_DIM, HG_VAL_DIM), jnp.float32)
        oc, s_ctx = gla_chunked(rev(q_c), rev(k_c), rev(i_c), rev(lg_c), s0)
        ol, _ = gla_chunked(rev(q_l), rev(k_l), rev(i_l), rev(lg_l), s_ctx)
        o_lat.append(rev(ol))
        o_ctx.append(rev(oc))
    return o_lat[0] + o_lat[1], o_ctx[0] + o_ctx[1]


def gated_head_norm(o, gate, gain):
    return from_heads(rms_norm(o, gain)) * jax.nn.silu(gate)


def merge_branches(o_na, o_hg, hg_gate, gate_na, gate_hg, hg_gain, w_na_o, w_hg_o, w_out):
    y_na = from_heads(o_na) @ w_na_o
    y_hg = gated_head_norm(o_hg, hg_gate, hg_gain) @ w_hg_o
    return (jax.nn.sigmoid(gate_na) * y_na + jax.nn.sigmoid(gate_hg) * y_hg) @ w_out


def parallel_mixer(h_lat, h_ctx, w_in, rel_bias, lb_f, lb_b, hg_gain, w_na_o, w_hg_o, w_out, with_ctx_out):
    B, L, _ = h_lat.shape
    rows = L // GRID_W
    (na_q_l, na_k_l, na_v_l, hq_l, hf_l, hb_l, hi_l, hg_l, ga_l, gh_l) = split_columns(h_lat @ w_in)
    (na_q_c, na_k_c, na_v_c, hq_c, hf_c, hb_c, hi_c, hg_c, ga_c, gh_c) = split_columns(h_ctx @ w_in)
    cos, sin = axial_rope(L)
    q_l = apply_rope(to_heads(na_q_l, NA_HEADS), cos, sin)
    k_l = apply_rope(to_heads(na_k_l, NA_HEADS), cos, sin)
    v_l = to_heads(na_v_l, NA_HEADS)
    k_c = to_heads(na_k_c, NA_HEADS)
    v_c = to_heads(na_v_c, NA_HEADS)
    o_na_l = neighbourhood_attention(q_l, k_l, v_l, k_c, v_c, rel_bias, rows)
    dirs = ((to_heads(hf_l, HG_HEADS), to_heads(hf_c, HG_HEADS), lb_f, False),
            (to_heads(hb_l, HG_HEADS), to_heads(hb_c, HG_HEADS), lb_b, True))
    o_hg_l, o_hg_c = hgrn2_bidirectional(to_heads(jax.nn.silu(hq_l), HG_HEADS), to_heads(hi_l, HG_HEADS),
                                         to_heads(jax.nn.silu(hq_c), HG_HEADS), to_heads(hi_c, HG_HEADS), dirs)
    y_l = merge_branches(o_na_l, o_hg_l, hg_l, ga_l, gh_l, hg_gain, w_na_o, w_hg_o, w_out)
    if not with_ctx_out:
        return y_l, None
    o_na_c = context_attention(to_heads(na_q_c, NA_HEADS), k_c, v_c)
    y_c = merge_branches(o_na_c, o_hg_c, hg_c, ga_c, gh_c, hg_gain, w_na_o, w_hg_o, w_out)
    return y_l, y_c


def grouped_moe(h, w_router, router_bias, w_gate, w_up, w_down):
    shp = h.shape
    t = h.reshape(-1, shp[-1])
    scores = jax.nn.sigmoid((t @ w_router).astype(jnp.float32))
    grouped = (scores + router_bias.astype(jnp.float32)).reshape(-1, N_GROUPS, EXPERTS_PER_GROUP)
    group_score = lax.top_k(grouped, TOP_K)[0].sum(-1)
    g_sel = jnp.argmax(group_score, axis=-1)
    in_group = (jnp.arange(N_GROUPS) == g_sel[:, None])[..., None]
    cand = jnp.where(in_group, grouped, MASK_VALUE).reshape(-1, N_EXPERTS)
    _, e_idx = lax.top_k(cand, TOP_K)
    w = jnp.take_along_axis(scores, e_idx, axis=-1)
    w = w / jnp.sum(w, axis=-1, keepdims=True)
    combine = jnp.sum(jax.nn.one_hot(e_idx, N_EXPERTS, dtype=jnp.float32) * w[..., None], axis=1).astype(t.dtype)
    out = jnp.zeros_like(t)
    for e in range(N_EXPERTS):
        h_e = jax.nn.silu(t @ w_gate[e]) * (t @ w_up[e])
        out = out + combine[:, e:e + 1] * (h_e @ w_down[e])
    return out.reshape(shp)


def setup_inputs(seed: int = 0) -> dict:
    key = jax.random.key(seed)
    ks = jax.random.split(key, 24)
    D = D_MODEL

    def nrm(k, shape, scale):
        return jax.random.normal(k, shape, jnp.float32) * scale

    return {
        'x': nrm(ks[0], (BATCH, SEQ, D), 1.0),
        'c': nrm(ks[1], (BATCH, D), 1.0),
        'ctx': nrm(ks[2], (BATCH, CTX_LEN, D), 1.0),
        'c_ctx': nrm(ks[3], (D,), 1.0),
        'w_ada': nrm(ks[4], (DEPTH, D, 6 * D), 0.5 * D ** -0.5),
        'b_ada': nrm(ks[5], (DEPTH, 6 * D), 0.02),
        'norm1_g': 1.0 + nrm(ks[6], (DEPTH, D), 0.02),
        'w_in': nrm(ks[7], (DEPTH, D, IN_WIDTH), D ** -0.5),
        'na_rel_bias': nrm(ks[8], (DEPTH, NA_HEADS, 2 * NA_WIN_ROWS - 1, 2 * NA_WIN_COLS - 1), 0.1),
        'hg_lower_fwd': nrm(ks[9], (DEPTH, HG_KEY_WIDTH), 0.1),
        'hg_lower_bwd': nrm(ks[10], (DEPTH, HG_KEY_WIDTH), 0.1),
        'hg_norm_g': 1.0 + nrm(ks[11], (DEPTH, HG_VAL_DIM), 0.02),
        'w_na_o': nrm(ks[12], (DEPTH, NA_WIDTH, D), NA_WIDTH ** -0.5),
        'w_hg_o': nrm(ks[13], (DEPTH, HG_VAL_WIDTH, D), HG_VAL_WIDTH ** -0.5),
        'w_out': nrm(ks[14], (DEPTH, D, D), D ** -0.5),
        'norm2_g': 1.0 + nrm(ks[15], (DEPTH, D), 0.02),
        'w_router': nrm(ks[16], (D, N_EXPERTS), D ** -0.5),
        'router_bias': nrm(ks[17], (N_EXPERTS,), 0.01),
        'w_gate': nrm(ks[18], (DEPTH, N_EXPERTS, D, EXPERT_FF), D ** -0.5),
        'w_up': nrm(ks[19], (DEPTH, N_EXPERTS, D, EXPERT_FF), D ** -0.5),
        'w_down': nrm(ks[20], (DEPTH, N_EXPERTS, EXPERT_FF, D), EXPERT_FF ** -0.5),
        'final_g': 1.0 + nrm(ks[21], (D,), 0.02),
    }


def reference(x, c, ctx, c_ctx, w_ada, b_ada, norm1_g, w_in, na_rel_bias, hg_lower_fwd, hg_lower_bwd,
              hg_norm_g, w_na_o, w_hg_o, w_out, norm2_g, w_router, router_bias, w_gate, w_up, w_down, final_g):
    n_ctx = ctx.shape[1]
    lb_f = layer_lower_bounds(hg_lower_fwd)
    lb_b = layer_lower_bounds(hg_lower_bwd)
    for l in range(DEPTH):
        last = l == DEPTH - 1
        sh1, sc1, g1, sh2, sc2, g2 = [m[:, None, :] for m in jnp.split(jax.nn.silu(c) @ w_ada[l] + b_ada[l], 6, axis=-1)]
        csh1, csc1, cg1, csh2, csc2, cg2 = jnp.split(jax.nn.silu(c_ctx) @ w_ada[l] + b_ada[l], 6, axis=-1)
        h_lat = modulate(rms_norm(x, norm1_g[l]), sh1, sc1)
        h_ctx = modulate(rms_norm(ctx, norm1_g[l]), csh1, csc1)
        y_lat, y_ctx = parallel_mixer(h_lat, h_ctx, w_in[l], na_rel_bias[l], lb_f[l], lb_b[l], hg_norm_g[l],
                                      w_na_o[l], w_hg_o[l], w_out[l], with_ctx_out=not last)
        x = x + g1 * y_lat
        h2_lat = modulate(rms_norm(x, norm2_g[l]), sh2, sc2)
        if last:
            x = x + g2 * grouped_moe(h2_lat, w_router, router_bias, w_gate[l], w_up[l], w_down[l])
        else:
            ctx = ctx + cg1 * y_ctx
            h2_ctx = modulate(rms_norm(ctx, norm2_g[l]), csh2, csc2)
            f = grouped_moe(jnp.concatenate([h2_ctx, h2_lat], axis=1), w_router, router_bias,
                            w_gate[l], w_up[l], w_down[l])
            ctx = ctx + cg2 * f[:, :n_ctx]
            x = x + g2 * f[:, n_ctx:]
    return rms_norm(x, final_g)
```

```python
import functools

import jax
import jax.numpy as jnp
import numpy as np
from jax import lax
from jax.experimental import pallas as pl
from jax.experimental.pallas import tpu as pltpu

F32 = jnp.float32
BF16 = jnp.bfloat16

GRID_W = 64
EPS = 1e-6
NA_HEADS = 8
NA_HEAD_DIM = 64
NA_WIDTH = NA_HEADS * NA_HEAD_DIM
NA_WIN_ROWS = 8
NA_WIN_COLS = 16
ROPE_THETA = 10000.0
HG_HEADS = 4
HG_DIM = 128
HG_WIDTH = HG_HEADS * HG_DIM
GATE_FLOOR = 1e-30
N_EXPERTS = 16
N_GROUPS = 4
EXPERTS_PER_GROUP = 4
N_PAIRS = 6
N_BUCKETS = N_GROUPS * N_PAIRS
BUCKET_ROWS = 32
MASKED = -1e30

LANES = 128
ROW_TILE = 256
HG_CHUNK = 128
HG_FINE = 8
VMEM_LIMIT = 56 * 1024 * 1024

_NT = (((1,), (1,)), ((), ()))
_TN = (((0,), (0,)), ((), ()))


def _cparams(sem):
    return pltpu.CompilerParams(dimension_semantics=sem, vmem_limit_bytes=VMEM_LIMIT)


def _ada_kernel(c_ref, w_ref, b_ref, o_ref):
    cc = c_ref[...]
    s = cc * jax.nn.sigmoid(cc)
    o_ref[0] = jnp.dot(s, w_ref[0], preferred_element_type=F32) + b_ref[0]


def _ada(cc, w_ada, b_ada):
    depth, d, n = w_ada.shape
    rows = cc.shape[0]
    tn = 1536
    return pl.pallas_call(
        _ada_kernel,
        grid=(depth, n // tn),
        in_specs=[pl.BlockSpec((rows, d), lambda l, j: (0, 0)),
                  pl.BlockSpec((1, d, tn), lambda l, j: (l, 0, j)),
                  pl.BlockSpec((1, 1, tn), lambda l, j: (l, 0, j))],
        out_specs=pl.BlockSpec((1, rows, tn), lambda l, j: (l, 0, j)),
        out_shape=jax.ShapeDtypeStruct((depth, rows, n), F32),
        compiler_params=_cparams(("arbitrary", "arbitrary")),
        name="ada",
    )(cc, w_ada, b_ada.reshape(depth, 1, n))


def _inproj_kernel(x_ref, mod_ref, g_ref, w_ref, cos_ref, sin_ref,
                   q_ref, k_ref, v_ref, hq_ref, hf_ref, hb_ref, hi_ref, og_ref, ga_ref, gh_ref):
    x = x_ref[0]
    y = x * lax.rsqrt(jnp.mean(x * x, axis=-1, keepdims=True) + EPS) * g_ref[...]
    h = (y * (1.0 + mod_ref[0, 1:2, :]) + mod_ref[0, 0:1, :]).astype(BF16)

    def proj(off, width):
        return jnp.dot(h, w_ref[:, off:off + width], preferred_element_type=F32)

    cos = cos_ref[...]
    sin = sin_ref[...]
    even = (lax.broadcasted_iota(jnp.int32, cos.shape, 1) % 2) == 0

    def rope(z):
        swapped = jnp.where(even, pltpu.roll(z, LANES - 1, 1), pltpu.roll(z, 1, 1))
        return z * cos + swapped * sin

    scale = NA_HEAD_DIM ** -0.5
    for blk in range(NA_WIDTH // LANES):
        sl = slice(blk * LANES, (blk + 1) * LANES)
        q_ref[0, :, sl] = (rope(proj(blk * LANES, LANES)) * scale).astype(q_ref.dtype)
        k_ref[0, :, sl] = rope(proj(NA_WIDTH + blk * LANES, LANES)).astype(k_ref.dtype)
    off = 2 * NA_WIDTH
    v_ref[0] = proj(off, NA_WIDTH).astype(v_ref.dtype)
    off += NA_WIDTH
    hq = proj(off, HG_WIDTH)
    hq_ref[0] = hq * jax.nn.sigmoid(hq)
    off += HG_WIDTH
    hf_ref[0] = proj(off, HG_WIDTH)
    off += HG_WIDTH
    hb_ref[0] = proj(off, HG_WIDTH)
    off += HG_WIDTH
    hi_ref[0] = proj(off, HG_WIDTH).astype(hi_ref.dtype)
    off += HG_WIDTH
    og_ref[0] = proj(off, HG_WIDTH)
    off += HG_WIDTH
    d = x.shape[-1]
    for half in range(2):
        sl = slice(half * (d // 2), (half + 1) * (d // 2))
        ga_ref[0, :, sl] = proj(off + half * (d // 2), d // 2)
        gh_ref[0, :, sl] = proj(off + d + half * (d // 2), d // 2)


def _inproj(xc, mods, g, w_bf16, cos_t, sin_t, ctx_row):
    b, lt, d = xc.shape
    n_t = lt // ROW_TILE

    def rows(width):
        return pl.BlockSpec((1, ROW_TILE, width), lambda bi, j: (bi, j, 0))

    def mod_map(bi, j):
        return (jnp.where(j == 0, ctx_row, bi), 0, 0)

    widths = [NA_WIDTH] * 3 + [HG_WIDTH] * 5 + [d, d]
    dtypes = [BF16, BF16, BF16, F32, F32, F32, BF16, F32, F32, F32]
    return pl.pallas_call(
        _inproj_kernel,
        grid=(b, n_t),
        in_specs=[rows(d),
                  pl.BlockSpec((1, 6, d), mod_map),
                  pl.BlockSpec((1, d), lambda bi, j: (0, 0)),
                  pl.BlockSpec(w_bf16.shape, lambda bi, j: (0, 0)),
                  pl.BlockSpec((ROW_TILE, LANES), lambda bi, j: (j, 0)),
                  pl.BlockSpec((ROW_TILE, LANES), lambda bi, j: (j, 0))],
        out_specs=[rows(w) for w in widths],
        out_shape=[jax.ShapeDtypeStruct((b, lt, w), dt) for w, dt in zip(widths, dtypes)],
        compiler_params=_cparams(("arbitrary", "arbitrary")),
        name="inproj",
    )(xc, mods, g, w_bf16, cos_t, sin_t)


def _rope_tables(n_ctx, seq):
    t = jnp.arange(seq, dtype=jnp.int32)
    row = (t // GRID_W).astype(F32)
    col = (t % GRID_W).astype(F32)
    rot_half = NA_HEAD_DIM // 2
    inv = ROPE_THETA ** (-jnp.arange(0, rot_half, 2, dtype=F32) / rot_half)
    ang = jnp.concatenate([row[:, None] * inv, col[:, None] * inv], axis=-1)
    cos = jnp.repeat(jnp.cos(ang), 2, axis=-1)
    sin = jnp.repeat(jnp.sin(ang), 2, axis=-1)
    sign = jnp.asarray(np.tile(np.array([-1.0, 1.0], np.float32), NA_HEAD_DIM // 2))
    sin = sin * sign
    cos = jnp.concatenate([jnp.ones((n_ctx, NA_HEAD_DIM), F32), cos], axis=0)
    sin = jnp.concatenate([jnp.zeros((n_ctx, NA_HEAD_DIM), F32), sin], axis=0)
    reps = LANES // NA_HEAD_DIM
    return jnp.tile(cos, (1, reps)), jnp.tile(sin, (1, reps))


def _na_bias_table(rel_bias):
    h = rel_bias.shape[0]
    rr = np.arange(NA_WIN_ROWS)[:, None, None, None]
    col = np.arange(GRID_W)[None, :, None, None]
    j = np.arange(NA_WIN_ROWS)[None, None, :, None]
    kc = np.arange(GRID_W)[None, None, None, :]
    cs = np.clip(col - NA_WIN_COLS // 2, 0, GRID_W - NA_WIN_COLS)
    shape = (NA_WIN_ROWS, GRID_W, NA_WIN_ROWS, GRID_W)
    valid = np.broadcast_to((kc >= cs) & (kc < cs + NA_WIN_COLS), shape)
    dr = np.broadcast_to(j - rr + NA_WIN_ROWS - 1, shape)
    dc = np.broadcast_to(np.clip(kc - col + NA_WIN_COLS - 1, 0, 2 * NA_WIN_COLS - 2), shape)
    tbl = rel_bias.astype(F32)[:, dr, dc]
    tbl = jnp.where(valid[None], tbl, MASKED)
    tbl = jnp.transpose(tbl, (1, 0, 2, 3, 4)).reshape(NA_WIN_ROWS, h, GRID_W, NA_WIN_ROWS * GRID_W)
    none = jnp.full((1,) + tbl.shape[1:], MASKED, F32)
    return jnp.concatenate([tbl, none], axis=0)


def _na_kernel(q_ref, k_ref, v_ref, t_ref, o_ref, *, tile_off, n_ctx, grid_rows):
    j = pl.program_id(1) + tile_off
    r = j - n_ctx // GRID_W
    rs = jnp.clip(r - NA_WIN_ROWS // 2, 0, grid_rows - NA_WIN_ROWS)
    start = pl.multiple_of(n_ctx + rs * GRID_W, GRID_W)
    win = NA_WIN_ROWS * GRID_W
    outs = []
    for h in range(NA_HEADS):
        sl = slice(h * NA_HEAD_DIM, (h + 1) * NA_HEAD_DIM)
        q = q_ref[0, :, sl]
        kw = k_ref[0, pl.ds(start, win), sl]
        vw = v_ref[0, pl.ds(start, win), sl]
        kc = k_ref[0, 0:n_ctx, sl]
        vc = v_ref[0, 0:n_ctx, sl]
        sw = lax.dot_general(q, kw, _NT, preferred_element_type=F32) + t_ref[0, h]
        sc = lax.dot_general(q, kc, _NT, preferred_element_type=F32)
        m = jnp.maximum(jnp.max(sw, axis=-1, keepdims=True), jnp.max(sc, axis=-1, keepdims=True))
        pw = jnp.exp(sw - m)
        pc = jnp.exp(sc - m)
        den = jnp.sum(pw, axis=-1, keepdims=True) + jnp.sum(pc, axis=-1, keepdims=True)
        o = (jnp.dot(pw.astype(BF16), vw, preferred_element_type=F32)
             + jnp.dot(pc.astype(BF16), vc, preferred_element_type=F32))
        outs.append(o / den)
    o_ref[0] = jnp.concatenate(outs, axis=-1)


def _na_attention(q, k, v, table, n_ctx, with_ctx):
    b, lt, _ = q.shape
    grid_rows = (lt - n_ctx) // GRID_W
    ctx_tiles = n_ctx // GRID_W
    tile_off = 0 if with_ctx else ctx_tiles
    n_steps = lt // GRID_W - tile_off

    def tbl_map(bi, j):
        r = j + tile_off - ctx_tiles
        rs = jnp.clip(r - NA_WIN_ROWS // 2, 0, grid_rows - NA_WIN_ROWS)
        return (jnp.where(r < 0, NA_WIN_ROWS, r - rs), 0, 0, 0)

    kern = functools.partial(_na_kernel, tile_off=tile_off, n_ctx=n_ctx, grid_rows=grid_rows)
    return pl.pallas_call(
        kern,
        grid=(b, n_steps),
        in_specs=[pl.BlockSpec((1, GRID_W, NA_WIDTH), lambda bi, j: (bi, j + tile_off, 0)),
                  pl.BlockSpec((1, lt, NA_WIDTH), lambda bi, j: (bi, 0, 0)),
                  pl.BlockSpec((1, lt, NA_WIDTH), lambda bi, j: (bi, 0, 0)),
                  pl.BlockSpec((1, NA_HEADS, GRID_W, NA_WIN_ROWS * GRID_W), tbl_map)],
        out_specs=pl.BlockSpec((1, GRID_W, NA_WIDTH), lambda bi, j: (bi, j, 0)),
        out_shape=jax.ShapeDtypeStruct((b, n_steps * GRID_W, NA_WIDTH), F32),
        compiler_params=_cparams(("arbitrary", "arbitrary")),
        name="na_attention",
    )(q, k, v, table)


def _split3(x):
    hi = x.astype(BF16)
    r1 = x - hi.astype(F32)
    mid = r1.astype(BF16)
    lo = (r1 - mid.astype(F32)).astype(BF16)
    return hi, mid, lo


def _hgrn_chunk(q, f, v, lb, st, rev):
    c = q.shape[0]
    one_m_lb = 1.0 - lb
    g = lb + one_m_lb * jax.nn.sigmoid(f)
    lg = jnp.log(jnp.maximum(g, GATE_FLOOR))
    kk = one_m_lb * jax.nn.sigmoid(-f)

    t_i = lax.broadcasted_iota(jnp.int32, (c, c), 0)
    s_i = lax.broadcasted_iota(jnp.int32, (c, c), 1)
    tri = jnp.where((s_i >= t_i) if rev else (s_i <= t_i), 1.0, 0.0).astype(BF16)
    hi, mid, lo = _split3(lg)
    b = (jnp.dot(tri, hi, preferred_element_type=F32) + jnp.dot(tri, mid, preferred_element_type=F32)
         + jnp.dot(tri, lo, preferred_element_type=F32))
    b_last = b[0:1] if rev else b[c - 1:c]

    vb = v.astype(BF16)
    qe = (q * jnp.exp(b)).astype(BF16)
    o = lax.dot_general(qe, st.astype(BF16), _NT, preferred_element_type=F32)
    ke = (kk * jnp.exp(b_last - b)).astype(BF16)
    st_new = st * jnp.exp(b_last) + lax.dot_general(vb, ke, _TN, preferred_element_type=F32)

    row = lax.broadcasted_iota(jnp.int32, (c, 1), 0)
    xor = t_i ^ s_i
    a = jnp.zeros((c, c), F32)
    w = c // 2
    while w >= HG_FINE:
        parts = []
        for blk in range(c // (2 * w)):
            ref_row = blk * 2 * w + (w if rev else w - 1)
            parts.append(jnp.broadcast_to(b[ref_row:ref_row + 1], (2 * w, b.shape[1])))
        r_b = parts[0] if len(parts) == 1 else jnp.concatenate(parts, axis=0)
        upper = (row % (2 * w)) >= w
        is_q = jnp.logical_not(upper) if rev else upper
        e = jnp.exp(jnp.where(is_q, b - r_b, r_b - b))
        qw = jnp.where(is_q, q * e, 0.0).astype(BF16)
        kw = jnp.where(is_q, 0.0, kk * e).astype(BF16)
        p = lax.dot_general(qw, kw, _NT, preferred_element_type=F32)
        a = a + jnp.where((xor >= w) & (xor < 2 * w), p, 0.0)
        w //= 2
    o = o + jnp.dot(a.astype(BF16), vb, preferred_element_type=F32)

    pos = row % HG_FINE
    o = o + jnp.sum(q * kk, axis=-1, keepdims=True) * v
    for d in range(1, HG_FINE):
        sh = (c - d) if rev else d
        bs = pltpu.roll(b, sh, 0)
        ks = pltpu.roll(kk, sh, 0)
        vs = pltpu.roll(v, sh, 0)
        valid = (pos + d <= HG_FINE - 1) if rev else (pos >= d)
        diff = jnp.where(valid, b - bs, 0.0)
        a_d = jnp.sum(q * ks * jnp.exp(diff), axis=-1, keepdims=True)
        o = o + jnp.where(valid, a_d, 0.0) * vs
    return o, st_new


def _hgrn_kernel(hq_ref, hf_ref, hb_ref, hi_ref, lbf_ref, lbb_ref, o_ref, sf_ref, sb_ref, ob_ref,
                 *, n_ctx_chunks, n_chunks):
    sf_ref[...] = jnp.zeros_like(sf_ref)
    sb_ref[...] = jnp.zeros_like(sb_ref)
    lbf = lbf_ref[0]
    lbb = lbb_ref[0]

    def step(cf, cb):
        rf = pl.multiple_of(cf * HG_CHUNK, HG_CHUNK)
        rb = pl.multiple_of(cb * HG_CHUNK, HG_CHUNK)
        of, sf = _hgrn_chunk(hq_ref[0, pl.ds(rf, HG_CHUNK), :], hf_ref[0, pl.ds(rf, HG_CHUNK), :],
                             hi_ref[0, pl.ds(rf, HG_CHUNK), :].astype(F32), lbf, sf_ref[...], False)
        ob, sb = _hgrn_chunk(hq_ref[0, pl.ds(rb, HG_CHUNK), :], hb_ref[0, pl.ds(rb, HG_CHUNK), :],
                             hi_ref[0, pl.ds(rb, HG_CHUNK), :].astype(F32), lbb, sb_ref[...], True)
        sf_ref[...] = sf
        sb_ref[...] = sb
        o_ref[0, pl.ds(rf, HG_CHUNK), :] = of
        ob_ref[pl.ds(rb, HG_CHUNK), :] = ob

    def ctx_body(i, carry):
        step(i, n_ctx_chunks - 1 - i)
        return carry

    def lat_body(i, carry):
        step(n_ctx_chunks + i, n_chunks - 1 - i)
        return carry

    lax.fori_loop(0, n_ctx_chunks, ctx_body, 0)
    lax.fori_loop(0, n_chunks - n_ctx_chunks, lat_body, 0)
    o_ref[0] = o_ref[0] + ob_ref[...]


def _hgrn(hq, hf, hb, hi, lb_f, lb_b, n_ctx):
    b, lt, _ = hq.shape
    kern = functools.partial(_hgrn_kernel, n_ctx_chunks=n_ctx // HG_CHUNK, n_chunks=lt // HG_CHUNK)
    seq = pl.BlockSpec((1, lt, HG_DIM), lambda bi, h: (bi, 0, h))
    lbs = pl.BlockSpec((1, 1, HG_DIM), lambda bi, h: (h, 0, 0))
    return pl.pallas_call(
        kern,
        grid=(b, HG_HEADS),
        in_specs=[seq, seq, seq, seq, lbs, lbs],
        out_specs=seq,
        out_shape=jax.ShapeDtypeStruct((b, lt, HG_WIDTH), F32),
        scratch_shapes=[pltpu.VMEM((HG_DIM, HG_DIM), F32), pltpu.VMEM((HG_DIM, HG_DIM), F32),
                        pltpu.VMEM((lt, HG_DIM), F32)],
        compiler_params=_cparams(("arbitrary", "arbitrary")),
        name="hgrn2",
    )(hq, hf, hb, hi, lb_f.reshape(HG_HEADS, 1, HG_DIM), lb_b.reshape(HG_HEADS, 1, HG_DIM))


def _split2(x):
    hi = x.astype(BF16)
    lo = (x - hi.astype(F32)).astype(BF16)
    return hi, lo


def _route(logits, bias):
    biased = jax.nn.sigmoid(logits) + bias
    rows = [biased[e:e + 1] for e in range(N_EXPERTS)]
    n = EXPERTS_PER_GROUP
    best = None
    g_sel = None
    for g in range(N_GROUPS):
        gs = None
        for i in range(n):
            for j2 in range(i + 1, n):
                pair = rows[g * n + i] + rows[g * n + j2]
                gs = pair if gs is None else jnp.maximum(gs, pair)
        if best is None:
            best, g_sel = gs, jnp.zeros(gs.shape, jnp.int32)
        else:
            take = gs > best
            best = jnp.where(take, gs, best)
            g_sel = jnp.where(take, g, g_sel)
    cand = []
    for i in range(n):
        c_i = rows[(N_GROUPS - 1) * n + i]
        for g in range(N_GROUPS - 2, -1, -1):
            c_i = jnp.where(g_sel == g, rows[g * n + i], c_i)
        cand.append(c_i)
    m1, i1 = cand[0], jnp.zeros(best.shape, jnp.int32)
    for i in range(1, n):
        take = cand[i] > m1
        m1 = jnp.where(take, cand[i], m1)
        i1 = jnp.where(take, i, i1)
    m2 = jnp.full(best.shape, -jnp.inf, F32)
    i2 = jnp.zeros(best.shape, jnp.int32)
    for i in range(n):
        take = (i1 != i) & (cand[i] > m2)
        m2 = jnp.where(take, cand[i], m2)
        i2 = jnp.where(take, i, i2)
    lo = jnp.minimum(i1, i2)
    hi = jnp.maximum(i1, i2)
    pair = jnp.where(lo == 0, hi - 1, jnp.where(lo == 1, hi + 1, N_PAIRS - 1))
    return g_sel * N_PAIRS + pair


def _merge_kernel(ona_ref, ohg_ref, og_ref, ga_ref, gh_ref, x_ref, mod_ref, hgg_ref, n2g_ref,
                  wna_ref, whg_ref, wout_ref, wr_ref, rb_ref,
                  xn_ref, h2_ref, bucket_ref, rank_ref, cnt_ref, carry_ref):
    first = (pl.program_id(0) == 0) & (pl.program_id(1) == 0)

    @pl.when(first)
    def _():
        carry_ref[...] = jnp.zeros_like(carry_ref)

    ohg = ohg_ref[0]
    og = og_ref[0]
    gain = hgg_ref[...]
    heads = []
    for h in range(HG_HEADS):
        sl = slice(h * HG_DIM, (h + 1) * HG_DIM)
        oh = ohg[:, sl]
        yh = oh * lax.rsqrt(jnp.mean(oh * oh, axis=-1, keepdims=True) + EPS) * gain
        gt = og[:, sl]
        heads.append((yh * (gt * jax.nn.sigmoid(gt))).astype(BF16))
    hn = jnp.concatenate(heads, axis=-1)
    y_na = jnp.dot(ona_ref[0].astype(BF16), wna_ref[...], preferred_element_type=F32)
    y_hg = jnp.dot(hn, whg_ref[...], preferred_element_type=F32)
    m = jax.nn.sigmoid(ga_ref[0]) * y_na + jax.nn.sigmoid(gh_ref[0]) * y_hg
    y = jnp.dot(m.astype(BF16), wout_ref[...], preferred_element_type=F32)
    xn = x_ref[0] + mod_ref[0, 2:3, :] * y
    xn_ref[0] = xn
    yn = xn * lax.rsqrt(jnp.mean(xn * xn, axis=-1, keepdims=True) + EPS) * n2g_ref[...]
    h2 = yn * (1.0 + mod_ref[0, 4:5, :]) + mod_ref[0, 3:4, :]
    h2_ref[...] = h2

    w_hi, w_lo = _split2(wr_ref[...])
    h_hi, h_lo = _split2(h2)
    logits = (lax.dot_general(w_hi, h_hi, _NT, preferred_element_type=F32)
              + lax.dot_general(w_hi, h_lo, _NT, preferred_element_type=F32)
              + lax.dot_general(w_lo, h_hi, _NT, preferred_element_type=F32))
    bucket = _route(logits, rb_ref[...])
    bucket_ref[0] = bucket

    t = bucket.shape[1]
    onehot = (lax.broadcasted_iota(jnp.int32, (BUCKET_ROWS, t), 0) == bucket).astype(F32)
    before = (lax.broadcasted_iota(jnp.int32, (t, t), 0)
              < lax.broadcasted_iota(jnp.int32, (t, t), 1)).astype(BF16)
    prefix = jnp.dot(onehot.astype(BF16), before, preferred_element_type=F32)
    carry = carry_ref[...]
    rank = jnp.sum(onehot * (prefix + carry[:, 0:1]), axis=0, keepdims=True)
    rank_ref[0] = rank.astype(jnp.int32)
    carry = carry + jnp.sum(onehot, axis=1, keepdims=True)
    carry_ref[...] = carry
    cnt_ref[...] = carry.astype(jnp.int32)


def _merge(o_na, o_hg, og, ga, gh, xc, mods, hg_gain, n2_gain, w_na_o, w_hg_o, w_out, w_router_t,
           router_bias, ctx_row, skip):
    b, lt, d = xc.shape
    n_t = lt // ROW_TILE - skip

    def rows(width):
        return pl.BlockSpec((1, ROW_TILE, width), lambda bi, j: (bi, j + skip, 0))

    def full(a):
        return pl.BlockSpec(a.shape, lambda bi, j: (0,) * a.ndim)

    def mod_map(bi, j):
        return (jnp.where(j + skip == 0, ctx_row, bi), 0, 0)

    tok = pl.BlockSpec((1, 1, ROW_TILE), lambda bi, j: (bi * n_t + j, 0, 0))
    consts = [hg_gain, n2_gain, w_na_o, w_hg_o, w_out, w_router_t, router_bias]
    return pl.pallas_call(
        _merge_kernel,
        grid=(b, n_t),
        in_specs=[pl.BlockSpec((1, ROW_TILE, NA_WIDTH), lambda bi, j: (bi, j, 0)),
                  rows(HG_WIDTH), rows(HG_WIDTH), rows(d), rows(d), rows(d),
                  pl.BlockSpec((1, 6, d), mod_map)] + [full(a) for a in consts],
        out_specs=[pl.BlockSpec((1, ROW_TILE, d), lambda bi, j: (bi, j, 0)),
                   pl.BlockSpec((ROW_TILE, d), lambda bi, j: (bi * n_t + j, 0)),
                   tok, tok,
                   pl.BlockSpec((BUCKET_ROWS, LANES), lambda bi, j: (0, 0))],
        out_shape=[jax.ShapeDtypeStruct((b, n_t * ROW_TILE, d), F32),
                   jax.ShapeDtypeStruct((b * n_t * ROW_TILE, d), F32),
                   jax.ShapeDtypeStruct((b * n_t, 1, ROW_TILE), jnp.int32),
                   jax.ShapeDtypeStruct((b * n_t, 1, ROW_TILE), jnp.int32),
                   jax.ShapeDtypeStruct((BUCKET_ROWS, LANES), jnp.int32)],
        scratch_shapes=[pltpu.VMEM((BUCKET_ROWS, LANES), F32)],
        compiler_params=_cparams(("arbitrary", "arbitrary")),
        name="merge_router",
    )(o_na, o_hg, og, ga, gh, xc, mods, *consts)


def _scatter_kernel(dest_ref, h_ref, xs_in_ref, xs_ref, sem):
    del xs_in_ref
    n = h_ref.shape[0]

    def row_copy(r):
        return pltpu.make_async_copy(h_ref.at[pl.ds(r, 1)], xs_ref.at[pl.ds(dest_ref[0, 0, r], 1)], sem)

    def issue(r, carry):
        row_copy(r).start()
        return carry

    def drain(r, carry):
        row_copy(r).wait()
        return carry

    lax.fori_loop(0, n, issue, 0)
    lax.fori_loop(0, n, drain, 0)


def _scatter_rows(h2, dest, n_sorted):
    t, d = h2.shape
    n_t = t // ROW_TILE
    return pl.pallas_call(
        _scatter_kernel,
        grid=(n_t,),
        in_specs=[pl.BlockSpec((1, 1, ROW_TILE), lambda i: (i, 0, 0), memory_space=pltpu.SMEM),
                  pl.BlockSpec((ROW_TILE, d), lambda i: (i, 0)),
                  pl.BlockSpec(memory_space=pl.ANY)],
        out_specs=pl.BlockSpec(memory_space=pl.ANY),
        out_shape=jax.ShapeDtypeStruct((n_sorted, d), F32),
        scratch_shapes=[pltpu.SemaphoreType.DMA(())],
        input_output_aliases={2: 0},
        compiler_params=_cparams(("arbitrary",)),
        name="scatter_rows",
    )(dest.reshape(n_t, 1, ROW_TILE), h2, jnp.zeros((n_sorted, d), F32))


def _final_kernel(dest_ref, xn_ref, mod_ref, g_ref, ys_ref, o_ref, buf_ref, sem, *, final_norm):
    n = buf_ref.shape[0]

    def row_copy(r):
        return pltpu.make_async_copy(ys_ref.at[pl.ds(dest_ref[0, 0, r], 1)], buf_ref.at[pl.ds(r, 1)], sem)

    def issue(r, carry):
        row_copy(r).start()
        return carry

    def drain(r, carry):
        row_copy(r).wait()
        return carry

    lax.fori_loop(0, n, issue, 0)
    lax.fori_loop(0, n, drain, 0)
    x = xn_ref[0] + mod_ref[0, 5:6, :] * buf_ref[...]
    if final_norm:
        x = x * lax.rsqrt(jnp.mean(x * x, axis=-1, keepdims=True) + EPS) * g_ref[...]
    o_ref[0] = x


def _gather_residual(xn, mods, ys, dest, gain, ctx_row, final_norm, skip):
    b, lt, d = xn.shape
    n_t = lt // ROW_TILE

    def mod_map(bi, j):
        return (jnp.where(j + skip == 0, ctx_row, bi), 0, 0)

    kern = functools.partial(_final_kernel, final_norm=final_norm)
    return pl.pallas_call(
        kern,
        grid=(b, n_t),
        in_specs=[pl.BlockSpec((1, 1, ROW_TILE), lambda bi, j: (bi * n_t + j, 0, 0),
                               memory_space=pltpu.SMEM),
                  pl.BlockSpec((1, ROW_TILE, d), lambda bi, j: (bi, j, 0)),
                  pl.BlockSpec((1, 6, d), mod_map),
                  pl.BlockSpec((1, d), lambda bi, j: (0, 0)),
                  pl.BlockSpec(memory_space=pl.ANY)],
        out_specs=pl.BlockSpec((1, ROW_TILE, d), lambda bi, j: (bi, j, 0)),
        out_shape=jax.ShapeDtypeStruct((b, lt, d), F32),
        scratch_shapes=[pltpu.VMEM((ROW_TILE, d), F32), pltpu.SemaphoreType.DMA(())],
        compiler_params=_cparams(("arbitrary", "arbitrary")),
        name="gather_residual",
    )(dest.reshape(b * n_t, 1, ROW_TILE), xn, mods, gain, ys)


def _moe_kernel(ea_ref, eb_ref, nused_ref, xs_ref, wra_ref, wrb_ref,
                wga_ref, wua_ref, wda_ref, wgb_ref, wub_ref, wdb_ref, ys_ref):
    del ea_ref, eb_ref
    i = pl.program_id(0)

    @pl.when(i < nused_ref[0])
    def _():
        x = xs_ref[...]
        s_a = jax.nn.sigmoid(jnp.sum(x * wra_ref[0], axis=-1, keepdims=True))
        s_b = jax.nn.sigmoid(jnp.sum(x * wrb_ref[0], axis=-1, keepdims=True))
        tot = s_a + s_b
        xb = x.astype(BF16)

        def expert(wg, wu, wd):
            gate = jnp.dot(xb, wg[0], preferred_element_type=F32)
            up = jnp.dot(xb, wu[0], preferred_element_type=F32)
            hid = (gate * jax.nn.sigmoid(gate) * up).astype(BF16)
            return jnp.dot(hid, wd[0], preferred_element_type=F32)

        ys_ref[...] = ((s_a / tot) * expert(wga_ref, wua_ref, wda_ref)
                       + (s_b / tot) * expert(wgb_ref, wub_ref, wdb_ref))

    @pl.when(i >= nused_ref[0])
    def _():
        ys_ref[...] = jnp.zeros_like(ys_ref)


def _moe(xs, tile_ea, tile_eb, n_used, w_router_rows, w_gate, w_up, w_down):
    n_sorted, d = xs.shape
    n_tiles = n_sorted // ROW_TILE
    ff = w_gate.shape[-1]

    def by_a(i, ea, eb, nu):
        return (ea[i], 0, 0)

    def by_b(i, ea, eb, nu):
        return (eb[i], 0, 0)

    grid_spec = pltpu.PrefetchScalarGridSpec(
        num_scalar_prefetch=3,
        grid=(n_tiles,),
        in_specs=[pl.BlockSpec((ROW_TILE, d), lambda i, ea, eb, nu: (i, 0)),
                  pl.BlockSpec((1, 1, d), by_a), pl.BlockSpec((1, 1, d), by_b),
                  pl.BlockSpec((1, d, ff), by_a), pl.BlockSpec((1, d, ff), by_a),
                  pl.BlockSpec((1, ff, d), by_a),
                  pl.BlockSpec((1, d, ff), by_b), pl.BlockSpec((1, d, ff), by_b),
                  pl.BlockSpec((1, ff, d), by_b)],
        out_specs=pl.BlockSpec((ROW_TILE, d), lambda i, ea, eb, nu: (i, 0)),
    )
    return pl.pallas_call(
        _moe_kernel,
        grid_spec=grid_spec,
        out_shape=jax.ShapeDtypeStruct((n_sorted, d), F32),
        compiler_params=_cparams(("arbitrary",)),
        name="moe_pairs",
    )(tile_ea, tile_eb, n_used, xs, w_router_rows, w_router_rows,
      w_gate, w_up, w_down, w_gate, w_up, w_down)


_PAIR_LO = np.array([0, 0, 0, 1, 1, 2], np.int32)
_PAIR_HI = np.array([1, 2, 3, 2, 3, 3], np.int32)


def _sorted_layout(bucket, rank, counts, n_tiles):
    counts = counts[:N_BUCKETS]
    padded = ((counts + ROW_TILE - 1) // ROW_TILE) * ROW_TILE
    ends = jnp.cumsum(padded)
    starts = ends - padded
    dest = starts[bucket] + rank
    tile_start = jnp.arange(n_tiles, dtype=jnp.int32) * ROW_TILE
    tile_bucket = jnp.minimum(jnp.searchsorted(ends, tile_start, side="right"), N_BUCKETS - 1).astype(jnp.int32)
    group = tile_bucket // N_PAIRS
    pair = tile_bucket % N_PAIRS
    ea = group * EXPERTS_PER_GROUP + jnp.asarray(_PAIR_LO)[pair]
    eb = group * EXPERTS_PER_GROUP + jnp.asarray(_PAIR_HI)[pair]
    n_used = (ends[-1] // ROW_TILE).astype(jnp.int32).reshape(1)
    return dest.astype(jnp.int32), ea.astype(jnp.int32), eb.astype(jnp.int32), n_used


def _lower_bounds(raw):
    p = jax.nn.softmax(raw.astype(F32), axis=0)
    return jnp.cumsum(p, axis=0) - p[0:1]


def kernel(x, c, ctx, c_ctx, w_ada, b_ada, norm1_g, w_in, na_rel_bias, hg_lower_fwd, hg_lower_bwd,
           hg_norm_g, w_na_o, w_hg_o, w_out, norm2_g, w_router, router_bias, w_gate, w_up, w_down,
           final_g):
    b, seq, d = x.shape
    n_ctx = ctx.shape[1]
    depth = w_ada.shape[0]
    lt = n_ctx + seq
    assert n_ctx % ROW_TILE == 0 and seq % ROW_TILE == 0 and seq % GRID_W == 0
    assert seq // GRID_W >= NA_WIN_ROWS and n_ctx % HG_CHUNK == 0 and seq % HG_CHUNK == 0

    ada_rows = -(-(b + 1) // 8) * 8
    ctx_row = b
    cc = jnp.zeros((ada_rows, d), F32).at[:b].set(c).at[b].set(c_ctx)
    mods = _ada(cc, w_ada, b_ada).reshape(depth, ada_rows, 6, d)

    lb_f = _lower_bounds(hg_lower_fwd)
    lb_b = _lower_bounds(hg_lower_bwd)
    cos_t, sin_t = _rope_tables(n_ctx, seq)
    w_router_t = jnp.transpose(w_router)
    w_router_rows = w_router_t.reshape(N_EXPERTS, 1, d)
    rb = router_bias.astype(F32).reshape(N_EXPERTS, 1)

    xc = jnp.concatenate([ctx, x], axis=1)
    out = None
    for l in range(depth):
        last = l == depth - 1
        skip = n_ctx // ROW_TILE if last else 0
        n_tiles = b * (lt // ROW_TILE - skip) + N_BUCKETS
        q, k, v, hq, hf, hb, hi, og, ga, gh = _inproj(
            xc, mods[l], norm1_g[l].reshape(1, d), w_in[l].astype(BF16), cos_t, sin_t, ctx_row)
        o_na = _na_attention(q, k, v, _na_bias_table(na_rel_bias[l]), n_ctx, with_ctx=not last)
        o_hg = _hgrn(hq, hf, hb, hi, lb_f[l], lb_b[l], n_ctx)
        xn, h2, bucket, rank, counts = _merge(
            o_na, o_hg, og, ga, gh, xc, mods[l], hg_norm_g[l].reshape(1, HG_DIM),
            norm2_g[l].reshape(1, d), w_na_o[l].astype(BF16), w_hg_o[l].astype(BF16),
            w_out[l].astype(BF16), w_router_t, rb, ctx_row, skip)
        dest, tile_ea, tile_eb, n_used = _sorted_layout(
            bucket.reshape(-1), rank.reshape(-1), counts[:, 0], n_tiles)
        xs = _scatter_rows(h2, dest, n_tiles * ROW_TILE)
        ys = _moe(xs, tile_ea, tile_eb, n_used, w_router_rows,
                  w_gate[l].astype(BF16), w_up[l].astype(BF16), w_down[l].astype(BF16))
        res = _gather_residual(xn, mods[l], ys, dest, final_g.reshape(1, d), ctx_row, last, skip)
        if last:
            out = res
        else:
            xc = res
    return out
```

```python
import functools

import jax
import jax.numpy as jnp
import numpy as np
from jax import lax
from jax.experimental import pallas as pl
from jax.experimental.pallas import tpu as pltpu

F32 = jnp.float32
BF16 = jnp.bfloat16

GRID_W = 64
EPS = 1e-6
NA_HEADS = 8
NA_HEAD_DIM = 64
NA_WIDTH = NA_HEADS * NA_HEAD_DIM
NA_WIN_ROWS = 8
NA_WIN_COLS = 16
ROPE_THETA = 10000.0
HG_HEADS = 4
HG_DIM = 128
HG_WIDTH = HG_HEADS * HG_DIM
GATE_FLOOR = 1e-30
N_EXPERTS = 16
N_GROUPS = 4
EXPERTS_PER_GROUP = 4
N_PAIRS = 6
N_BUCKETS = N_GROUPS * N_PAIRS
BUCKET_ROWS = 32
MASKED = -1e30
LOG2E = 1.4426950408889634

LANES = 128
ROW_TILE = 256
NA_Q_ROWS = 2
SUBLANES = 8
HG_CHUNK = 128
HG_UNROLL = 2
ISSUE_UNROLL = 8
VMEM_LIMIT = 56 * 1024 * 1024

_NT = (((1,), (1,)), ((), ()))
_TN = (((0,), (0,)), ((), ()))


def _cparams(sem):
    return pltpu.CompilerParams(dimension_semantics=sem, vmem_limit_bytes=VMEM_LIMIT)


def _ada_kernel(c_ref, w_ref, b_ref, o_ref):
    cc = c_ref[...]
    s = cc * jax.nn.sigmoid(cc)
    o_ref[0] = jnp.dot(s, w_ref[0], preferred_element_type=F32) + b_ref[0]


def _ada(cc, w_ada, b_ada):
    depth, d, n = w_ada.shape
    rows = cc.shape[0]
    tn = 1536
    return pl.pallas_call(
        _ada_kernel,
        grid=(depth, n // tn),
        in_specs=[pl.BlockSpec((rows, d), lambda l, j: (0, 0)),
                  pl.BlockSpec((1, d, tn), lambda l, j: (l, 0, j)),
                  pl.BlockSpec((1, 1, tn), lambda l, j: (l, 0, j))],
        out_specs=pl.BlockSpec((1, rows, tn), lambda l, j: (l, 0, j)),
        out_shape=jax.ShapeDtypeStruct((depth, rows, n), F32),
        compiler_params=_cparams(("arbitrary", "arbitrary")),
        name="ada",
    )(cc, w_ada, b_ada.reshape(depth, 1, n))


def _inproj_kernel(x_ref, mod_ref, g_ref, w_ref, cos_ref, sin_ref,
                   q_ref, k_ref, v_ref, hq_ref, hf_ref, hb_ref, hi_ref, og_ref, ga_ref, gh_ref):
    x = x_ref[0]
    y = x * lax.rsqrt(jnp.mean(x * x, axis=-1, keepdims=True) + EPS) * g_ref[...]
    h = (y * (1.0 + mod_ref[0, 1:2, :]) + mod_ref[0, 0:1, :]).astype(BF16)

    def proj(off, width):
        return jnp.dot(h, w_ref[:, off:off + width], preferred_element_type=F32)

    cos = cos_ref[...]
    sin = sin_ref[...]
    even = (lax.broadcasted_iota(jnp.int32, cos.shape, 1) % 2) == 0

    def rope(z):
        swapped = jnp.where(even, pltpu.roll(z, LANES - 1, 1), pltpu.roll(z, 1, 1))
        return z * cos + swapped * sin

    scale = NA_HEAD_DIM ** -0.5 * LOG2E
    for blk in range(NA_WIDTH // LANES):
        sl = slice(blk * LANES, (blk + 1) * LANES)
        q_ref[0, :, sl] = (rope(proj(blk * LANES, LANES)) * scale).astype(q_ref.dtype)
        k_ref[0, :, sl] = rope(proj(NA_WIDTH + blk * LANES, LANES)).astype(k_ref.dtype)
    off = 2 * NA_WIDTH
    v_ref[0] = proj(off, NA_WIDTH).astype(v_ref.dtype)
    off += NA_WIDTH
    hq = proj(off, HG_WIDTH)
    hq_ref[0] = hq * jax.nn.sigmoid(hq)
    off += HG_WIDTH
    hf_ref[0] = proj(off, HG_WIDTH)
    off += HG_WIDTH
    hb_ref[0] = proj(off, HG_WIDTH)
    off += HG_WIDTH
    hi_ref[0] = proj(off, HG_WIDTH).astype(hi_ref.dtype)
    off += HG_WIDTH
    og_ref[0] = proj(off, HG_WIDTH)
    off += HG_WIDTH
    d = x.shape[-1]
    for half in range(2):
        sl = slice(half * (d // 2), (half + 1) * (d // 2))
        ga_ref[0, :, sl] = proj(off + half * (d // 2), d // 2)
        gh_ref[0, :, sl] = proj(off + d + half * (d // 2), d // 2)


def _inproj(xc, mods, g, w_bf16, cos_t, sin_t, ctx_row):
    b, lt, d = xc.shape
    n_t = lt // ROW_TILE

    def rows(width):
        return pl.BlockSpec((1, ROW_TILE, width), lambda bi, j: (bi, j, 0))

    def mod_map(bi, j):
        return (jnp.where(j == 0, ctx_row, bi), 0, 0)

    widths = [NA_WIDTH] * 3 + [HG_WIDTH] * 5 + [d, d]
    dtypes = [BF16, BF16, BF16, F32, F32, F32, BF16, F32, F32, F32]
    return pl.pallas_call(
        _inproj_kernel,
        grid=(b, n_t),
        in_specs=[rows(d),
                  pl.BlockSpec((1, 6, d), mod_map),
                  pl.BlockSpec((1, d), lambda bi, j: (0, 0)),
                  pl.BlockSpec(w_bf16.shape, lambda bi, j: (0, 0)),
                  pl.BlockSpec((ROW_TILE, LANES), lambda bi, j: (j, 0)),
                  pl.BlockSpec((ROW_TILE, LANES), lambda bi, j: (j, 0))],
        out_specs=[rows(w) for w in widths],
        out_shape=[jax.ShapeDtypeStruct((b, lt, w), dt) for w, dt in zip(widths, dtypes)],
        compiler_params=_cparams(("arbitrary", "arbitrary")),
        name="inproj",
    )(xc, mods, g, w_bf16, cos_t, sin_t)


def _rope_tables(n_ctx, seq):
    t = jnp.arange(seq, dtype=jnp.int32)
    row = (t // GRID_W).astype(F32)
    col = (t % GRID_W).astype(F32)
    rot_half = NA_HEAD_DIM // 2
    inv = ROPE_THETA ** (-jnp.arange(0, rot_half, 2, dtype=F32) / rot_half)
    ang = jnp.concatenate([row[:, None] * inv, col[:, None] * inv], axis=-1)
    cos = jnp.repeat(jnp.cos(ang), 2, axis=-1)
    sin = jnp.repeat(jnp.sin(ang), 2, axis=-1)
    sign = jnp.asarray(np.tile(np.array([-1.0, 1.0], np.float32), NA_HEAD_DIM // 2))
    sin = sin * sign
    cos = jnp.concatenate([jnp.ones((n_ctx, NA_HEAD_DIM), F32), cos], axis=0)
    sin = jnp.concatenate([jnp.zeros((n_ctx, NA_HEAD_DIM), F32), sin], axis=0)
    reps = LANES // NA_HEAD_DIM
    return jnp.tile(cos, (1, reps)), jnp.tile(sin, (1, reps))


def _na_plan(n_ctx, grid_rows):
    span = NA_WIN_ROWS + NA_Q_ROWS - 1
    variants, var_of_step, ws_of_step = [], [], []
    for r0 in range(0, grid_rows, NA_Q_ROWS):
        rs = [min(max(r0 + dq - NA_WIN_ROWS // 2, 0), grid_rows - NA_WIN_ROWS) for dq in range(NA_Q_ROWS)]
        ws = min(rs[0], grid_rows - span)
        key = (r0 - ws,) + tuple(r - ws for r in rs)
        if key not in variants:
            variants.append(key)
        var_of_step.append(variants.index(key))
        ws_of_step.append(ws)
    ctx_steps = n_ctx // (NA_Q_ROWS * GRID_W)
    var_of_step = [len(variants)] * ctx_steps + var_of_step
    ws_of_step = [0] * ctx_steps + ws_of_step
    return variants, np.asarray(var_of_step, np.int32), np.asarray(ws_of_step, np.int32)


def _na_bias_table(rel_bias, variants):
    h = rel_bias.shape[0]
    span = NA_WIN_ROWS + NA_Q_ROWS - 1
    n_dr, n_dc = 2 * NA_WIN_ROWS - 1, 2 * NA_WIN_COLS - 1
    col = np.arange(GRID_W)[:, None]
    kc = np.arange(GRID_W)[None, :]
    cs = np.clip(col - NA_WIN_COLS // 2, 0, GRID_W - NA_WIN_COLS)
    col_ok = (kc >= cs) & (kc < cs + NA_WIN_COLS)
    oh_c = ((kc - col + NA_WIN_COLS - 1)[..., None] == np.arange(n_dc)) & col_ok[..., None]
    oh_r = np.zeros((len(variants), NA_Q_ROWS, span, n_dr), np.float32)
    row_ok = np.zeros((len(variants), NA_Q_ROWS, span), bool)
    for vi, key in enumerate(variants):
        for dq in range(NA_Q_ROWS):
            for j in range(key[1 + dq], key[1 + dq] + NA_WIN_ROWS):
                oh_r[vi, dq, j, j - (key[0] + dq) + NA_WIN_ROWS - 1] = 1.0
                row_ok[vi, dq, j] = True
    hp = lax.Precision.HIGHEST
    t1 = jnp.einsum("hrc,xkc->hrxk", rel_bias.astype(F32), jnp.asarray(oh_c.astype(np.float32)), precision=hp)
    tbl = jnp.einsum("vqjr,hrxk->vhqxjk", jnp.asarray(oh_r), t1, precision=hp)
    ok = row_ok[:, None, :, None, :, None] & col_ok[None, None, None, :, None, :]
    tbl = jnp.where(jnp.asarray(ok), tbl * LOG2E, MASKED)
    tbl = tbl.reshape(len(variants), h, NA_Q_ROWS * GRID_W, span * GRID_W)
    none = jnp.full((1,) + tbl.shape[1:], MASKED, F32)
    return jnp.concatenate([tbl, none], axis=0)


def _na_kernel(var_ref, ws_ref, q_ref, k_ref, v_ref, t_ref, o_ref, sw_ref, sc_ref, *, step_off, n_ctx):
    del var_ref
    nq = q_ref.shape[1]
    win = sw_ref.shape[1]
    j = pl.program_id(1) + step_off
    start = pl.multiple_of(n_ctx + ws_ref[j] * GRID_W, GRID_W)
    low = lax.broadcasted_iota(jnp.int32, (nq, LANES), 1) < NA_HEAD_DIM
    for hp in range(NA_HEADS // 2):
        sl = slice(hp * LANES, (hp + 1) * LANES)
        q2 = q_ref[0, :, sl]
        kw = k_ref[0, pl.ds(start, win), sl]
        kc = k_ref[0, 0:n_ctx, sl]
        for par in range(2):
            h = 2 * hp + par
            qh = jnp.where(low if par == 0 else jnp.logical_not(low), q2, jnp.zeros_like(q2))
            rows = slice(h * nq, (h + 1) * nq)
            sw_ref[rows, :] = lax.dot_general(qh, kw, _NT, preferred_element_type=F32) + t_ref[0, h]
            sc_ref[rows, :] = lax.dot_general(qh, kc, _NT, preferred_element_type=F32)
    sw = sw_ref[...]
    sc = sc_ref[...]
    m = jnp.maximum(jnp.max(sw, axis=-1, keepdims=True), jnp.max(sc, axis=-1, keepdims=True))
    pw = jnp.exp2(sw - m)
    pc = jnp.exp2(sc - m)
    inv = 1.0 / (jnp.sum(pw, axis=-1, keepdims=True) + jnp.sum(pc, axis=-1, keepdims=True))
    pw = pw.astype(BF16)
    pc = pc.astype(BF16)
    for hp in range(NA_HEADS // 2):
        sl = slice(hp * LANES, (hp + 1) * LANES)
        vw = v_ref[0, pl.ds(start, win), sl]
        vc = v_ref[0, 0:n_ctx, sl]
        pair = []
        for par in range(2):
            rows = slice((2 * hp + par) * nq, (2 * hp + par + 1) * nq)
            o = (jnp.dot(pw[rows], vw, preferred_element_type=F32)
                 + jnp.dot(pc[rows], vc, preferred_element_type=F32))
            pair.append(o * inv[rows])
        o_ref[0, :, sl] = jnp.where(low, pair[0], pair[1])


def _na_attention(q, k, v, rel_bias, n_ctx, with_ctx):
    b, lt, _ = q.shape
    nq = NA_Q_ROWS * GRID_W
    span = NA_WIN_ROWS + NA_Q_ROWS - 1
    variants, var_of_step, ws_of_step = _na_plan(n_ctx, (lt - n_ctx) // GRID_W)
    table = _na_bias_table(rel_bias, variants)
    step_off = 0 if with_ctx else n_ctx // nq
    n_steps = lt // nq - step_off

    kern = functools.partial(_na_kernel, step_off=step_off, n_ctx=n_ctx)
    grid_spec = pltpu.PrefetchScalarGridSpec(
        num_scalar_prefetch=2,
        grid=(b, n_steps),
        in_specs=[pl.BlockSpec((1, nq, NA_WIDTH), lambda bi, j, var, ws: (bi, j + step_off, 0)),
                  pl.BlockSpec((1, lt, NA_WIDTH), lambda bi, j, var, ws: (bi, 0, 0)),
                  pl.BlockSpec((1, lt, NA_WIDTH), lambda bi, j, var, ws: (bi, 0, 0)),
                  pl.BlockSpec((1, NA_HEADS, nq, span * GRID_W),
                               lambda bi, j, var, ws: (var[j + step_off], 0, 0, 0))],
        out_specs=pl.BlockSpec((1, nq, NA_WIDTH), lambda bi, j, var, ws: (bi, j, 0)),
        scratch_shapes=[pltpu.VMEM((NA_HEADS * nq, span * GRID_W), F32),
                        pltpu.VMEM((NA_HEADS * nq, n_ctx), F32)],
    )
    return pl.pallas_call(
        kern,
        grid_spec=grid_spec,
        out_shape=jax.ShapeDtypeStruct((b, n_steps * nq, NA_WIDTH), F32),
        compiler_params=_cparams(("arbitrary", "arbitrary")),
        name="na_attention",
    )(jnp.asarray(var_of_step), jnp.asarray(ws_of_step), q, k, v, table)


def _split3(x):
    hi = x.astype(BF16)
    r1 = x - hi.astype(F32)
    mid = r1.astype(BF16)
    lo = (r1 - mid.astype(F32)).astype(BF16)
    return hi, mid, lo


def _block_ref_rows(b, w, rev):
    c, n = b.shape
    off = w if rev else w - 1
    if 2 * w >= SUBLANES:
        parts = [jnp.broadcast_to(b[s + off:s + off + 1], (2 * w, n)) for s in range(0, c, 2 * w)]
        return parts[0] if len(parts) == 1 else jnp.concatenate(parts, axis=0)
    b3 = b.reshape(c // SUBLANES, SUBLANES, n)
    sub = lax.broadcasted_iota(jnp.int32, b3.shape, 1)
    r = None
    for s in range(0, SUBLANES, 2 * w):
        piece = jnp.broadcast_to(b3[:, s + off:s + off + 1, :], b3.shape)
        r = piece if r is None else jnp.where(sub >= s, piece, r)
    return r.reshape(c, n)


def _hgrn_chunk(q, f, vb, lb, st, tri, negq_ref, negk_ref, lmask_ref, rev):
    c = q.shape[0]
    one_m_lb = 1.0 - lb
    g = lb + one_m_lb * jax.nn.sigmoid(f)
    lg = jnp.log(jnp.maximum(g, GATE_FLOOR)) * LOG2E
    kk = one_m_lb * jax.nn.sigmoid(-f)
    hi, mid, lo = _split3(lg)
    b = (jnp.dot(tri, hi, preferred_element_type=F32) + jnp.dot(tri, mid, preferred_element_type=F32)
         + jnp.dot(tri, lo, preferred_element_type=F32))
    b_last = b[0:1] if rev else b[c - 1:c]

    qe = (q * jnp.exp2(b)).astype(BF16)
    o = lax.dot_general(qe, st.astype(BF16), _NT, preferred_element_type=F32)
    ke = (kk * jnp.exp2(b_last - b)).astype(BF16)
    st_new = st * jnp.exp2(b_last) + lax.dot_general(vb, ke, _TN, preferred_element_type=F32)

    a = None
    w = c // 2
    for li in range(lmask_ref.shape[0]):
        d = b - _block_ref_rows(b, w, rev)
        qw = (q * jnp.exp2(d + negq_ref[li])).astype(BF16)
        kw = (kk * jnp.exp2(negk_ref[li] - d)).astype(BF16)
        p = lax.dot_general(qw, kw, _NT, preferred_element_type=F32) * lmask_ref[li]
        a = p if a is None else a + p
        w //= 2
    o = o + jnp.dot(a.astype(BF16), vb, preferred_element_type=F32)
    o = o + jnp.sum(q * kk, axis=-1, keepdims=True) * vb.astype(F32)
    return o, st_new


def _hgrn_kernel(hq_ref, hf_ref, hb_ref, hi_ref, lbf_ref, lbb_ref, trif_ref, trir_ref,
                 negqf_ref, negkf_ref, negqr_ref, negkr_ref, lmask_ref,
                 o_ref, sf_ref, sb_ref, ob_ref, *, n_ctx_chunks, n_chunks):
    sf_ref[...] = jnp.zeros_like(sf_ref)
    sb_ref[...] = jnp.zeros_like(sb_ref)
    lbf = lbf_ref[0]
    lbb = lbb_ref[0]

    def steps(cf0, cb0):
        sf = sf_ref[...]
        sb = sb_ref[...]
        for u in range(HG_UNROLL):
            rf = pl.ds(pl.multiple_of((cf0 + u) * HG_CHUNK, HG_CHUNK), HG_CHUNK)
            rb = pl.ds(pl.multiple_of((cb0 - u) * HG_CHUNK, HG_CHUNK), HG_CHUNK)
            of, sf = _hgrn_chunk(hq_ref[0, rf, :], hf_ref[0, rf, :], hi_ref[0, rf, :], lbf, sf,
                                 trif_ref[...], negqf_ref, negkf_ref, lmask_ref, False)
            ob, sb = _hgrn_chunk(hq_ref[0, rb, :], hb_ref[0, rb, :], hi_ref[0, rb, :], lbb, sb,
                                 trir_ref[...], negqr_ref, negkr_ref, lmask_ref, True)
            o_ref[0, rf, :] = of
            ob_ref[rb, :] = ob
        sf_ref[...] = sf
        sb_ref[...] = sb

    def ctx_body(i, carry):
        steps(i * HG_UNROLL, n_ctx_chunks - 1 - i * HG_UNROLL)
        return carry

    def lat_body(i, carry):
        steps(n_ctx_chunks + i * HG_UNROLL, n_chunks - 1 - i * HG_UNROLL)
        return carry

    lax.fori_loop(0, n_ctx_chunks // HG_UNROLL, ctx_body, 0)
    lax.fori_loop(0, (n_chunks - n_ctx_chunks) // HG_UNROLL, lat_body, 0)
    o_ref[0] = o_ref[0] + ob_ref[...]


def _hgrn_level_constants():
    c = HG_CHUNK
    t = np.arange(c)
    xor = t[:, None] ^ t[None, :]
    lmask, negq_f, negq_r = [], [], []
    w = c // 2
    while w >= 1:
        lmask.append(((xor >= w) & (xor < 2 * w)).astype(np.float32))
        upper = (t % (2 * w)) >= w
        negq_f.append(np.where(upper, 0.0, MASKED))
        negq_r.append(np.where(upper, MASKED, 0.0))
        w //= 2

    def rows(m):
        return jnp.asarray(np.broadcast_to(np.stack(m)[:, :, None], (len(m), c, HG_DIM)).astype(np.float32))

    tri_f = (t[None, :] <= t[:, None]).astype(np.float32)
    return (jnp.asarray(tri_f, BF16), jnp.asarray(tri_f.T, BF16),
            rows(negq_f), rows(negq_r), rows(negq_r), rows(negq_f), jnp.asarray(np.stack(lmask)))


def _hgrn(hq, hf, hb, hi, lb_f, lb_b, n_ctx):
    b, lt, _ = hq.shape
    kern = functools.partial(_hgrn_kernel, n_ctx_chunks=n_ctx // HG_CHUNK, n_chunks=lt // HG_CHUNK)
    seq = pl.BlockSpec((1, lt, HG_DIM), lambda bi, h: (bi, 0, h))
    lbs = pl.BlockSpec((1, 1, HG_DIM), lambda bi, h: (h, 0, 0))
    consts = _hgrn_level_constants()
    return pl.pallas_call(
        kern,
        grid=(b, HG_HEADS),
        in_specs=[seq, seq, seq, seq, lbs, lbs]
        + [pl.BlockSpec(a.shape, lambda bi, h, nd=a.ndim: (0,) * nd) for a in consts],
        out_specs=seq,
        out_shape=jax.ShapeDtypeStruct((b, lt, HG_WIDTH), F32),
        scratch_shapes=[pltpu.VMEM((HG_DIM, HG_DIM), F32), pltpu.VMEM((HG_DIM, HG_DIM), F32),
                        pltpu.VMEM((lt, HG_DIM), F32)],
        compiler_params=_cparams(("arbitrary", "arbitrary")),
        name="hgrn2",
    )(hq, hf, hb, hi, lb_f.reshape(HG_HEADS, 1, HG_DIM), lb_b.reshape(HG_HEADS, 1, HG_DIM), *consts)


def _split2(x):
    hi = x.astype(BF16)
    lo = (x - hi.astype(F32)).astype(BF16)
    return hi, lo


def _route(logits, bias):
    biased = jax.nn.sigmoid(logits) + bias
    rows = [biased[e:e + 1] for e in range(N_EXPERTS)]
    n = EXPERTS_PER_GROUP
    best = None
    g_sel = None
    for g in range(N_GROUPS):
        gs = None
        for i in range(n):
            for j2 in range(i + 1, n):
                pair = rows[g * n + i] + rows[g * n + j2]
                gs = pair if gs is None else jnp.maximum(gs, pair)
        if best is None:
            best, g_sel = gs, jnp.zeros(gs.shape, jnp.int32)
        else:
            take = gs > best
            best = jnp.where(take, gs, best)
            g_sel = jnp.where(take, g, g_sel)
    cand = []
    for i in range(n):
        c_i = rows[(N_GROUPS - 1) * n + i]
        for g in range(N_GROUPS - 2, -1, -1):
            c_i = jnp.where(g_sel == g, rows[g * n + i], c_i)
        cand.append(c_i)
    m1, i1 = cand[0], jnp.zeros(best.shape, jnp.int32)
    for i in range(1, n):
        take = cand[i] > m1
        m1 = jnp.where(take, cand[i], m1)
        i1 = jnp.where(take, i, i1)
    m2 = jnp.full(best.shape, -jnp.inf, F32)
    i2 = jnp.zeros(best.shape, jnp.int32)
    for i in range(n):
        take = (i1 != i) & (cand[i] > m2)
        m2 = jnp.where(take, cand[i], m2)
        i2 = jnp.where(take, i, i2)
    lo = jnp.minimum(i1, i2)
    hi = jnp.maximum(i1, i2)
    pair = jnp.where(lo == 0, hi - 1, jnp.where(lo == 1, hi + 1, N_PAIRS - 1))
    return g_sel * N_PAIRS + pair


def _merge_kernel(ona_ref, ohg_ref, og_ref, ga_ref, gh_ref, x_ref, mod_ref, hgg_ref, n2g_ref,
                  wna_ref, whg_ref, wout_ref, wr_ref, rb_ref,
                  xn_ref, h2_ref, bucket_ref, rank_ref, cnt_ref, carry_ref):
    first = (pl.program_id(0) == 0) & (pl.program_id(1) == 0)

    @pl.when(first)
    def _():
        carry_ref[...] = jnp.zeros_like(carry_ref)

    ohg = ohg_ref[0]
    og = og_ref[0]
    gain = hgg_ref[...]
    heads = []
    for h in range(HG_HEADS):
        sl = slice(h * HG_DIM, (h + 1) * HG_DIM)
        oh = ohg[:, sl]
        yh = oh * lax.rsqrt(jnp.mean(oh * oh, axis=-1, keepdims=True) + EPS) * gain
        gt = og[:, sl]
        heads.append((yh * (gt * jax.nn.sigmoid(gt))).astype(BF16))
    hn = jnp.concatenate(heads, axis=-1)
    y_na = jnp.dot(ona_ref[0].astype(BF16), wna_ref[...], preferred_element_type=F32)
    y_hg = jnp.dot(hn, whg_ref[...], preferred_element_type=F32)
    m = jax.nn.sigmoid(ga_ref[0]) * y_na + jax.nn.sigmoid(gh_ref[0]) * y_hg
    y = jnp.dot(m.astype(BF16), wout_ref[...], preferred_element_type=F32)
    xn = x_ref[0] + mod_ref[0, 2:3, :] * y
    xn_ref[0] = xn
    yn = xn * lax.rsqrt(jnp.mean(xn * xn, axis=-1, keepdims=True) + EPS) * n2g_ref[...]
    h2 = yn * (1.0 + mod_ref[0, 4:5, :]) + mod_ref[0, 3:4, :]
    h2_ref[...] = h2

    w_hi, w_lo = _split2(wr_ref[...])
    h_hi, h_lo = _split2(h2)
    logits = (lax.dot_general(w_hi, h_hi, _NT, preferred_element_type=F32)
              + lax.dot_general(w_hi, h_lo, _NT, preferred_element_type=F32)
              + lax.dot_general(w_lo, h_hi, _NT, preferred_element_type=F32))
    bucket = _route(logits, rb_ref[...])
    bucket_ref[0] = bucket

    t = bucket.shape[1]
    onehot = (lax.broadcasted_iota(jnp.int32, (BUCKET_ROWS, t), 0) == bucket).astype(F32)
    before = (lax.broadcasted_iota(jnp.int32, (t, t), 0)
              < lax.broadcasted_iota(jnp.int32, (t, t), 1)).astype(BF16)
    prefix = jnp.dot(onehot.astype(BF16), before, preferred_element_type=F32)
    carry = carry_ref[...]
    rank = jnp.sum(onehot * (prefix + carry[:, 0:1]), axis=0, keepdims=True)
    rank_ref[0] = rank.astype(jnp.int32)
    carry = carry + jnp.sum(onehot, axis=1, keepdims=True)
    carry_ref[...] = carry
    cnt_ref[...] = carry.astype(jnp.int32)


def _merge(o_na, o_hg, og, ga, gh, xc, mods, hg_gain, n2_gain, w_na_o, w_hg_o, w_out, w_router_t,
           router_bias, ctx_row, skip):
    b, lt, d = xc.shape
    n_t = lt // ROW_TILE - skip

    def rows(width):
        return pl.BlockSpec((1, ROW_TILE, width), lambda bi, j: (bi, j + skip, 0))

    def full(a):
        return pl.BlockSpec(a.shape, lambda bi, j: (0,) * a.ndim)

    def mod_map(bi, j):
        return (jnp.where(j + skip == 0, ctx_row, bi), 0, 0)

    tok = pl.BlockSpec((1, 1, ROW_TILE), lambda bi, j: (bi * n_t + j, 0, 0))
    consts = [hg_gain, n2_gain, w_na_o, w_hg_o, w_out, w_router_t, router_bias]
    return pl.pallas_call(
        _merge_kernel,
        grid=(b, n_t),
        in_specs=[pl.BlockSpec((1, ROW_TILE, NA_WIDTH), lambda bi, j: (bi, j, 0)),
                  rows(HG_WIDTH), rows(HG_WIDTH), rows(d), rows(d), rows(d),
                  pl.BlockSpec((1, 6, d), mod_map)] + [full(a) for a in consts],
        out_specs=[pl.BlockSpec((1, ROW_TILE, d), lambda bi, j: (bi, j, 0)),
                   pl.BlockSpec((ROW_TILE, d), lambda bi, j: (bi * n_t + j, 0)),
                   tok, tok,
                   pl.BlockSpec((BUCKET_ROWS, LANES), lambda bi, j: (0, 0))],
        out_shape=[jax.ShapeDtypeStruct((b, n_t * ROW_TILE, d), F32),
                   jax.ShapeDtypeStruct((b * n_t * ROW_TILE, d), F32),
                   jax.ShapeDtypeStruct((b * n_t, 1, ROW_TILE), jnp.int32),
                   jax.ShapeDtypeStruct((b * n_t, 1, ROW_TILE), jnp.int32),
                   jax.ShapeDtypeStruct((BUCKET_ROWS, LANES), jnp.int32)],
        scratch_shapes=[pltpu.VMEM((BUCKET_ROWS, LANES), F32)],
        compiler_params=_cparams(("arbitrary", "arbitrary")),
        name="merge_router",
    )(o_na, o_hg, og, ga, gh, xc, mods, *consts)


def _start_row_copies(n, row_copy):
    def body(g, carry):
        for u in range(ISSUE_UNROLL):
            row_copy(g * ISSUE_UNROLL + u).start()
        return carry

    lax.fori_loop(0, n // ISSUE_UNROLL, body, 0)


def _scatter_kernel(dest_ref, h_ref, xs_in_ref, xs_ref, sem):
    del xs_in_ref
    n = h_ref.shape[0]
    _start_row_copies(n, lambda r: pltpu.make_async_copy(
        h_ref.at[pl.ds(r, 1)], xs_ref.at[pl.ds(dest_ref[0, 0, r], 1)], sem))
    pltpu.make_async_copy(h_ref, xs_ref.at[pl.ds(0, n)], sem).wait()


def _scatter_rows(h2, dest, n_sorted):
    t, d = h2.shape
    tile = 2 * ROW_TILE if t % (2 * ROW_TILE) == 0 else ROW_TILE
    n_t = t // tile
    return pl.pallas_call(
        _scatter_kernel,
        grid=(n_t,),
        in_specs=[pl.BlockSpec((1, 1, tile), lambda i: (i, 0, 0), memory_space=pltpu.SMEM),
                  pl.BlockSpec((tile, d), lambda i: (i, 0)),
                  pl.BlockSpec(memory_space=pl.ANY)],
        out_specs=pl.BlockSpec(memory_space=pl.ANY),
        out_shape=jax.ShapeDtypeStruct((n_sorted, d), F32),
        scratch_shapes=[pltpu.SemaphoreType.DMA(())],
        input_output_aliases={2: 0},
        compiler_params=_cparams(("arbitrary",)),
        name="scatter_rows",
    )(dest.reshape(n_t, 1, tile), h2, jnp.zeros((n_sorted, d), F32))


def _final_kernel(dest_ref, dest_next_ref, xn_ref, mod_ref, g_ref, ys_ref, o_ref, buf_ref, sems,
                  *, final_norm):
    n = buf_ref.shape[1]
    step = pl.program_id(0) * pl.num_programs(1) + pl.program_id(1)
    n_steps = pl.num_programs(0) * pl.num_programs(1)

    def gather(d_ref, slot):
        _start_row_copies(n, lambda r: pltpu.make_async_copy(
            ys_ref.at[pl.ds(d_ref[0, 0, r], 1)], buf_ref.at[slot, pl.ds(r, 1)], sems.at[slot]))

    @pl.when(step == 0)
    def _():
        gather(dest_ref, 0)

    @pl.when(step + 1 < n_steps)
    def _():
        gather(dest_next_ref, (step + 1) % 2)

    slot = step % 2
    pltpu.make_async_copy(ys_ref.at[pl.ds(0, n)], buf_ref.at[slot], sems.at[slot]).wait()
    x = xn_ref[0] + mod_ref[0, 5:6, :] * buf_ref[slot]
    if final_norm:
        x = x * lax.rsqrt(jnp.mean(x * x, axis=-1, keepdims=True) + EPS) * g_ref[...]
    o_ref[0] = x


def _gather_residual(xn, mods, ys, dest, gain, ctx_row, final_norm, skip):
    b, lt, d = xn.shape
    n_t = lt // ROW_TILE

    def mod_map(bi, j):
        return (jnp.where(j + skip == 0, ctx_row, bi), 0, 0)

    kern = functools.partial(_final_kernel, final_norm=final_norm)
    last_tile = b * n_t - 1
    dest3 = dest.reshape(b * n_t, 1, ROW_TILE)
    return pl.pallas_call(
        kern,
        grid=(b, n_t),
        in_specs=[pl.BlockSpec((1, 1, ROW_TILE), lambda bi, j: (bi * n_t + j, 0, 0),
                               memory_space=pltpu.SMEM),
                  pl.BlockSpec((1, 1, ROW_TILE), lambda bi, j: (jnp.minimum(bi * n_t + j + 1, last_tile), 0, 0),
                               memory_space=pltpu.SMEM),
                  pl.BlockSpec((1, ROW_TILE, d), lambda bi, j: (bi, j, 0)),
                  pl.BlockSpec((1, 6, d), mod_map),
                  pl.BlockSpec((1, d), lambda bi, j: (0, 0)),
                  pl.BlockSpec(memory_space=pl.ANY)],
        out_specs=pl.BlockSpec((1, ROW_TILE, d), lambda bi, j: (bi, j, 0)),
        out_shape=jax.ShapeDtypeStruct((b, lt, d), F32),
        scratch_shapes=[pltpu.VMEM((2, ROW_TILE, d), F32), pltpu.SemaphoreType.DMA((2,))],
        compiler_params=_cparams(("arbitrary", "arbitrary")),
        name="gather_residual",
    )(dest3, dest3, xn, mods, gain, ys)


def _moe_kernel(ea_ref, eb_ref, nused_ref, xs_ref, wra_ref, wrb_ref,
                wga_ref, wua_ref, wda_ref, wgb_ref, wub_ref, wdb_ref, ys_ref):
    del ea_ref, eb_ref
    i = pl.program_id(0)

    @pl.when(i < nused_ref[0])
    def _():
        x = xs_ref[...]
        s_a = jax.nn.sigmoid(jnp.sum(x * wra_ref[0], axis=-1, keepdims=True))
        s_b = jax.nn.sigmoid(jnp.sum(x * wrb_ref[0], axis=-1, keepdims=True))
        tot = s_a + s_b
        xb = x.astype(BF16)

        def expert(wg, wu, wd):
            gate = jnp.dot(xb, wg[0], preferred_element_type=F32)
            up = jnp.dot(xb, wu[0], preferred_element_type=F32)
            hid = (gate * jax.nn.sigmoid(gate) * up).astype(BF16)
            return jnp.dot(hid, wd[0], preferred_element_type=F32)

        ys_ref[...] = ((s_a / tot) * expert(wga_ref, wua_ref, wda_ref)
                       + (s_b / tot) * expert(wgb_ref, wub_ref, wdb_ref))

    @pl.when(i >= nused_ref[0])
    def _():
        ys_ref[...] = jnp.zeros_like(ys_ref)


def _moe(xs, tile_ea, tile_eb, n_used, w_router_rows, w_gate, w_up, w_down):
    n_sorted, d = xs.shape
    n_tiles = n_sorted // ROW_TILE
    ff = w_gate.shape[-1]

    def by_a(i, ea, eb, nu):
        return (ea[i], 0, 0)

    def by_b(i, ea, eb, nu):
        return (eb[i], 0, 0)

    grid_spec = pltpu.PrefetchScalarGridSpec(
        num_scalar_prefetch=3,
        grid=(n_tiles,),
        in_specs=[pl.BlockSpec((ROW_TILE, d), lambda i, ea, eb, nu: (i, 0)),
                  pl.BlockSpec((1, 1, d), by_a), pl.BlockSpec((1, 1, d), by_b),
                  pl.BlockSpec((1, d, ff), by_a), pl.BlockSpec((1, d, ff), by_a),
                  pl.BlockSpec((1, ff, d), by_a),
                  pl.BlockSpec((1, d, ff), by_b), pl.BlockSpec((1, d, ff), by_b),
                  pl.BlockSpec((1, ff, d), by_b)],
        out_specs=pl.BlockSpec((ROW_TILE, d), lambda i, ea, eb, nu: (i, 0)),
    )
    return pl.pallas_call(
        _moe_kernel,
        grid_spec=grid_spec,
        out_shape=jax.ShapeDtypeStruct((n_sorted, d), F32),
        compiler_params=_cparams(("arbitrary",)),
        name="moe_pairs",
    )(tile_ea, tile_eb, n_used, xs, w_router_rows, w_router_rows,
      w_gate, w_up, w_down, w_gate, w_up, w_down)


_PAIR_LO = np.array([0, 0, 0, 1, 1, 2], np.int32)
_PAIR_HI = np.array([1, 2, 3, 2, 3, 3], np.int32)


def _sorted_layout(bucket, rank, counts, n_tiles):
    counts = counts[:N_BUCKETS]
    padded = ((counts + ROW_TILE - 1) // ROW_TILE) * ROW_TILE
    ends = jnp.cumsum(padded)
    starts = ends - padded
    dest = starts[bucket] + rank
    tile_start = jnp.arange(n_tiles, dtype=jnp.int32) * ROW_TILE
    tile_bucket = jnp.sum((ends[None, :] <= tile_start[:, None]).astype(jnp.int32), axis=1)
    tile_bucket = jnp.minimum(tile_bucket, N_BUCKETS - 1)
    group = tile_bucket // N_PAIRS
    pair = tile_bucket % N_PAIRS
    ea = group * EXPERTS_PER_GROUP + jnp.asarray(_PAIR_LO)[pair]
    eb = group * EXPERTS_PER_GROUP + jnp.asarray(_PAIR_HI)[pair]
    n_used = (ends[-1] // ROW_TILE).astype(jnp.int32).reshape(1)
    return dest.astype(jnp.int32), ea.astype(jnp.int32), eb.astype(jnp.int32), n_used


def _lower_bounds(raw):
    p = jax.nn.softmax(raw.astype(F32), axis=0)
    return jnp.cumsum(p, axis=0) - p[0:1]


def kernel(x, c, ctx, c_ctx, w_ada, b_ada, norm1_g, w_in, na_rel_bias, hg_lower_fwd, hg_lower_bwd,
           hg_norm_g, w_na_o, w_hg_o, w_out, norm2_g, w_router, router_bias, w_gate, w_up, w_down,
           final_g):
    b, seq, d = x.shape
    n_ctx = ctx.shape[1]
    depth = w_ada.shape[0]
    lt = n_ctx + seq
    assert n_ctx % ROW_TILE == 0 and seq % ROW_TILE == 0 and seq % GRID_W == 0
    assert n_ctx % HG_CHUNK == 0 and seq % HG_CHUNK == 0
    assert seq // GRID_W >= NA_WIN_ROWS + NA_Q_ROWS - 1 and (seq // GRID_W) % NA_Q_ROWS == 0
    assert n_ctx % (NA_Q_ROWS * GRID_W) == 0

    ada_rows = -(-(b + 1) // 8) * 8
    ctx_row = b
    cc = jnp.concatenate([c, c_ctx[None, :], jnp.zeros((ada_rows - b - 1, d), F32)], axis=0)
    mods = _ada(cc, w_ada, b_ada).reshape(depth, ada_rows, 6, d)

    lb_f = _lower_bounds(hg_lower_fwd)
    lb_b = _lower_bounds(hg_lower_bwd)
    cos_t, sin_t = _rope_tables(n_ctx, seq)
    w_router_t = jnp.transpose(w_router)
    w_router_rows = w_router_t.reshape(N_EXPERTS, 1, d)
    rb = router_bias.astype(F32).reshape(N_EXPERTS, 1)

    xc = jnp.concatenate([ctx, x], axis=1)
    out = None
    for l in range(depth):
        last = l == depth - 1
        skip = n_ctx // ROW_TILE if last else 0
        n_tiles = b * (lt // ROW_TILE - skip) + N_BUCKETS
        q, k, v, hq, hf, hb, hi, og, ga, gh = _inproj(
            xc, mods[l], norm1_g[l].reshape(1, d), w_in[l].astype(BF16), cos_t, sin_t, ctx_row)
        o_na = _na_attention(q, k, v, na_rel_bias[l], n_ctx, with_ctx=not last)
        o_hg = _hgrn(hq, hf, hb, hi, lb_f[l], lb_b[l], n_ctx)
        xn, h2, bucket, rank, counts = _merge(
            o_na, o_hg, og, ga, gh, xc, mods[l], hg_norm_g[l].reshape(1, HG_DIM),
            norm2_g[l].reshape(1, d), w_na_o[l].astype(BF16), w_hg_o[l].astype(BF16),
            w_out[l].astype(BF16), w_router_t, rb, ctx_row, skip)
        dest, tile_ea, tile_eb, n_used = _sorted_layout(
            bucket.reshape(-1), rank.reshape(-1), counts[:, 0], n_tiles)
        xs = _scatter_rows(h2, dest, n_tiles * ROW_TILE)
        ys = _moe(xs, tile_ea, tile_eb, n_used, w_router_rows,
                  w_gate[l].astype(BF16), w_up[l].astype(BF16), w_down[l].astype(BF16))
        res = _gather_residual(xn, mods[l], ys, dest, final_g.reshape(1, d), ctx_row, last, skip)
        if last:
            out = res
        else:
            xc = res
    return out
```

```python
import functools
import math

import jax
import jax.numpy as jnp
import numpy as np
from jax import lax
from jax.experimental import pallas as pl
from jax.experimental.pallas import tpu as pltpu

F32 = jnp.float32
BF16 = jnp.bfloat16

GRID_W = 64
EPS = 1e-6
NA_HEADS = 8
NA_HEAD_DIM = 64
NA_WIDTH = NA_HEADS * NA_HEAD_DIM
NA_WIN_ROWS = 8
NA_WIN_COLS = 16
ROPE_THETA = 10000.0
HG_HEADS = 4
HG_DIM = 128
HG_WIDTH = HG_HEADS * HG_DIM
GATE_FLOOR = 1e-30
N_EXPERTS = 16
N_GROUPS = 4
EXPERTS_PER_GROUP = 4
N_PAIRS = 6
N_BUCKETS = N_GROUPS * N_PAIRS
BUCKET_ROWS = 32
MASKED = -1e30
LOG2E = 1.4426950408889634

LANES = 128
ROW_TILE = 256
NA_Q_ROWS = 2
SUBLANES = 8
HG_CHUNK = 128
HG_UNROLL = 4
ISSUE_UNROLL = 8
VMEM_LIMIT = 56 * 1024 * 1024

_NT = (((1,), (1,)), ((), ()))
_TN = (((0,), (0,)), ((), ()))


def _cparams(sem):
    return pltpu.CompilerParams(dimension_semantics=sem, vmem_limit_bytes=VMEM_LIMIT)


def _ada_kernel(c_ref, w_ref, b_ref, o_ref):
    cc = c_ref[...]
    s = cc * jax.nn.sigmoid(cc)
    o_ref[0] = jnp.dot(s, w_ref[0], preferred_element_type=F32) + b_ref[0]


def _ada(cc, w_ada, b_ada):
    depth, d, n = w_ada.shape
    rows = cc.shape[0]
    tn = 1536
    return pl.pallas_call(
        _ada_kernel,
        grid=(depth, n // tn),
        in_specs=[pl.BlockSpec((rows, d), lambda l, j: (0, 0)),
                  pl.BlockSpec((1, d, tn), lambda l, j: (l, 0, j)),
                  pl.BlockSpec((1, 1, tn), lambda l, j: (l, 0, j))],
        out_specs=pl.BlockSpec((1, rows, tn), lambda l, j: (l, 0, j)),
        out_shape=jax.ShapeDtypeStruct((depth, rows, n), F32),
        compiler_params=_cparams(("arbitrary", "arbitrary")),
        name="ada",
    )(cc, w_ada, b_ada.reshape(depth, 1, n))


def _inproj_kernel(x_ref, mod_ref, g_ref, w_ref, cos_ref, sin_ref,
                   q_ref, k_ref, v_ref, hq_ref, hf_ref, hb_ref, hi_ref, og_ref, ga_ref, gh_ref):
    x = x_ref[0]
    y = x * lax.rsqrt(jnp.mean(x * x, axis=-1, keepdims=True) + EPS) * g_ref[...]
    h = (y * (1.0 + mod_ref[0, 1:2, :]) + mod_ref[0, 0:1, :]).astype(BF16)

    def proj(off, width):
        return jnp.dot(h, w_ref[:, off:off + width], preferred_element_type=F32)

    cos = cos_ref[...]
    sin = sin_ref[...]
    even = (lax.broadcasted_iota(jnp.int32, cos.shape, 1) % 2) == 0

    def rope(z):
        swapped = jnp.where(even, pltpu.roll(z, LANES - 1, 1), pltpu.roll(z, 1, 1))
        return z * cos + swapped * sin

    scale = NA_HEAD_DIM ** -0.5 * LOG2E
    for blk in range(NA_WIDTH // LANES):
        sl = slice(blk * LANES, (blk + 1) * LANES)
        q_ref[0, :, sl] = (rope(proj(blk * LANES, LANES)) * scale).astype(q_ref.dtype)
        k_ref[0, :, sl] = rope(proj(NA_WIDTH + blk * LANES, LANES)).astype(k_ref.dtype)
    off = 2 * NA_WIDTH
    v_ref[0] = proj(off, NA_WIDTH).astype(v_ref.dtype)
    off += NA_WIDTH
    hq = proj(off, HG_WIDTH)
    hq_ref[0] = hq * jax.nn.sigmoid(hq)
    off += HG_WIDTH
    hf_ref[0] = proj(off, HG_WIDTH)
    off += HG_WIDTH
    hb_ref[0] = proj(off, HG_WIDTH)
    off += HG_WIDTH
    hi_ref[0] = proj(off, HG_WIDTH).astype(hi_ref.dtype)
    off += HG_WIDTH
    og_ref[0] = proj(off, HG_WIDTH)
    off += HG_WIDTH
    d = x.shape[-1]
    for half in range(2):
        sl = slice(half * (d // 2), (half + 1) * (d // 2))
        ga_ref[0, :, sl] = proj(off + half * (d // 2), d // 2)
        gh_ref[0, :, sl] = proj(off + d + half * (d // 2), d // 2)


def _inproj(xc, mods, g, w_bf16, cos_t, sin_t, ctx_row):
    b, lt, d = xc.shape
    n_t = lt // ROW_TILE

    def rows(width):
        return pl.BlockSpec((1, ROW_TILE, width), lambda bi, j: (bi, j, 0))

    def mod_map(bi, j):
        return (jnp.where(j == 0, ctx_row, bi), 0, 0)

    widths = [NA_WIDTH] * 3 + [HG_WIDTH] * 5 + [d, d]
    dtypes = [BF16, BF16, BF16, F32, F32, F32, BF16, F32, F32, F32]
    return pl.pallas_call(
        _inproj_kernel,
        grid=(b, n_t),
        in_specs=[rows(d),
                  pl.BlockSpec((1, 6, d), mod_map),
                  pl.BlockSpec((1, d), lambda bi, j: (0, 0)),
                  pl.BlockSpec(w_bf16.shape, lambda bi, j: (0, 0)),
                  pl.BlockSpec((ROW_TILE, LANES), lambda bi, j: (j, 0)),
                  pl.BlockSpec((ROW_TILE, LANES), lambda bi, j: (j, 0))],
        out_specs=[rows(w) for w in widths],
        out_shape=[jax.ShapeDtypeStruct((b, lt, w), dt) for w, dt in zip(widths, dtypes)],
        compiler_params=_cparams(("arbitrary", "arbitrary")),
        name="inproj",
    )(xc, mods, g, w_bf16, cos_t, sin_t)


def _rope_tables(n_ctx, seq):
    t = jnp.arange(seq, dtype=jnp.int32)
    row = (t // GRID_W).astype(F32)
    col = (t % GRID_W).astype(F32)
    rot_half = NA_HEAD_DIM // 2
    inv = ROPE_THETA ** (-jnp.arange(0, rot_half, 2, dtype=F32) / rot_half)
    ang = jnp.concatenate([row[:, None] * inv, col[:, None] * inv], axis=-1)
    cos = jnp.repeat(jnp.cos(ang), 2, axis=-1)
    sin = jnp.repeat(jnp.sin(ang), 2, axis=-1)
    sign = jnp.asarray(np.tile(np.array([-1.0, 1.0], np.float32), NA_HEAD_DIM // 2))
    sin = sin * sign
    cos = jnp.concatenate([jnp.ones((n_ctx, NA_HEAD_DIM), F32), cos], axis=0)
    sin = jnp.concatenate([jnp.zeros((n_ctx, NA_HEAD_DIM), F32), sin], axis=0)
    reps = LANES // NA_HEAD_DIM
    return jnp.tile(cos, (1, reps)), jnp.tile(sin, (1, reps))


def _na_plan(n_ctx, grid_rows):
    span = NA_WIN_ROWS + NA_Q_ROWS - 1
    variants, var_of_step, ws_of_step = [], [], []
    for r0 in range(0, grid_rows, NA_Q_ROWS):
        rs = [min(max(r0 + dq - NA_WIN_ROWS // 2, 0), grid_rows - NA_WIN_ROWS) for dq in range(NA_Q_ROWS)]
        ws = min(rs[0], grid_rows - span)
        key = (r0 - ws,) + tuple(r - ws for r in rs)
        if key not in variants:
            variants.append(key)
        var_of_step.append(variants.index(key))
        ws_of_step.append(ws)
    ctx_steps = n_ctx // (NA_Q_ROWS * GRID_W)
    var_of_step = [len(variants)] * ctx_steps + var_of_step
    ws_of_step = [0] * ctx_steps + ws_of_step
    return variants, np.asarray(var_of_step, np.int32), np.asarray(ws_of_step, np.int32)


def _na_bias_table(rel_bias, variants):
    h = rel_bias.shape[0]
    span = NA_WIN_ROWS + NA_Q_ROWS - 1
    n_dr, n_dc = 2 * NA_WIN_ROWS - 1, 2 * NA_WIN_COLS - 1
    col = np.arange(GRID_W)[:, None]
    kc = np.arange(GRID_W)[None, :]
    cs = np.clip(col - NA_WIN_COLS // 2, 0, GRID_W - NA_WIN_COLS)
    col_ok = (kc >= cs) & (kc < cs + NA_WIN_COLS)
    oh_c = ((kc - col + NA_WIN_COLS - 1)[..., None] == np.arange(n_dc)) & col_ok[..., None]
    oh_r = np.zeros((len(variants), NA_Q_ROWS, span, n_dr), np.float32)
    row_ok = np.zeros((len(variants), NA_Q_ROWS, span), bool)
    for vi, key in enumerate(variants):
        for dq in range(NA_Q_ROWS):
            for j in range(key[1 + dq], key[1 + dq] + NA_WIN_ROWS):
                oh_r[vi, dq, j, j - (key[0] + dq) + NA_WIN_ROWS - 1] = 1.0
                row_ok[vi, dq, j] = True
    hp = lax.Precision.HIGHEST
    t1 = jnp.einsum("hrc,xkc->hrxk", rel_bias.astype(F32), jnp.asarray(oh_c.astype(np.float32)), precision=hp)
    tbl = jnp.einsum("vqjr,hrxk->vhqxjk", jnp.asarray(oh_r), t1, precision=hp)
    ok = row_ok[:, None, :, None, :, None] & col_ok[None, None, None, :, None, :]
    tbl = jnp.where(jnp.asarray(ok), tbl * LOG2E, MASKED)
    tbl = tbl.reshape(len(variants), h, NA_Q_ROWS * GRID_W, span * GRID_W)
    none = jnp.full((1,) + tbl.shape[1:], MASKED, F32)
    return jnp.concatenate([tbl, none], axis=0)


def _na_kernel(var_ref, ws_ref, q_ref, k_ref, v_ref, t_ref, o_ref, sw_ref, sc_ref, *, step_off, n_ctx):
    del var_ref
    nq = q_ref.shape[1]
    win = sw_ref.shape[1]
    j = pl.program_id(1) + step_off
    start = pl.multiple_of(n_ctx + ws_ref[j] * GRID_W, GRID_W)
    low = lax.broadcasted_iota(jnp.int32, (nq, LANES), 1) < NA_HEAD_DIM
    for hp in range(NA_HEADS // 2):
        sl = slice(hp * LANES, (hp + 1) * LANES)
        q2 = q_ref[0, :, sl]
        kw = k_ref[0, pl.ds(start, win), sl]
        kc = k_ref[0, 0:n_ctx, sl]
        for par in range(2):
            h = 2 * hp + par
            qh = jnp.where(low if par == 0 else jnp.logical_not(low), q2, jnp.zeros_like(q2))
            rows = slice(h * nq, (h + 1) * nq)
            sw_ref[rows, :] = lax.dot_general(qh, kw, _NT, preferred_element_type=F32) + t_ref[0, h]
            sc_ref[rows, :] = lax.dot_general(qh, kc, _NT, preferred_element_type=F32)
    sw = sw_ref[...]
    sc = sc_ref[...]
    m = jnp.maximum(jnp.max(sw, axis=-1, keepdims=True), jnp.max(sc, axis=-1, keepdims=True))
    pw = jnp.exp2(sw - m)
    pc = jnp.exp2(sc - m)
    inv = 1.0 / (jnp.sum(pw, axis=-1, keepdims=True) + jnp.sum(pc, axis=-1, keepdims=True))
    pw = pw.astype(BF16)
    pc = pc.astype(BF16)
    for hp in range(NA_HEADS // 2):
        sl = slice(hp * LANES, (hp + 1) * LANES)
        vw = v_ref[0, pl.ds(start, win), sl]
        vc = v_ref[0, 0:n_ctx, sl]
        pair = []
        for par in range(2):
            rows = slice((2 * hp + par) * nq, (2 * hp + par + 1) * nq)
            o = (jnp.dot(pw[rows], vw, preferred_element_type=F32)
                 + jnp.dot(pc[rows], vc, preferred_element_type=F32))
            pair.append(o * inv[rows])
        o_ref[0, :, sl] = jnp.where(low, pair[0], pair[1])


def _na_attention(q, k, v, rel_bias, n_ctx, with_ctx):
    b, lt, _ = q.shape
    nq = NA_Q_ROWS * GRID_W
    span = NA_WIN_ROWS + NA_Q_ROWS - 1
    variants, var_of_step, ws_of_step = _na_plan(n_ctx, (lt - n_ctx) // GRID_W)
    table = _na_bias_table(rel_bias, variants)
    step_off = 0 if with_ctx else n_ctx // nq
    n_steps = lt // nq - step_off

    kern = functools.partial(_na_kernel, step_off=step_off, n_ctx=n_ctx)
    grid_spec = pltpu.PrefetchScalarGridSpec(
        num_scalar_prefetch=2,
        grid=(b, n_steps),
        in_specs=[pl.BlockSpec((1, nq, NA_WIDTH), lambda bi, j, var, ws: (bi, j + step_off, 0)),
                  pl.BlockSpec((1, lt, NA_WIDTH), lambda bi, j, var, ws: (bi, 0, 0)),
                  pl.BlockSpec((1, lt, NA_WIDTH), lambda bi, j, var, ws: (bi, 0, 0)),
                  pl.BlockSpec((1, NA_HEADS, nq, span * GRID_W),
                               lambda bi, j, var, ws: (var[j + step_off], 0, 0, 0))],
        out_specs=pl.BlockSpec((1, nq, NA_WIDTH), lambda bi, j, var, ws: (bi, j, 0)),
        scratch_shapes=[pltpu.VMEM((NA_HEADS * nq, span * GRID_W), F32),
                        pltpu.VMEM((NA_HEADS * nq, n_ctx), F32)],
    )
    return pl.pallas_call(
        kern,
        grid_spec=grid_spec,
        out_shape=jax.ShapeDtypeStruct((b, n_steps * nq, NA_WIDTH), F32),
        compiler_params=_cparams(("arbitrary", "arbitrary")),
        name="na_attention",
    )(jnp.asarray(var_of_step), jnp.asarray(ws_of_step), q, k, v, table)


def _split3(x):
    hi = x.astype(BF16)
    r1 = x - hi.astype(F32)
    mid = r1.astype(BF16)
    lo = (r1 - mid.astype(F32)).astype(BF16)
    return hi, mid, lo


def _block_ref_rows(b, w, rev, upper_fill=None, lower_fill=None):
    c, n = b.shape
    off = w if rev else w - 1
    if upper_fill is not None or lower_fill is not None:
        parts = []
        for s in range(0, c, 2 * w):
            ref = jnp.broadcast_to(b[s + off:s + off + 1], (w, n))
            parts.append(ref if lower_fill is None else jnp.full((w, n), lower_fill, b.dtype))
            parts.append(ref if upper_fill is None else jnp.full((w, n), upper_fill, b.dtype))
        return jnp.concatenate(parts, axis=0)
    if 2 * w >= SUBLANES:
        parts = [jnp.broadcast_to(b[s + off:s + off + 1], (2 * w, n)) for s in range(0, c, 2 * w)]
        return parts[0] if len(parts) == 1 else jnp.concatenate(parts, axis=0)
    b3 = b.reshape(c // SUBLANES, SUBLANES, n)
    sub = lax.broadcasted_iota(jnp.int32, b3.shape, 1)
    r = None
    for s in range(0, SUBLANES, 2 * w):
        piece = jnp.broadcast_to(b3[:, s + off:s + off + 1, :], b3.shape)
        r = piece if r is None else jnp.where(sub >= s, piece, r)
    return r.reshape(c, n)


def _hgrn_chunks(chains, states, lmask_ref):
    n = len(chains)
    c = chains[0][0].shape[0]
    lg, kk = [], []
    for q, f, vb, lb, tri, negq_ref, negk_ref, rev, sid in chains:
        one_m_lb = 1.0 - lb
        g = lb + one_m_lb * jax.nn.sigmoid(f)
        lg.append(jnp.log(jnp.maximum(g, GATE_FLOOR)) * LOG2E)
        kk.append(one_m_lb * jax.nn.sigmoid(-f))
    pieces = [_split3(x) for x in lg]
    b = []
    for i in range(n):
        tri = chains[i][4]
        hi, mid, lo = pieces[i]
        b.append(jnp.dot(tri, hi, preferred_element_type=F32) + jnp.dot(tri, mid, preferred_element_type=F32)
                 + jnp.dot(tri, lo, preferred_element_type=F32))

    qb = [ch[0].astype(BF16) for ch in chains]
    kb = [x.astype(BF16) for x in kk]

    states = list(states)
    o = []
    for i in range(n):
        vb, rev, sid = chains[i][2], chains[i][7], chains[i][8]
        b_last = b[i][0:1] if rev else b[i][c - 1:c]
        qe = qb[i] * jnp.exp2(b[i]).astype(BF16)
        ke = kb[i] * jnp.exp2(b_last - b[i]).astype(BF16)
        st = states[sid]
        o.append(lax.dot_general(qe, st.astype(BF16), _NT, preferred_element_type=F32))
        states[sid] = st * jnp.exp2(b_last) + lax.dot_general(vb, ke, _TN, preferred_element_type=F32)

    a = [None] * n
    w = c // 2
    for li in range(lmask_ref.shape[0]):
        for i in range(n):
            negq_ref, negk_ref, rev = chains[i][5], chains[i][6], chains[i][7]
            if w >= SUBLANES:
                eq = b[i] - _block_ref_rows(b[i], w, rev, -MASKED if rev else None, None if rev else -MASKED)
                ek = _block_ref_rows(b[i], w, rev, None if rev else MASKED, MASKED if rev else None) - b[i]
            else:
                d = b[i] - _block_ref_rows(b[i], w, rev)
                eq = d + negq_ref[li]
                ek = negk_ref[li] - d
            qw = qb[i] * jnp.exp2(eq).astype(BF16)
            kw = kb[i] * jnp.exp2(ek).astype(BF16)
            p = lax.dot_general(qw, kw, _NT, preferred_element_type=F32) * lmask_ref[li]
            a[i] = p if a[i] is None else a[i] + p
        w //= 2
    for i in range(n):
        q, vb = chains[i][0], chains[i][2]
        o[i] = o[i] + jnp.dot(a[i].astype(BF16), vb, preferred_element_type=F32)
        o[i] = o[i] + jnp.sum(q * kk[i], axis=-1, keepdims=True) * vb.astype(F32)
    return o, states


def _hgrn_kernel(hq_ref, hf_ref, hb_ref, hi_ref, lbf_ref, lbb_ref, trif_ref, trir_ref,
                 negqf_ref, negkf_ref, negqr_ref, negkr_ref, lmask_ref,
                 o_ref, sf_ref, sb_ref, ob_ref, *, n_ctx_chunks, n_chunks):
    sf_ref[...] = jnp.zeros_like(sf_ref)
    sb_ref[...] = jnp.zeros_like(sb_ref)
    lbf = lbf_ref[0]
    lbb = lbb_ref[0]

    def steps(cf0, cb0, unroll):
        chains, rows = [], []
        for u in range(unroll):
            rf = pl.ds(pl.multiple_of((cf0 + u) * HG_CHUNK, HG_CHUNK), HG_CHUNK)
            rb = pl.ds(pl.multiple_of((cb0 - u) * HG_CHUNK, HG_CHUNK), HG_CHUNK)
            chains.append((hq_ref[0, rf, :], hf_ref[0, rf, :], hi_ref[0, rf, :], lbf,
                           trif_ref[...], negqf_ref, negkf_ref, False, 0))
            chains.append((hq_ref[0, rb, :], hb_ref[0, rb, :], hi_ref[0, rb, :], lbb,
                           trir_ref[...], negqr_ref, negkr_ref, True, 1))
            rows += [rf, rb]
        outs, (sf, sb) = _hgrn_chunks(chains, [sf_ref[...], sb_ref[...]], lmask_ref)
        for i, r in enumerate(rows):
            if i % 2 == 0:
                o_ref[0, r, :] = outs[i]
            else:
                ob_ref[r, :] = outs[i]
        sf_ref[...] = sf
        sb_ref[...] = sb

    n_lat = n_chunks - n_ctx_chunks
    u_ctx = math.gcd(HG_UNROLL, n_ctx_chunks)
    u_lat = math.gcd(HG_UNROLL, n_lat)

    def ctx_body(i, carry):
        steps(i * u_ctx, n_ctx_chunks - 1 - i * u_ctx, u_ctx)
        return carry

    def lat_body(i, carry):
        steps(n_ctx_chunks + i * u_lat, n_chunks - 1 - i * u_lat, u_lat)
        return carry

    lax.fori_loop(0, n_ctx_chunks // u_ctx, ctx_body, 0)
    lax.fori_loop(0, n_lat // u_lat, lat_body, 0)
    o_ref[0] = o_ref[0] + ob_ref[...]


def _hgrn_level_constants():
    c = HG_CHUNK
    t = np.arange(c)
    xor = t[:, None] ^ t[None, :]
    lmask, negq_f, negq_r = [], [], []
    w = c // 2
    while w >= 1:
        lmask.append(((xor >= w) & (xor < 2 * w)).astype(np.float32))
        upper = (t % (2 * w)) >= w
        negq_f.append(np.where(upper, 0.0, MASKED))
        negq_r.append(np.where(upper, MASKED, 0.0))
        w //= 2

    def rows(m):
        return jnp.asarray(np.broadcast_to(np.stack(m)[:, :, None], (len(m), c, HG_DIM)).astype(np.float32))

    tri_f = (t[None, :] <= t[:, None]).astype(np.float32)
    return (jnp.asarray(tri_f, BF16), jnp.asarray(tri_f.T, BF16),
            rows(negq_f), rows(negq_r), rows(negq_r), rows(negq_f), jnp.asarray(np.stack(lmask)))


def _hgrn(hq, hf, hb, hi, lb_f, lb_b, n_ctx):
    b, lt, _ = hq.shape
    kern = functools.partial(_hgrn_kernel, n_ctx_chunks=n_ctx // HG_CHUNK, n_chunks=lt // HG_CHUNK)
    seq = pl.BlockSpec((1, lt, HG_DIM), lambda bi, h: (bi, 0, h))
    lbs = pl.BlockSpec((1, 1, HG_DIM), lambda bi, h: (h, 0, 0))
    consts = _hgrn_level_constants()
    return pl.pallas_call(
        kern,
        grid=(b, HG_HEADS),
        in_specs=[seq, seq, seq, seq, lbs, lbs]
        + [pl.BlockSpec(a.shape, lambda bi, h, nd=a.ndim: (0,) * nd) for a in consts],
        out_specs=seq,
        out_shape=jax.ShapeDtypeStruct((b, lt, HG_WIDTH), F32),
        scratch_shapes=[pltpu.VMEM((HG_DIM, HG_DIM), F32), pltpu.VMEM((HG_DIM, HG_DIM), F32),
                        pltpu.VMEM((lt, HG_DIM), F32)],
        compiler_params=_cparams(("arbitrary", "arbitrary")),
        name="hgrn2",
    )(hq, hf, hb, hi, lb_f.reshape(HG_HEADS, 1, HG_DIM), lb_b.reshape(HG_HEADS, 1, HG_DIM), *consts)


def _split2(x):
    hi = x.astype(BF16)
    lo = (x - hi.astype(F32)).astype(BF16)
    return hi, lo


def _route(logits, bias):
    biased = jax.nn.sigmoid(logits) + bias
    rows = [biased[e:e + 1] for e in range(N_EXPERTS)]
    n = EXPERTS_PER_GROUP
    best = None
    g_sel = None
    for g in range(N_GROUPS):
        gs = None
        for i in range(n):
            for j2 in range(i + 1, n):
                pair = rows[g * n + i] + rows[g * n + j2]
                gs = pair if gs is None else jnp.maximum(gs, pair)
        if best is None:
            best, g_sel = gs, jnp.zeros(gs.shape, jnp.int32)
        else:
            take = gs > best
            best = jnp.where(take, gs, best)
            g_sel = jnp.where(take, g, g_sel)
    cand = []
    for i in range(n):
        c_i = rows[(N_GROUPS - 1) * n + i]
        for g in range(N_GROUPS - 2, -1, -1):
            c_i = jnp.where(g_sel == g, rows[g * n + i], c_i)
        cand.append(c_i)
    m1, i1 = cand[0], jnp.zeros(best.shape, jnp.int32)
    for i in range(1, n):
        take = cand[i] > m1
        m1 = jnp.where(take, cand[i], m1)
        i1 = jnp.where(take, i, i1)
    m2 = jnp.full(best.shape, -jnp.inf, F32)
    i2 = jnp.zeros(best.shape, jnp.int32)
    for i in range(n):
        take = (i1 != i) & (cand[i] > m2)
        m2 = jnp.where(take, cand[i], m2)
        i2 = jnp.where(take, i, i2)
    lo = jnp.minimum(i1, i2)
    hi = jnp.maximum(i1, i2)
    pair = jnp.where(lo == 0, hi - 1, jnp.where(lo == 1, hi + 1, N_PAIRS - 1))
    return g_sel * N_PAIRS + pair


def _merge_kernel(ona_ref, ohg_ref, og_ref, ga_ref, gh_ref, x_ref, mod_ref, hgg_ref, n2g_ref,
                  wna_ref, whg_ref, wout_ref, wr_ref, rb_ref,
                  xn_ref, h2_ref, bucket_ref, rank_ref, cnt_ref, carry_ref):
    first = (pl.program_id(0) == 0) & (pl.program_id(1) == 0)

    @pl.when(first)
    def _():
        carry_ref[...] = jnp.zeros_like(carry_ref)

    ohg = ohg_ref[0]
    og = og_ref[0]
    gain = hgg_ref[...]
    heads = []
    for h in range(HG_HEADS):
        sl = slice(h * HG_DIM, (h + 1) * HG_DIM)
        oh = ohg[:, sl]
        yh = oh * lax.rsqrt(jnp.mean(oh * oh, axis=-1, keepdims=True) + EPS) * gain
        gt = og[:, sl]
        heads.append((yh * (gt * jax.nn.sigmoid(gt))).astype(BF16))
    hn = jnp.concatenate(heads, axis=-1)
    y_na = jnp.dot(ona_ref[0].astype(BF16), wna_ref[...], preferred_element_type=F32)
    y_hg = jnp.dot(hn, whg_ref[...], preferred_element_type=F32)
    m = jax.nn.sigmoid(ga_ref[0]) * y_na + jax.nn.sigmoid(gh_ref[0]) * y_hg
    y = jnp.dot(m.astype(BF16), wout_ref[...], preferred_element_type=F32)
    xn = x_ref[0] + mod_ref[0, 2:3, :] * y
    xn_ref[0] = xn
    yn = xn * lax.rsqrt(jnp.mean(xn * xn, axis=-1, keepdims=True) + EPS) * n2g_ref[...]
    h2 = yn * (1.0 + mod_ref[0, 4:5, :]) + mod_ref[0, 3:4, :]
    h2_ref[...] = h2

    w_hi, w_lo = _split2(wr_ref[...])
    h_hi, h_lo = _split2(h2)
    logits = (lax.dot_general(w_hi, h_hi, _NT, preferred_element_type=F32)
              + lax.dot_general(w_hi, h_lo, _NT, preferred_element_type=F32)
              + lax.dot_general(w_lo, h_hi, _NT, preferred_element_type=F32))
    bucket = _route(logits, rb_ref[...])
    bucket_ref[0] = bucket

    t = bucket.shape[1]
    onehot = (lax.broadcasted_iota(jnp.int32, (BUCKET_ROWS, t), 0) == bucket).astype(F32)
    before = (lax.broadcasted_iota(jnp.int32, (t, t), 0)
              < lax.broadcasted_iota(jnp.int32, (t, t), 1)).astype(BF16)
    prefix = jnp.dot(onehot.astype(BF16), before, preferred_element_type=F32)
    carry = carry_ref[...]
    rank = jnp.sum(onehot * (prefix + carry[:, 0:1]), axis=0, keepdims=True)
    rank_ref[0] = rank.astype(jnp.int32)
    carry = carry + jnp.sum(onehot, axis=1, keepdims=True)
    carry_ref[...] = carry
    cnt_ref[...] = carry.astype(jnp.int32)


def _merge(o_na, o_hg, og, ga, gh, xc, mods, hg_gain, n2_gain, w_na_o, w_hg_o, w_out, w_router_t,
           router_bias, ctx_row, skip):
    b, lt, d = xc.shape
    n_t = lt // ROW_TILE - skip

    def rows(width):
        return pl.BlockSpec((1, ROW_TILE, width), lambda bi, j: (bi, j + skip, 0))

    def full(a):
        return pl.BlockSpec(a.shape, lambda bi, j: (0,) * a.ndim)

    def mod_map(bi, j):
        return (jnp.where(j + skip == 0, ctx_row, bi), 0, 0)

    tok = pl.BlockSpec((1, 1, ROW_TILE), lambda bi, j: (bi * n_t + j, 0, 0))
    consts = [hg_gain, n2_gain, w_na_o, w_hg_o, w_out, w_router_t, router_bias]
    return pl.pallas_call(
        _merge_kernel,
        grid=(b, n_t),
        in_specs=[pl.BlockSpec((1, ROW_TILE, NA_WIDTH), lambda bi, j: (bi, j, 0)),
                  rows(HG_WIDTH), rows(HG_WIDTH), rows(d), rows(d), rows(d),
                  pl.BlockSpec((1, 6, d), mod_map)] + [full(a) for a in consts],
        out_specs=[pl.BlockSpec((1, ROW_TILE, d), lambda bi, j: (bi, j, 0)),
                   pl.BlockSpec((ROW_TILE, d), lambda bi, j: (bi * n_t + j, 0)),
                   tok, tok,
                   pl.BlockSpec((BUCKET_ROWS, LANES), lambda bi, j: (0, 0))],
        out_shape=[jax.ShapeDtypeStruct((b, n_t * ROW_TILE, d), F32),
                   jax.ShapeDtypeStruct((b * n_t * ROW_TILE, d), F32),
                   jax.ShapeDtypeStruct((b * n_t, 1, ROW_TILE), jnp.int32),
                   jax.ShapeDtypeStruct((b * n_t, 1, ROW_TILE), jnp.int32),
                   jax.ShapeDtypeStruct((BUCKET_ROWS, LANES), jnp.int32)],
        scratch_shapes=[pltpu.VMEM((BUCKET_ROWS, LANES), F32)],
        compiler_params=_cparams(("arbitrary", "arbitrary")),
        name="merge_router",
    )(o_na, o_hg, og, ga, gh, xc, mods, *consts)


def _start_row_copies(n, row_copy):
    def body(g, carry):
        for u in range(ISSUE_UNROLL):
            row_copy(g * ISSUE_UNROLL + u).start()
        return carry

    lax.fori_loop(0, n // ISSUE_UNROLL, body, 0)


def _scatter_kernel(dest_ref, h_ref, xs_in_ref, xs_ref, sem):
    del xs_in_ref
    n = h_ref.shape[0]
    _start_row_copies(n, lambda r: pltpu.make_async_copy(
        h_ref.at[pl.ds(r, 1)], xs_ref.at[pl.ds(dest_ref[0, 0, r], 1)], sem))
    pltpu.make_async_copy(h_ref, xs_ref.at[pl.ds(0, n)], sem).wait()


def _scatter_rows(h2, dest, n_sorted):
    t, d = h2.shape
    tile = 2 * ROW_TILE if t % (2 * ROW_TILE) == 0 else ROW_TILE
    n_t = t // tile
    return pl.pallas_call(
        _scatter_kernel,
        grid=(n_t,),
        in_specs=[pl.BlockSpec((1, 1, tile), lambda i: (i, 0, 0), memory_space=pltpu.SMEM),
                  pl.BlockSpec((tile, d), lambda i: (i, 0)),
                  pl.BlockSpec(memory_space=pl.ANY)],
        out_specs=pl.BlockSpec(memory_space=pl.ANY),
        out_shape=jax.ShapeDtypeStruct((n_sorted, d), F32),
        scratch_shapes=[pltpu.SemaphoreType.DMA(())],
        input_output_aliases={2: 0},
        compiler_params=_cparams(("arbitrary",)),
        name="scatter_rows",
    )(dest.reshape(n_t, 1, tile), h2, jnp.zeros((n_sorted, d), F32))


def _final_kernel(dest_ref, dest_next_ref, xn_ref, mod_ref, g_ref, ys_ref, o_ref, buf_ref, sems,
                  *, final_norm):
    n = buf_ref.shape[1]
    step = pl.program_id(0) * pl.num_programs(1) + pl.program_id(1)
    n_steps = pl.num_programs(0) * pl.num_programs(1)

    def gather(d_ref, slot):
        _start_row_copies(n, lambda r: pltpu.make_async_copy(
            ys_ref.at[pl.ds(d_ref[0, 0, r], 1)], buf_ref.at[slot, pl.ds(r, 1)], sems.at[slot]))

    @pl.when(step == 0)
    def _():
        gather(dest_ref, 0)

    @pl.when(step + 1 < n_steps)
    def _():
        gather(dest_next_ref, (step + 1) % 2)

    slot = step % 2
    pltpu.make_async_copy(ys_ref.at[pl.ds(0, n)], buf_ref.at[slot], sems.at[slot]).wait()
    x = xn_ref[0] + mod_ref[0, 5:6, :] * buf_ref[slot]
    if final_norm:
        x = x * lax.rsqrt(jnp.mean(x * x, axis=-1, keepdims=True) + EPS) * g_ref[...]
    o_ref[0] = x


def _gather_residual(xn, mods, ys, dest, gain, ctx_row, final_norm, skip):
    b, lt, d = xn.shape
    n_t = lt // ROW_TILE

    def mod_map(bi, j):
        return (jnp.where(j + skip == 0, ctx_row, bi), 0, 0)

    kern = functools.partial(_final_kernel, final_norm=final_norm)
    last_tile = b * n_t - 1
    dest3 = dest.reshape(b * n_t, 1, ROW_TILE)
    return pl.pallas_call(
        kern,
        grid=(b, n_t),
        in_specs=[pl.BlockSpec((1, 1, ROW_TILE), lambda bi, j: (bi * n_t + j, 0, 0),
                               memory_space=pltpu.SMEM),
                  pl.BlockSpec((1, 1, ROW_TILE), lambda bi, j: (jnp.minimum(bi * n_t + j + 1, last_tile), 0, 0),
                               memory_space=pltpu.SMEM),
                  pl.BlockSpec((1, ROW_TILE, d), lambda bi, j: (bi, j, 0)),
                  pl.BlockSpec((1, 6, d), mod_map),
                  pl.BlockSpec((1, d), lambda bi, j: (0, 0)),
                  pl.BlockSpec(memory_space=pl.ANY)],
        out_specs=pl.BlockSpec((1, ROW_TILE, d), lambda bi, j: (bi, j, 0)),
        out_shape=jax.ShapeDtypeStruct((b, lt, d), F32),
        scratch_shapes=[pltpu.VMEM((2, ROW_TILE, d), F32), pltpu.SemaphoreType.DMA((2,))],
        compiler_params=_cparams(("arbitrary", "arbitrary")),
        name="gather_residual",
    )(dest3, dest3, xn, mods, gain, ys)


def _moe_kernel(ea_ref, eb_ref, nused_ref, xs_ref, wra_ref, wrb_ref,
                wga_ref, wua_ref, wda_ref, wgb_ref, wub_ref, wdb_ref, ys_ref):
    del ea_ref, eb_ref
    i = pl.program_id(0)

    @pl.when(i < nused_ref[0])
    def _():
        x = xs_ref[...]
        s_a = jax.nn.sigmoid(jnp.sum(x * wra_ref[0], axis=-1, keepdims=True))
        s_b = jax.nn.sigmoid(jnp.sum(x * wrb_ref[0], axis=-1, keepdims=True))
        tot = s_a + s_b
        xb = x.astype(BF16)

        def expert(wg, wu, wd):
            gate = jnp.dot(xb, wg[0], preferred_element_type=F32)
            up = jnp.dot(xb, wu[0], preferred_element_type=F32)
            hid = (gate * jax.nn.sigmoid(gate) * up).astype(BF16)
            return jnp.dot(hid, wd[0], preferred_element_type=F32)

        ys_ref[...] = ((s_a / tot) * expert(wga_ref, wua_ref, wda_ref)
                       + (s_b / tot) * expert(wgb_ref, wub_ref, wdb_ref))

    @pl.when(i >= nused_ref[0])
    def _():
        ys_ref[...] = jnp.zeros_like(ys_ref)


def _moe(xs, tile_ea, tile_eb, n_used, w_router_rows, w_gate, w_up, w_down):
    n_sorted, d = xs.shape
    n_tiles = n_sorted // ROW_TILE
    ff = w_gate.shape[-1]

    def by_a(i, ea, eb, nu):
        return (ea[i], 0, 0)

    def by_b(i, ea, eb, nu):
        return (eb[i], 0, 0)

    grid_spec = pltpu.PrefetchScalarGridSpec(
        num_scalar_prefetch=3,
        grid=(n_tiles,),
        in_specs=[pl.BlockSpec((ROW_TILE, d), lambda i, ea, eb, nu: (i, 0)),
                  pl.BlockSpec((1, 1, d), by_a), pl.BlockSpec((1, 1, d), by_b),
                  pl.BlockSpec((1, d, ff), by_a), pl.BlockSpec((1, d, ff), by_a),
                  pl.BlockSpec((1, ff, d), by_a),
                  pl.BlockSpec((1, d, ff), by_b), pl.BlockSpec((1, d, ff), by_b),
                  pl.BlockSpec((1, ff, d), by_b)],
        out_specs=pl.BlockSpec((ROW_TILE, d), lambda i, ea, eb, nu: (i, 0)),
    )
    return pl.pallas_call(
        _moe_kernel,
        grid_spec=grid_spec,
        out_shape=jax.ShapeDtypeStruct((n_sorted, d), F32),
        compiler_params=_cparams(("arbitrary",)),
        name="moe_pairs",
    )(tile_ea, tile_eb, n_used, xs, w_router_rows, w_router_rows,
      w_gate, w_up, w_down, w_gate, w_up, w_down)


_PAIR_LO = np.array([0, 0, 0, 1, 1, 2], np.int32)
_PAIR_HI = np.array([1, 2, 3, 2, 3, 3], np.int32)


def _sorted_layout(bucket, rank, counts, n_tiles):
    counts = counts[:N_BUCKETS]
    padded = ((counts + ROW_TILE - 1) // ROW_TILE) * ROW_TILE
    ends = jnp.cumsum(padded)
    starts = ends - padded
    dest = starts[bucket] + rank
    tile_start = jnp.arange(n_tiles, dtype=jnp.int32) * ROW_TILE
    tile_bucket = jnp.sum((ends[None, :] <= tile_start[:, None]).astype(jnp.int32), axis=1)
    tile_bucket = jnp.minimum(tile_bucket, N_BUCKETS - 1)
    group = tile_bucket // N_PAIRS
    pair = tile_bucket % N_PAIRS
    ea = group * EXPERTS_PER_GROUP + jnp.asarray(_PAIR_LO)[pair]
    eb = group * EXPERTS_PER_GROUP + jnp.asarray(_PAIR_HI)[pair]
    n_used = (ends[-1] // ROW_TILE).astype(jnp.int32).reshape(1)
    return dest.astype(jnp.int32), ea.astype(jnp.int32), eb.astype(jnp.int32), n_used


def _lower_bounds(raw):
    p = jax.nn.softmax(raw.astype(F32), axis=0)
    return jnp.cumsum(p, axis=0) - p[0:1]


def kernel(x, c, ctx, c_ctx, w_ada, b_ada, norm1_g, w_in, na_rel_bias, hg_lower_fwd, hg_lower_bwd,
           hg_norm_g, w_na_o, w_hg_o, w_out, norm2_g, w_router, router_bias, w_gate, w_up, w_down,
           final_g):
    b, seq, d = x.shape
    n_ctx = ctx.shape[1]
    depth = w_ada.shape[0]
    lt = n_ctx + seq
    assert n_ctx % ROW_TILE == 0 and seq % ROW_TILE == 0 and seq % GRID_W == 0
    assert n_ctx % HG_CHUNK == 0 and seq % HG_CHUNK == 0
    assert seq // GRID_W >= NA_WIN_ROWS + NA_Q_ROWS - 1 and (seq // GRID_W) % NA_Q_ROWS == 0
    assert n_ctx % (NA_Q_ROWS * GRID_W) == 0

    ada_rows = -(-(b + 1) // 8) * 8
    ctx_row = b
    cc = jnp.concatenate([c, c_ctx[None, :], jnp.zeros((ada_rows - b - 1, d), F32)], axis=0)
    mods = _ada(cc, w_ada, b_ada).reshape(depth, ada_rows, 6, d)

    lb_f = _lower_bounds(hg_lower_fwd)
    lb_b = _lower_bounds(hg_lower_bwd)
    cos_t, sin_t = _rope_tables(n_ctx, seq)
    w_router_t = jnp.transpose(w_router)
    w_router_rows = w_router_t.reshape(N_EXPERTS, 1, d)
    rb = router_bias.astype(F32).reshape(N_EXPERTS, 1)

    xc = jnp.concatenate([ctx, x], axis=1)
    out = None
    for l in range(depth):
        last = l == depth - 1
        skip = n_ctx // ROW_TILE if last else 0
        n_tiles = b * (lt // ROW_TILE - skip) + N_BUCKETS
        q, k, v, hq, hf, hb, hi, og, ga, gh = _inproj(
            xc, mods[l], norm1_g[l].reshape(1, d), w_in[l].astype(BF16), cos_t, sin_t, ctx_row)
        o_na = _na_attention(q, k, v, na_rel_bias[l], n_ctx, with_ctx=not last)
        o_hg = _hgrn(hq, hf, hb, hi, lb_f[l], lb_b[l], n_ctx)
        xn, h2, bucket, rank, counts = _merge(
            o_na, o_hg, og, ga, gh, xc, mods[l], hg_norm_g[l].reshape(1, HG_DIM),
            norm2_g[l].reshape(1, d), w_na_o[l].astype(BF16), w_hg_o[l].astype(BF16),
            w_out[l].astype(BF16), w_router_t, rb, ctx_row, skip)
        dest, tile_ea, tile_eb, n_used = _sorted_layout(
            bucket.reshape(-1), rank.reshape(-1), counts[:, 0], n_tiles)
        xs = _scatter_rows(h2, dest, n_tiles * ROW_TILE)
        ys = _moe(xs, tile_ea, tile_eb, n_used, w_router_rows,
                  w_gate[l].astype(BF16), w_up[l].astype(BF16), w_down[l].astype(BF16))
        res = _gather_residual(xn, mods[l], ys, dest, final_g.reshape(1, d), ctx_row, last, skip)
        if last:
            out = res
        else:
            xc = res
    return out
```

```python
import functools
import math

import jax
import jax.numpy as jnp
import numpy as np
from jax import lax
from jax.experimental import pallas as pl
from jax.experimental.pallas import tpu as pltpu

F32 = jnp.float32
BF16 = jnp.bfloat16

GRID_W = 64
EPS = 1e-6
NA_HEADS = 8
NA_HEAD_DIM = 64
NA_WIDTH = NA_HEADS * NA_HEAD_DIM
NA_WIN_ROWS = 8
NA_WIN_COLS = 16
ROPE_THETA = 10000.0
HG_HEADS = 4
HG_DIM = 128
HG_WIDTH = HG_HEADS * HG_DIM
GATE_FLOOR = 1e-30
N_EXPERTS = 16
N_GROUPS = 4
EXPERTS_PER_GROUP = 4
N_PAIRS = 6
N_BUCKETS = N_GROUPS * N_PAIRS
BUCKET_ROWS = 32
MASKED = -1e30
LOG2E = 1.4426950408889634

LANES = 128
ROW_TILE = 256
NA_Q_ROWS = 2
SUBLANES = 8
HG_CHUNK = 128
HG_UNROLL = 4
VMEM_LIMIT = 56 * 1024 * 1024

_NT = (((1,), (1,)), ((), ()))
_TN = (((0,), (0,)), ((), ()))


def _cparams(sem):
    return pltpu.CompilerParams(dimension_semantics=sem, vmem_limit_bytes=VMEM_LIMIT)


def _ada_kernel(c_ref, w_ref, b_ref, o_ref):
    cc = c_ref[...]
    s = cc * jax.nn.sigmoid(cc)
    o_ref[0] = jnp.dot(s, w_ref[0], preferred_element_type=F32) + b_ref[0]


def _ada(cc, w_ada, b_ada):
    depth, d, n = w_ada.shape
    rows = cc.shape[0]
    tn = 1536
    return pl.pallas_call(
        _ada_kernel,
        grid=(depth, n // tn),
        in_specs=[pl.BlockSpec((rows, d), lambda l, j: (0, 0)),
                  pl.BlockSpec((1, d, tn), lambda l, j: (l, 0, j)),
                  pl.BlockSpec((1, 1, tn), lambda l, j: (l, 0, j))],
        out_specs=pl.BlockSpec((1, rows, tn), lambda l, j: (l, 0, j)),
        out_shape=jax.ShapeDtypeStruct((depth, rows, n), F32),
        compiler_params=_cparams(("arbitrary", "arbitrary")),
        name="ada",
    )(cc, w_ada, b_ada.reshape(depth, 1, n))


def _inproj_kernel(x_ref, mod_ref, g_ref, w_ref, cos_ref, sin_ref,
                   q_ref, k_ref, v_ref, hq_ref, hf_ref, hb_ref, hi_ref, og_ref, ga_ref, gh_ref):
    x = x_ref[0]
    y = x * lax.rsqrt(jnp.mean(x * x, axis=-1, keepdims=True) + EPS) * g_ref[...]
    h = (y * (1.0 + mod_ref[0, 1:2, :]) + mod_ref[0, 0:1, :]).astype(BF16)

    def proj(off, width):
        return jnp.dot(h, w_ref[:, off:off + width], preferred_element_type=F32)

    cos = cos_ref[...]
    sin = sin_ref[...]
    even = (lax.broadcasted_iota(jnp.int32, cos.shape, 1) % 2) == 0

    def rope(z):
        swapped = jnp.where(even, pltpu.roll(z, LANES - 1, 1), pltpu.roll(z, 1, 1))
        return z * cos + swapped * sin

    scale = NA_HEAD_DIM ** -0.5 * LOG2E
    for blk in range(NA_WIDTH // LANES):
        sl = slice(blk * LANES, (blk + 1) * LANES)
        q_ref[0, :, sl] = (rope(proj(blk * LANES, LANES)) * scale).astype(q_ref.dtype)
        k_ref[0, :, sl] = rope(proj(NA_WIDTH + blk * LANES, LANES)).astype(k_ref.dtype)
    off = 2 * NA_WIDTH
    v_ref[0] = proj(off, NA_WIDTH).astype(v_ref.dtype)
    off += NA_WIDTH
    hq = proj(off, HG_WIDTH)
    hq_ref[0] = hq * jax.nn.sigmoid(hq)
    off += HG_WIDTH
    hf_ref[0] = proj(off, HG_WIDTH)
    off += HG_WIDTH
    hb_ref[0] = proj(off, HG_WIDTH)
    off += HG_WIDTH
    hi_ref[0] = proj(off, HG_WIDTH).astype(hi_ref.dtype)
    off += HG_WIDTH
    og_ref[0] = proj(off, HG_WIDTH)
    off += HG_WIDTH
    d = x.shape[-1]
    for half in range(2):
        sl = slice(half * (d // 2), (half + 1) * (d // 2))
        ga_ref[0, :, sl] = proj(off + half * (d // 2), d // 2)
        gh_ref[0, :, sl] = proj(off + d + half * (d // 2), d // 2)


def _inproj(xc, mods, g, w_bf16, cos_t, sin_t, ctx_row):
    b, lt, d = xc.shape
    n_t = lt // ROW_TILE

    def rows(width):
        return pl.BlockSpec((1, ROW_TILE, width), lambda bi, j: (bi, j, 0))

    def mod_map(bi, j):
        return (jnp.where(j == 0, ctx_row, bi), 0, 0)

    widths = [NA_WIDTH] * 3 + [HG_WIDTH] * 5 + [d, d]
    dtypes = [BF16, BF16, BF16, F32, F32, F32, BF16, F32, F32, F32]
    return pl.pallas_call(
        _inproj_kernel,
        grid=(b, n_t),
        in_specs=[rows(d),
                  pl.BlockSpec((1, 6, d), mod_map),
                  pl.BlockSpec((1, d), lambda bi, j: (0, 0)),
                  pl.BlockSpec(w_bf16.shape, lambda bi, j: (0, 0)),
                  pl.BlockSpec((ROW_TILE, LANES), lambda bi, j: (j, 0)),
                  pl.BlockSpec((ROW_TILE, LANES), lambda bi, j: (j, 0))],
        out_specs=[rows(w) for w in widths],
        out_shape=[jax.ShapeDtypeStruct((b, lt, w), dt) for w, dt in zip(widths, dtypes)],
        compiler_params=_cparams(("arbitrary", "arbitrary")),
        name="inproj",
    )(xc, mods, g, w_bf16, cos_t, sin_t)


def _rope_tables(n_ctx, seq):
    t = jnp.arange(seq, dtype=jnp.int32)
    row = (t // GRID_W).astype(F32)
    col = (t % GRID_W).astype(F32)
    rot_half = NA_HEAD_DIM // 2
    inv = ROPE_THETA ** (-jnp.arange(0, rot_half, 2, dtype=F32) / rot_half)
    ang = jnp.concatenate([row[:, None] * inv, col[:, None] * inv], axis=-1)
    cos = jnp.repeat(jnp.cos(ang), 2, axis=-1)
    sin = jnp.repeat(jnp.sin(ang), 2, axis=-1)
    sign = jnp.asarray(np.tile(np.array([-1.0, 1.0], np.float32), NA_HEAD_DIM // 2))
    sin = sin * sign
    cos = jnp.concatenate([jnp.ones((n_ctx, NA_HEAD_DIM), F32), cos], axis=0)
    sin = jnp.concatenate([jnp.zeros((n_ctx, NA_HEAD_DIM), F32), sin], axis=0)
    reps = LANES // NA_HEAD_DIM
    return jnp.tile(cos, (1, reps)), jnp.tile(sin, (1, reps))


def _na_plan(n_ctx, grid_rows):
    span = NA_WIN_ROWS + NA_Q_ROWS - 1
    variants, var_of_step, ws_of_step = [], [], []
    for r0 in range(0, grid_rows, NA_Q_ROWS):
        rs = [min(max(r0 + dq - NA_WIN_ROWS // 2, 0), grid_rows - NA_WIN_ROWS) for dq in range(NA_Q_ROWS)]
        ws = min(rs[0], grid_rows - span)
        key = (r0 - ws,) + tuple(r - ws for r in rs)
        if key not in variants:
            variants.append(key)
        var_of_step.append(variants.index(key))
        ws_of_step.append(ws)
    ctx_steps = n_ctx // (NA_Q_ROWS * GRID_W)
    var_of_step = [len(variants)] * ctx_steps + var_of_step
    ws_of_step = [0] * ctx_steps + ws_of_step
    return variants, np.asarray(var_of_step, np.int32), np.asarray(ws_of_step, np.int32)


def _na_bias_planes(rel_bias, variants):
    span = NA_WIN_ROWS + NA_Q_ROWS - 1
    n_dr, n_dc = 2 * NA_WIN_ROWS - 1, 2 * NA_WIN_COLS - 1
    col = np.arange(GRID_W)[:, None]
    kc = np.arange(GRID_W)[None, :]
    cs = np.clip(col - NA_WIN_COLS // 2, 0, GRID_W - NA_WIN_COLS)
    col_ok = (kc >= cs) & (kc < cs + NA_WIN_COLS)
    oh_c = ((kc - col + NA_WIN_COLS - 1)[..., None] == np.arange(n_dc)) & col_ok[..., None]
    planes = jnp.einsum("hrc,xkc->hrxk", rel_bias.astype(F32), jnp.asarray(oh_c.astype(np.float32)),
                        precision=lax.Precision.HIGHEST)
    planes = jnp.where(jnp.asarray(col_ok), planes * LOG2E, MASKED)
    planes = jnp.concatenate([planes, jnp.full_like(planes[:, :1], MASKED)], axis=1)
    plane_of = np.full((len(variants) + 1, NA_Q_ROWS, span), n_dr, np.int32)
    for vi, key in enumerate(variants):
        for dq in range(NA_Q_ROWS):
            for j in range(key[1 + dq], key[1 + dq] + NA_WIN_ROWS):
                plane_of[vi, dq, j] = j - (key[0] + dq) + NA_WIN_ROWS - 1
    return planes, plane_of.reshape(-1)


def _na_kernel(var_ref, ws_ref, plane_ref, q_ref, k_ref, v_ref, b_ref, o_ref, sw_ref, sc_ref, t_ref,
               *, step_off, n_ctx):
    nq = q_ref.shape[1]
    win = sw_ref.shape[1]
    span = win // GRID_W
    j = pl.program_id(1) + step_off
    var = var_ref[j]

    @pl.when((pl.program_id(1) == 0) | (var != var_ref[jnp.maximum(j - 1, 0)]))
    def _():
        for h in range(NA_HEADS):
            for dq in range(nq // GRID_W):
                base = (var * (nq // GRID_W) + dq) * span
                row = jnp.concatenate([b_ref[h, plane_ref[base + jj]] for jj in range(span)], axis=1)
                t_ref[h, dq * GRID_W:(dq + 1) * GRID_W, :] = row

    start = pl.multiple_of(n_ctx + ws_ref[j] * GRID_W, GRID_W)
    low = lax.broadcasted_iota(jnp.int32, (nq, LANES), 1) < NA_HEAD_DIM
    for hp in range(NA_HEADS // 2):
        sl = slice(hp * LANES, (hp + 1) * LANES)
        q2 = q_ref[0, :, sl]
        kw = k_ref[0, pl.ds(start, win), sl]
        kc = k_ref[0, 0:n_ctx, sl]
        for par in range(2):
            h = 2 * hp + par
            qh = jnp.where(low if par == 0 else jnp.logical_not(low), q2, jnp.zeros_like(q2))
            rows = slice(h * nq, (h + 1) * nq)
            sw_ref[rows, :] = lax.dot_general(qh, kw, _NT, preferred_element_type=F32) + t_ref[h]
            sc_ref[rows, :] = lax.dot_general(qh, kc, _NT, preferred_element_type=F32)
    sw = sw_ref[...]
    sc = sc_ref[...]
    m = jnp.maximum(jnp.max(sw, axis=-1, keepdims=True), jnp.max(sc, axis=-1, keepdims=True))
    pw = jnp.exp2(sw - m)
    pc = jnp.exp2(sc - m)
    inv = 1.0 / (jnp.sum(pw, axis=-1, keepdims=True) + jnp.sum(pc, axis=-1, keepdims=True))
    pw = pw.astype(BF16)
    pc = pc.astype(BF16)
    for hp in range(NA_HEADS // 2):
        sl = slice(hp * LANES, (hp + 1) * LANES)
        vw = v_ref[0, pl.ds(start, win), sl]
        vc = v_ref[0, 0:n_ctx, sl]
        pair = []
        for par in range(2):
            rows = slice((2 * hp + par) * nq, (2 * hp + par + 1) * nq)
            o = (jnp.dot(pw[rows], vw, preferred_element_type=F32)
                 + jnp.dot(pc[rows], vc, preferred_element_type=F32))
            pair.append(o * inv[rows])
        o_ref[0, :, sl] = jnp.where(low, pair[0], pair[1])


def _na_attention(q, k, v, rel_bias, n_ctx, with_ctx):
    b, lt, _ = q.shape
    nq = NA_Q_ROWS * GRID_W
    span = NA_WIN_ROWS + NA_Q_ROWS - 1
    variants, var_of_step, ws_of_step = _na_plan(n_ctx, (lt - n_ctx) // GRID_W)
    planes, plane_of = _na_bias_planes(rel_bias, variants)
    step_off = 0 if with_ctx else n_ctx // nq
    n_steps = lt // nq - step_off

    kern = functools.partial(_na_kernel, step_off=step_off, n_ctx=n_ctx)
    grid_spec = pltpu.PrefetchScalarGridSpec(
        num_scalar_prefetch=3,
        grid=(b, n_steps),
        in_specs=[pl.BlockSpec((1, nq, NA_WIDTH), lambda bi, j, *_: (bi, j + step_off, 0)),
                  pl.BlockSpec((1, lt, NA_WIDTH), lambda bi, j, *_: (bi, 0, 0)),
                  pl.BlockSpec((1, lt, NA_WIDTH), lambda bi, j, *_: (bi, 0, 0)),
                  pl.BlockSpec(planes.shape, lambda bi, j, *_: (0, 0, 0, 0))],
        out_specs=pl.BlockSpec((1, nq, NA_WIDTH), lambda bi, j, *_: (bi, j, 0)),
        scratch_shapes=[pltpu.VMEM((NA_HEADS * nq, span * GRID_W), F32),
                        pltpu.VMEM((NA_HEADS * nq, n_ctx), F32),
                        pltpu.VMEM((NA_HEADS, nq, span * GRID_W), F32)],
    )
    return pl.pallas_call(
        kern,
        grid_spec=grid_spec,
        out_shape=jax.ShapeDtypeStruct((b, n_steps * nq, NA_WIDTH), F32),
        compiler_params=_cparams(("arbitrary", "arbitrary")),
        name="na_attention",
    )(jnp.asarray(var_of_step), jnp.asarray(ws_of_step), jnp.asarray(plane_of), q, k, v, planes)


def _split3(x):
    hi = x.astype(BF16)
    r1 = x - hi.astype(F32)
    mid = r1.astype(BF16)
    lo = (r1 - mid.astype(F32)).astype(BF16)
    return hi, mid, lo


def _block_ref_rows(b, w, rev, upper_fill=None, lower_fill=None):
    c, n = b.shape
    off = w if rev else w - 1
    if upper_fill is not None or lower_fill is not None:
        parts = []
        for s in range(0, c, 2 * w):
            ref = jnp.broadcast_to(b[s + off:s + off + 1], (w, n))
            parts.append(ref if lower_fill is None else jnp.full((w, n), lower_fill, b.dtype))
            parts.append(ref if upper_fill is None else jnp.full((w, n), upper_fill, b.dtype))
        return jnp.concatenate(parts, axis=0)
    if 2 * w >= SUBLANES:
        parts = [jnp.broadcast_to(b[s + off:s + off + 1], (2 * w, n)) for s in range(0, c, 2 * w)]
        return parts[0] if len(parts) == 1 else jnp.concatenate(parts, axis=0)
    b3 = b.reshape(c // SUBLANES, SUBLANES, n)
    sub = lax.broadcasted_iota(jnp.int32, b3.shape, 1)
    r = None
    for s in range(0, SUBLANES, 2 * w):
        piece = jnp.broadcast_to(b3[:, s + off:s + off + 1, :], b3.shape)
        r = piece if r is None else jnp.where(sub >= s, piece, r)
    return r.reshape(c, n)


def _hgrn_chunks(chains, states, lmask_ref):
    n = len(chains)
    c = chains[0][0].shape[0]
    lg, kk = [], []
    for q, f, vb, lb, tri, negq_ref, negk_ref, rev, sid in chains:
        one_m_lb = 1.0 - lb
        g = lb + one_m_lb * jax.nn.sigmoid(f)
        lg.append(jnp.log(jnp.maximum(g, GATE_FLOOR)) * LOG2E)
        kk.append(one_m_lb * jax.nn.sigmoid(-f))
    pieces = [_split3(x) for x in lg]
    b = []
    for i in range(n):
        tri = chains[i][4]
        hi, mid, lo = pieces[i]
        b.append(jnp.dot(tri, hi, preferred_element_type=F32) + jnp.dot(tri, mid, preferred_element_type=F32)
                 + jnp.dot(tri, lo, preferred_element_type=F32))

    qb = [ch[0].astype(BF16) for ch in chains]
    kb = [x.astype(BF16) for x in kk]

    states = list(states)
    o = []
    for i in range(n):
        vb, rev, sid = chains[i][2], chains[i][7], chains[i][8]
        b_last = b[i][0:1] if rev else b[i][c - 1:c]
        qe = qb[i] * jnp.exp2(b[i]).astype(BF16)
        ke = kb[i] * jnp.exp2(b_last - b[i]).astype(BF16)
        st = states[sid]
        o.append(lax.dot_general(qe, st.astype(BF16), _NT, preferred_element_type=F32))
        states[sid] = st * jnp.exp2(b_last) + lax.dot_general(vb, ke, _TN, preferred_element_type=F32)

    a = [None] * n
    w = c // 2
    for li in range(lmask_ref.shape[0]):
        for i in range(n):
            negq_ref, negk_ref, rev = chains[i][5], chains[i][6], chains[i][7]
            if w >= SUBLANES:
                eq = b[i] - _block_ref_rows(b[i], w, rev, -MASKED if rev else None, None if rev else -MASKED)
                ek = _block_ref_rows(b[i], w, rev, None if rev else MASKED, MASKED if rev else None) - b[i]
            else:
                d = b[i] - _block_ref_rows(b[i], w, rev)
                eq = d + negq_ref[li]
                ek = negk_ref[li] - d
            qw = qb[i] * jnp.exp2(eq).astype(BF16)
            kw = kb[i] * jnp.exp2(ek).astype(BF16)
            p = lax.dot_general(qw, kw, _NT, preferred_element_type=F32) * lmask_ref[li]
            a[i] = p if a[i] is None else a[i] + p
        w //= 2
    for i in range(n):
        q, vb = chains[i][0], chains[i][2]
        o[i] = o[i] + jnp.dot(a[i].astype(BF16), vb, preferred_element_type=F32)
        o[i] = o[i] + jnp.sum(q * kk[i], axis=-1, keepdims=True) * vb.astype(F32)
    return o, states


def _hgrn_kernel(hq_ref, hf_ref, hb_ref, hi_ref, lbf_ref, lbb_ref, trif_ref, trir_ref,
                 negqf_ref, negkf_ref, negqr_ref, negkr_ref, lmask_ref,
                 o_ref, sf_ref, sb_ref, ob_ref, *, n_ctx_chunks, n_chunks):
    sf_ref[...] = jnp.zeros_like(sf_ref)
    sb_ref[...] = jnp.zeros_like(sb_ref)
    lbf = lbf_ref[0]
    lbb = lbb_ref[0]

    def steps(cf0, cb0, unroll):
        chains, rows = [], []
        for u in range(unroll):
            rf = pl.ds(pl.multiple_of((cf0 + u) * HG_CHUNK, HG_CHUNK), HG_CHUNK)
            rb = pl.ds(pl.multiple_of((cb0 - u) * HG_CHUNK, HG_CHUNK), HG_CHUNK)
            chains.append((hq_ref[0, rf, :], hf_ref[0, rf, :], hi_ref[0, rf, :], lbf,
                           trif_ref[...], negqf_ref, negkf_ref, False, 0))
            chains.append((hq_ref[0, rb, :], hb_ref[0, rb, :], hi_ref[0, rb, :], lbb,
                           trir_ref[...], negqr_ref, negkr_ref, True, 1))
            rows += [rf, rb]
        outs, (sf, sb) = _hgrn_chunks(chains, [sf_ref[...], sb_ref[...]], lmask_ref)
        for i, r in enumerate(rows):
            if i % 2 == 0:
                o_ref[0, r, :] = outs[i]
            else:
                ob_ref[r, :] = outs[i]
        sf_ref[...] = sf
        sb_ref[...] = sb

    n_lat = n_chunks - n_ctx_chunks
    u_ctx = math.gcd(HG_UNROLL, n_ctx_chunks)
    u_lat = math.gcd(HG_UNROLL, n_lat)

    def ctx_body(i, carry):
        steps(i * u_ctx, n_ctx_chunks - 1 - i * u_ctx, u_ctx)
        return carry

    def lat_body(i, carry):
        steps(n_ctx_chunks + i * u_lat, n_chunks - 1 - i * u_lat, u_lat)
        return carry

    lax.fori_loop(0, n_ctx_chunks // u_ctx, ctx_body, 0)
    lax.fori_loop(0, n_lat // u_lat, lat_body, 0)
    o_ref[0] = o_ref[0] + ob_ref[...]


def _hgrn_level_constants():
    c = HG_CHUNK
    t = np.arange(c)
    xor = t[:, None] ^ t[None, :]
    lmask, negq_f, negq_r = [], [], []
    w = c // 2
    while w >= 1:
        lmask.append(((xor >= w) & (xor < 2 * w)).astype(np.float32))
        upper = (t % (2 * w)) >= w
        negq_f.append(np.where(upper, 0.0, MASKED))
        negq_r.append(np.where(upper, MASKED, 0.0))
        w //= 2

    def rows(m):
        return jnp.asarray(np.broadcast_to(np.stack(m)[:, :, None], (len(m), c, HG_DIM)).astype(np.float32))

    tri_f = (t[None, :] <= t[:, None]).astype(np.float32)
    return (jnp.asarray(tri_f, BF16), jnp.asarray(tri_f.T, BF16),
            rows(negq_f), rows(negq_r), rows(negq_r), rows(negq_f), jnp.asarray(np.stack(lmask)))


def _hgrn(hq, hf, hb, hi, lb_f, lb_b, n_ctx):
    b, lt, _ = hq.shape
    kern = functools.partial(_hgrn_kernel, n_ctx_chunks=n_ctx // HG_CHUNK, n_chunks=lt // HG_CHUNK)
    seq = pl.BlockSpec((1, lt, HG_DIM), lambda bi, h: (bi, 0, h))
    lbs = pl.BlockSpec((1, 1, HG_DIM), lambda bi, h: (h, 0, 0))
    consts = _hgrn_level_constants()
    return pl.pallas_call(
        kern,
        grid=(b, HG_HEADS),
        in_specs=[seq, seq, seq, seq, lbs, lbs]
        + [pl.BlockSpec(a.shape, lambda bi, h, nd=a.ndim: (0,) * nd) for a in consts],
        out_specs=seq,
        out_shape=jax.ShapeDtypeStruct((b, lt, HG_WIDTH), F32),
        scratch_shapes=[pltpu.VMEM((HG_DIM, HG_DIM), F32), pltpu.VMEM((HG_DIM, HG_DIM), F32),
                        pltpu.VMEM((lt, HG_DIM), F32)],
        compiler_params=_cparams(("arbitrary", "arbitrary")),
        name="hgrn2",
    )(hq, hf, hb, hi, lb_f.reshape(HG_HEADS, 1, HG_DIM), lb_b.reshape(HG_HEADS, 1, HG_DIM), *consts)


def _split2(x):
    hi = x.astype(BF16)
    lo = (x - hi.astype(F32)).astype(BF16)
    return hi, lo


def _route(logits, bias):
    biased = jax.nn.sigmoid(logits) + bias
    rows = [biased[e:e + 1] for e in range(N_EXPERTS)]
    n = EXPERTS_PER_GROUP
    best = None
    g_sel = None
    for g in range(N_GROUPS):
        gs = None
        for i in range(n):
            for j2 in range(i + 1, n):
                pair = rows[g * n + i] + rows[g * n + j2]
                gs = pair if gs is None else jnp.maximum(gs, pair)
        if best is None:
            best, g_sel = gs, jnp.zeros(gs.shape, jnp.int32)
        else:
            take = gs > best
            best = jnp.where(take, gs, best)
            g_sel = jnp.where(take, g, g_sel)
    cand = []
    for i in range(n):
        c_i = rows[(N_GROUPS - 1) * n + i]
        for g in range(N_GROUPS - 2, -1, -1):
            c_i = jnp.where(g_sel == g, rows[g * n + i], c_i)
        cand.append(c_i)
    m1, i1 = cand[0], jnp.zeros(best.shape, jnp.int32)
    for i in range(1, n):
        take = cand[i] > m1
        m1 = jnp.where(take, cand[i], m1)
        i1 = jnp.where(take, i, i1)
    m2 = jnp.full(best.shape, -jnp.inf, F32)
    i2 = jnp.zeros(best.shape, jnp.int32)
    for i in range(n):
        take = (i1 != i) & (cand[i] > m2)
        m2 = jnp.where(take, cand[i], m2)
        i2 = jnp.where(take, i, i2)
    lo = jnp.minimum(i1, i2)
    hi = jnp.maximum(i1, i2)
    pair = jnp.where(lo == 0, hi - 1, jnp.where(lo == 1, hi + 1, N_PAIRS - 1))
    return g_sel * N_PAIRS + pair


def _merge_kernel(ona_ref, ohg_ref, og_ref, ga_ref, gh_ref, x_ref, mod_ref, hgg_ref, n2g_ref,
                  wna_ref, whg_ref, wout_ref, wr_ref, rb_ref,
                  xn_ref, h2_ref, bucket_ref, rank_ref, cnt_ref, carry_ref):
    first = (pl.program_id(0) == 0) & (pl.program_id(1) == 0)

    @pl.when(first)
    def _():
        carry_ref[...] = jnp.zeros_like(carry_ref)

    ohg = ohg_ref[0]
    og = og_ref[0]
    gain = hgg_ref[...]
    heads = []
    for h in range(HG_HEADS):
        sl = slice(h * HG_DIM, (h + 1) * HG_DIM)
        oh = ohg[:, sl]
        yh = oh * lax.rsqrt(jnp.mean(oh * oh, axis=-1, keepdims=True) + EPS) * gain
        gt = og[:, sl]
        heads.append((yh * (gt * jax.nn.sigmoid(gt))).astype(BF16))
    hn = jnp.concatenate(heads, axis=-1)
    y_na = jnp.dot(ona_ref[0].astype(BF16), wna_ref[...], preferred_element_type=F32)
    y_hg = jnp.dot(hn, whg_ref[...], preferred_element_type=F32)
    m = jax.nn.sigmoid(ga_ref[0]) * y_na + jax.nn.sigmoid(gh_ref[0]) * y_hg
    y = jnp.dot(m.astype(BF16), wout_ref[...], preferred_element_type=F32)
    xn = x_ref[0] + mod_ref[0, 2:3, :] * y
    xn_ref[0] = xn
    yn = xn * lax.rsqrt(jnp.mean(xn * xn, axis=-1, keepdims=True) + EPS) * n2g_ref[...]
    h2 = yn * (1.0 + mod_ref[0, 4:5, :]) + mod_ref[0, 3:4, :]
    h2_ref[...] = h2

    w_hi, w_lo = _split2(wr_ref[...])
    h_hi, h_lo = _split2(h2)
    logits = (lax.dot_general(w_hi, h_hi, _NT, preferred_element_type=F32)
              + lax.dot_general(w_hi, h_lo, _NT, preferred_element_type=F32)
              + lax.dot_general(w_lo, h_hi, _NT, preferred_element_type=F32))
    bucket = _route(logits, rb_ref[...])
    bucket_ref[0] = bucket

    t = bucket.shape[1]
    onehot = (lax.broadcasted_iota(jnp.int32, (BUCKET_ROWS, t), 0) == bucket).astype(F32)
    before = (lax.broadcasted_iota(jnp.int32, (t, t), 0)
              < lax.broadcasted_iota(jnp.int32, (t, t), 1)).astype(BF16)
    prefix = jnp.dot(onehot.astype(BF16), before, preferred_element_type=F32)
    carry = carry_ref[...]
    rank = jnp.sum(onehot * (prefix + carry[:, 0:1]), axis=0, keepdims=True)
    rank_ref[0] = rank.astype(jnp.int32)
    carry = carry + jnp.sum(onehot, axis=1, keepdims=True)
    carry_ref[...] = carry
    cnt_ref[...] = carry.astype(jnp.int32)


def _merge(o_na, o_hg, og, ga, gh, xc, mods, hg_gain, n2_gain, w_na_o, w_hg_o, w_out, w_router_t,
           router_bias, ctx_row, skip):
    b, lt, d = xc.shape
    n_t = lt // ROW_TILE - skip

    def rows(width):
        return pl.BlockSpec((1, ROW_TILE, width), lambda bi, j: (bi, j + skip, 0))

    def full(a):
        return pl.BlockSpec(a.shape, lambda bi, j: (0,) * a.ndim)

    def mod_map(bi, j):
        return (jnp.where(j + skip == 0, ctx_row, bi), 0, 0)

    tok = pl.BlockSpec((1, 1, ROW_TILE), lambda bi, j: (bi * n_t + j, 0, 0))
    consts = [hg_gain, n2_gain, w_na_o, w_hg_o, w_out, w_router_t, router_bias]
    return pl.pallas_call(
        _merge_kernel,
        grid=(b, n_t),
        in_specs=[pl.BlockSpec((1, ROW_TILE, NA_WIDTH), lambda bi, j: (bi, j, 0)),
                  rows(HG_WIDTH), rows(HG_WIDTH), rows(d), rows(d), rows(d),
                  pl.BlockSpec((1, 6, d), mod_map)] + [full(a) for a in consts],
        out_specs=[pl.BlockSpec((1, ROW_TILE, d), lambda bi, j: (bi, j, 0)),
                   pl.BlockSpec((ROW_TILE, d), lambda bi, j: (bi * n_t + j, 0)),
                   tok, tok,
                   pl.BlockSpec((BUCKET_ROWS, LANES), lambda bi, j: (0, 0))],
        out_shape=[jax.ShapeDtypeStruct((b, n_t * ROW_TILE, d), F32),
                   jax.ShapeDtypeStruct((b * n_t * ROW_TILE, d), F32),
                   jax.ShapeDtypeStruct((b * n_t, 1, ROW_TILE), jnp.int32),
                   jax.ShapeDtypeStruct((b * n_t, 1, ROW_TILE), jnp.int32),
                   jax.ShapeDtypeStruct((BUCKET_ROWS, LANES), jnp.int32)],
        scratch_shapes=[pltpu.VMEM((BUCKET_ROWS, LANES), F32)],
        compiler_params=_cparams(("arbitrary", "arbitrary")),
        name="merge_router",
    )(o_na, o_hg, og, ga, gh, xc, mods, *consts)


def _start_row_copies(n, row_copy):
    for r in range(n):
        row_copy(r).start()


def _scatter_kernel(dest_ref, h_ref, xs_in_ref, xs_ref, sem):
    del xs_in_ref
    n = h_ref.shape[0]
    _start_row_copies(n, lambda r: pltpu.make_async_copy(
        h_ref.at[pl.ds(r, 1)], xs_ref.at[pl.ds(dest_ref[0, 0, r], 1)], sem))
    pltpu.make_async_copy(h_ref, xs_ref.at[pl.ds(0, n)], sem).wait()


def _scatter_rows(h2, dest, n_sorted):
    t, d = h2.shape
    tile = 2 * ROW_TILE if t % (2 * ROW_TILE) == 0 else ROW_TILE
    n_t = t // tile
    return pl.pallas_call(
        _scatter_kernel,
        grid=(n_t,),
        in_specs=[pl.BlockSpec((1, 1, tile), lambda i: (i, 0, 0), memory_space=pltpu.SMEM),
                  pl.BlockSpec((tile, d), lambda i: (i, 0)),
                  pl.BlockSpec(memory_space=pl.ANY)],
        out_specs=pl.BlockSpec(memory_space=pl.ANY),
        out_shape=jax.ShapeDtypeStruct((n_sorted, d), F32),
        scratch_shapes=[pltpu.SemaphoreType.DMA(())],
        input_output_aliases={2: 0},
        compiler_params=_cparams(("arbitrary",)),
        name="scatter_rows",
    )(dest.reshape(n_t, 1, tile), h2, jnp.zeros((n_sorted, d), F32))


def _final_kernel(dest_ref, dest_next_ref, xn_ref, mod_ref, g_ref, ys_ref, o_ref, buf_ref, sems,
                  *, final_norm):
    n = buf_ref.shape[1]
    step = pl.program_id(0) * pl.num_programs(1) + pl.program_id(1)
    n_steps = pl.num_programs(0) * pl.num_programs(1)

    def gather(d_ref, slot):
        _start_row_copies(n, lambda r: pltpu.make_async_copy(
            ys_ref.at[pl.ds(d_ref[0, 0, r], 1)], buf_ref.at[slot, pl.ds(r, 1)], sems.at[slot]))

    @pl.when(step == 0)
    def _():
        gather(dest_ref, 0)

    @pl.when(step + 1 < n_steps)
    def _():
        gather(dest_next_ref, (step + 1) % 2)

    slot = step % 2
    pltpu.make_async_copy(ys_ref.at[pl.ds(0, n)], buf_ref.at[slot], sems.at[slot]).wait()
    x = xn_ref[0] + mod_ref[0, 5:6, :] * buf_ref[slot]
    if final_norm:
        x = x * lax.rsqrt(jnp.mean(x * x, axis=-1, keepdims=True) + EPS) * g_ref[...]
    o_ref[0] = x


def _gather_residual(xn, mods, ys, dest, gain, ctx_row, final_norm, skip):
    b, lt, d = xn.shape
    n_t = lt // ROW_TILE

    def mod_map(bi, j):
        return (jnp.where(j + skip == 0, ctx_row, bi), 0, 0)

    kern = functools.partial(_final_kernel, final_norm=final_norm)
    last_tile = b * n_t - 1
    dest3 = dest.reshape(b * n_t, 1, ROW_TILE)
    return pl.pallas_call(
        kern,
        grid=(b, n_t),
        in_specs=[pl.BlockSpec((1, 1, ROW_TILE), lambda bi, j: (bi * n_t + j, 0, 0),
                               memory_space=pltpu.SMEM),
                  pl.BlockSpec((1, 1, ROW_TILE), lambda bi, j: (jnp.minimum(bi * n_t + j + 1, last_tile), 0, 0),
                               memory_space=pltpu.SMEM),
                  pl.BlockSpec((1, ROW_TILE, d), lambda bi, j: (bi, j, 0)),
                  pl.BlockSpec((1, 6, d), mod_map),
                  pl.BlockSpec((1, d), lambda bi, j: (0, 0)),
                  pl.BlockSpec(memory_space=pl.ANY)],
        out_specs=pl.BlockSpec((1, ROW_TILE, d), lambda bi, j: (bi, j, 0)),
        out_shape=jax.ShapeDtypeStruct((b, lt, d), F32),
        scratch_shapes=[pltpu.VMEM((2, ROW_TILE, d), F32), pltpu.SemaphoreType.DMA((2,))],
        compiler_params=_cparams(("arbitrary", "arbitrary")),
        name="gather_residual",
    )(dest3, dest3, xn, mods, gain, ys)


def _moe_kernel(ea_ref, eb_ref, nused_ref, xs_ref, wra_ref, wrb_ref,
                wga_ref, wua_ref, wda_ref, wgb_ref, wub_ref, wdb_ref, ys_ref):
    del ea_ref, eb_ref
    i = pl.program_id(0)

    @pl.when(i < nused_ref[0])
    def _():
        x = xs_ref[...]
        s_a = jax.nn.sigmoid(jnp.sum(x * wra_ref[0], axis=-1, keepdims=True))
        s_b = jax.nn.sigmoid(jnp.sum(x * wrb_ref[0], axis=-1, keepdims=True))
        tot = s_a + s_b
        xb = x.astype(BF16)

        def expert(wg, wu, wd):
            gate = jnp.dot(xb, wg[0], preferred_element_type=F32)
            up = jnp.dot(xb, wu[0], preferred_element_type=F32)
            hid = (gate * jax.nn.sigmoid(gate) * up).astype(BF16)
            return jnp.dot(hid, wd[0], preferred_element_type=F32)

        ys_ref[...] = ((s_a / tot) * expert(wga_ref, wua_ref, wda_ref)
                       + (s_b / tot) * expert(wgb_ref, wub_ref, wdb_ref))

    @pl.when(i >= nused_ref[0])
    def _():
        ys_ref[...] = jnp.zeros_like(ys_ref)


def _moe(xs, tile_ea, tile_eb, n_used, w_router_rows, w_gate, w_up, w_down):
    n_sorted, d = xs.shape
    n_tiles = n_sorted // ROW_TILE
    ff = w_gate.shape[-1]

    def by_a(i, ea, eb, nu):
        return (ea[i], 0, 0)

    def by_b(i, ea, eb, nu):
        return (eb[i], 0, 0)

    grid_spec = pltpu.PrefetchScalarGridSpec(
        num_scalar_prefetch=3,
        grid=(n_tiles,),
        in_specs=[pl.BlockSpec((ROW_TILE, d), lambda i, ea, eb, nu: (i, 0)),
                  pl.BlockSpec((1, 1, d), by_a), pl.BlockSpec((1, 1, d), by_b),
                  pl.BlockSpec((1, d, ff), by_a), pl.BlockSpec((1, d, ff), by_a),
                  pl.BlockSpec((1, ff, d), by_a),
                  pl.BlockSpec((1, d, ff), by_b), pl.BlockSpec((1, d, ff), by_b),
                  pl.BlockSpec((1, ff, d), by_b)],
        out_specs=pl.BlockSpec((ROW_TILE, d), lambda i, ea, eb, nu: (i, 0)),
    )
    return pl.pallas_call(
        _moe_kernel,
        grid_spec=grid_spec,
        out_shape=jax.ShapeDtypeStruct((n_sorted, d), F32),
        compiler_params=_cparams(("arbitrary",)),
        name="moe_pairs",
    )(tile_ea, tile_eb, n_used, xs, w_router_rows, w_router_rows,
      w_gate, w_up, w_down, w_gate, w_up, w_down)


_PAIR_LO = np.array([0, 0, 0, 1, 1, 2], np.int32)
_PAIR_HI = np.array([1, 2, 3, 2, 3, 3], np.int32)


def _sorted_layout(bucket, rank, counts, n_tiles):
    counts = counts[:N_BUCKETS]
    padded = ((counts + ROW_TILE - 1) // ROW_TILE) * ROW_TILE
    ends = jnp.cumsum(padded)
    starts = ends - padded
    dest = starts[bucket] + rank
    tile_start = jnp.arange(n_tiles, dtype=jnp.int32) * ROW_TILE
    tile_bucket = jnp.sum((ends[None, :] <= tile_start[:, None]).astype(jnp.int32), axis=1)
    tile_bucket = jnp.minimum(tile_bucket, N_BUCKETS - 1)
    group = tile_bucket // N_PAIRS
    pair = tile_bucket % N_PAIRS
    ea = group * EXPERTS_PER_GROUP + jnp.asarray(_PAIR_LO)[pair]
    eb = group * EXPERTS_PER_GROUP + jnp.asarray(_PAIR_HI)[pair]
    n_used = (ends[-1] // ROW_TILE).astype(jnp.int32).reshape(1)
    return dest.astype(jnp.int32), ea.astype(jnp.int32), eb.astype(jnp.int32), n_used


def _lower_bounds(raw):
    p = jax.nn.softmax(raw.astype(F32), axis=0)
    return jnp.cumsum(p, axis=0) - p[0:1]


def kernel(x, c, ctx, c_ctx, w_ada, b_ada, norm1_g, w_in, na_rel_bias, hg_lower_fwd, hg_lower_bwd,
           hg_norm_g, w_na_o, w_hg_o, w_out, norm2_g, w_router, router_bias, w_gate, w_up, w_down,
           final_g):
    b, seq, d = x.shape
    n_ctx = ctx.shape[1]
    depth = w_ada.shape[0]
    lt = n_ctx + seq
    assert n_ctx % ROW_TILE == 0 and seq % ROW_TILE == 0 and seq % GRID_W == 0
    assert n_ctx % HG_CHUNK == 0 and seq % HG_CHUNK == 0
    assert seq // GRID_W >= NA_WIN_ROWS + NA_Q_ROWS - 1 and (seq // GRID_W) % NA_Q_ROWS == 0
    assert n_ctx % (NA_Q_ROWS * GRID_W) == 0

    ada_rows = -(-(b + 1) // 8) * 8
    ctx_row = b
    cc = jnp.concatenate([c, c_ctx[None, :], jnp.zeros((ada_rows - b - 1, d), F32)], axis=0)
    mods = _ada(cc, w_ada, b_ada).reshape(depth, ada_rows, 6, d)

    lb_f = _lower_bounds(hg_lower_fwd)
    lb_b = _lower_bounds(hg_lower_bwd)
    cos_t, sin_t = _rope_tables(n_ctx, seq)
    w_router_t = jnp.transpose(w_router)
    w_router_rows = w_router_t.reshape(N_EXPERTS, 1, d)
    rb = router_bias.astype(F32).reshape(N_EXPERTS, 1)

    xc = jnp.concatenate([ctx, x], axis=1)
    out = None
    for l in range(depth):
        last = l == depth - 1
        skip = n_ctx // ROW_TILE if last else 0
        n_tiles = b * (lt // ROW_TILE - skip) + N_BUCKETS
        q, k, v, hq, hf, hb, hi, og, ga, gh = _inproj(
            xc, mods[l], norm1_g[l].reshape(1, d), w_in[l].astype(BF16), cos_t, sin_t, ctx_row)
        o_na = _na_attention(q, k, v, na_rel_bias[l], n_ctx, with_ctx=not last)
        o_hg = _hgrn(hq, hf, hb, hi, lb_f[l], lb_b[l], n_ctx)
        xn, h2, bucket, rank, counts = _merge(
            o_na, o_hg, og, ga, gh, xc, mods[l], hg_norm_g[l].reshape(1, HG_DIM),
            norm2_g[l].reshape(1, d), w_na_o[l].astype(BF16), w_hg_o[l].astype(BF16),
            w_out[l].astype(BF16), w_router_t, rb, ctx_row, skip)
        dest, tile_ea, tile_eb, n_used = _sorted_layout(
            bucket.reshape(-1), rank.reshape(-1), counts[:, 0], n_tiles)
        xs = _scatter_rows(h2, dest, n_tiles * ROW_TILE)
        ys = _moe(xs, tile_ea, tile_eb, n_used, w_router_rows,
                  w_gate[l].astype(BF16), w_up[l].astype(BF16), w_down[l].astype(BF16))
        res = _gather_residual(xn, mods[l], ys, dest, final_g.reshape(1, d), ctx_row, last, skip)
        if last:
            out = res
        else:
            xc = res
    return out
```

```python
import functools
import math

import jax
import jax.numpy as jnp
import numpy as np
from jax import lax
from jax.experimental import pallas as pl
from jax.experimental.pallas import tpu as pltpu

F32 = jnp.float32
BF16 = jnp.bfloat16

GRID_W = 64
EPS = 1e-6
NA_HEADS = 8
NA_HEAD_DIM = 64
NA_WIDTH = NA_HEADS * NA_HEAD_DIM
NA_WIN_ROWS = 8
NA_WIN_COLS = 16
ROPE_THETA = 10000.0
HG_HEADS = 4
HG_DIM = 128
HG_WIDTH = HG_HEADS * HG_DIM
GATE_FLOOR = 1e-30
N_EXPERTS = 16
N_GROUPS = 4
EXPERTS_PER_GROUP = 4
N_PAIRS = 6
N_BUCKETS = N_GROUPS * N_PAIRS
BUCKET_ROWS = 32
MASKED = -1e30
LOG2E = 1.4426950408889634

LANES = 128
ROW_TILE = 256
NA_Q_ROWS = 2
SUBLANES = 8
HG_CHUNK = 128
HG_UNROLL = 4
VMEM_LIMIT = 56 * 1024 * 1024

_NT = (((1,), (1,)), ((), ()))
_TN = (((0,), (0,)), ((), ()))


def _cparams(sem):
    return pltpu.CompilerParams(dimension_semantics=sem, vmem_limit_bytes=VMEM_LIMIT)


def _ada_kernel(c_ref, w_ref, b_ref, o_ref):
    cc = c_ref[...]
    s = cc * jax.nn.sigmoid(cc)
    o_ref[0] = jnp.dot(s, w_ref[0], preferred_element_type=F32) + b_ref[0]


def _ada(cc, w_ada, b_ada):
    depth, d, n = w_ada.shape
    rows = cc.shape[0]
    tn = 1536
    return pl.pallas_call(
        _ada_kernel,
        grid=(depth, n // tn),
        in_specs=[pl.BlockSpec((rows, d), lambda l, j: (0, 0)),
                  pl.BlockSpec((1, d, tn), lambda l, j: (l, 0, j)),
                  pl.BlockSpec((1, 1, tn), lambda l, j: (l, 0, j))],
        out_specs=pl.BlockSpec((1, rows, tn), lambda l, j: (l, 0, j)),
        out_shape=jax.ShapeDtypeStruct((depth, rows, n), F32),
        compiler_params=_cparams(("arbitrary", "arbitrary")),
        name="ada",
    )(cc, w_ada, b_ada.reshape(depth, 1, n))


def _inproj_kernel(x0_ref, x_ref, mod_ref, g_ref, w_ref, cos_ref, sin_ref,
                   q_ref, k_ref, v_ref, hq_ref, hf_ref, hb_ref, hi_ref, og_ref, ga_ref, gh_ref):
    x = jnp.where(pl.program_id(1) == 0, x0_ref[0], x_ref[0])
    y = x * lax.rsqrt(jnp.mean(x * x, axis=-1, keepdims=True) + EPS) * g_ref[...]
    h = (y * (1.0 + mod_ref[0, 1:2, :]) + mod_ref[0, 0:1, :]).astype(BF16)

    def proj(off, width):
        return jnp.dot(h, w_ref[0, :, off:off + width], preferred_element_type=F32)

    cos = cos_ref[...]
    sin = sin_ref[...]
    even = (lax.broadcasted_iota(jnp.int32, cos.shape, 1) % 2) == 0

    def rope(z):
        swapped = jnp.where(even, pltpu.roll(z, LANES - 1, 1), pltpu.roll(z, 1, 1))
        return z * cos + swapped * sin

    scale = NA_HEAD_DIM ** -0.5 * LOG2E
    for blk in range(NA_WIDTH // LANES):
        sl = slice(blk * LANES, (blk + 1) * LANES)
        q_ref[0, :, sl] = (rope(proj(blk * LANES, LANES)) * scale).astype(q_ref.dtype)
        k_ref[0, :, sl] = rope(proj(NA_WIDTH + blk * LANES, LANES)).astype(k_ref.dtype)
    off = 2 * NA_WIDTH
    v_ref[0] = proj(off, NA_WIDTH).astype(v_ref.dtype)
    off += NA_WIDTH
    hq = proj(off, HG_WIDTH)
    hq_ref[0] = hq * jax.nn.sigmoid(hq)
    off += HG_WIDTH
    hf_ref[0] = proj(off, HG_WIDTH)
    off += HG_WIDTH
    hb_ref[0] = proj(off, HG_WIDTH)
    off += HG_WIDTH
    hi_ref[0] = proj(off, HG_WIDTH).astype(hi_ref.dtype)
    off += HG_WIDTH
    og_ref[0] = proj(off, HG_WIDTH)
    off += HG_WIDTH
    d = x.shape[-1]
    for half in range(2):
        sl = slice(half * (d // 2), (half + 1) * (d // 2))
        ga_ref[0, :, sl] = proj(off + half * (d // 2), d // 2)
        gh_ref[0, :, sl] = proj(off + d + half * (d // 2), d // 2)


def _stream_specs(first, rest, n_first):
    d = first.shape[-1]
    return (pl.BlockSpec((1, ROW_TILE, d), lambda bi, j: (bi, 0, 0)),
            pl.BlockSpec((1, ROW_TILE, d), lambda bi, j: (bi, jnp.maximum(j - n_first, 0), 0)))


def _inproj(stream, lt, mods, g, w_all, layer, cos_t, sin_t, ctx_row):
    first, rest, n_first = stream
    b, _, d = first.shape
    n_t = lt // ROW_TILE

    def rows(width):
        return pl.BlockSpec((1, ROW_TILE, width), lambda bi, j: (bi, j, 0))

    def mod_map(bi, j):
        return (jnp.where(j == 0, ctx_row, bi), 0, 0)

    widths = [NA_WIDTH] * 3 + [HG_WIDTH] * 5 + [d, d]
    dtypes = [BF16, BF16, BF16, F32, F32, F32, BF16, F32, F32, F32]
    return pl.pallas_call(
        _inproj_kernel,
        grid=(b, n_t),
        in_specs=[*_stream_specs(first, rest, n_first),
                  pl.BlockSpec((1, 6, d), mod_map),
                  pl.BlockSpec((1, d), lambda bi, j: (0, 0)),
                  pl.BlockSpec((1,) + w_all.shape[1:], lambda bi, j: (layer, 0, 0)),
                  pl.BlockSpec((ROW_TILE, LANES), lambda bi, j: (j, 0)),
                  pl.BlockSpec((ROW_TILE, LANES), lambda bi, j: (j, 0))],
        out_specs=[rows(w) for w in widths],
        out_shape=[jax.ShapeDtypeStruct((b, lt, w), dt) for w, dt in zip(widths, dtypes)],
        compiler_params=_cparams(("arbitrary", "arbitrary")),
        name="inproj",
    )(first, rest, mods, g, w_all, cos_t, sin_t)


def _rope_tables(n_ctx, seq):
    t = jnp.arange(seq, dtype=jnp.int32)
    row = (t // GRID_W).astype(F32)
    col = (t % GRID_W).astype(F32)
    rot_half = NA_HEAD_DIM // 2
    inv = ROPE_THETA ** (-jnp.arange(0, rot_half, 2, dtype=F32) / rot_half)
    ang = jnp.concatenate([row[:, None] * inv, col[:, None] * inv], axis=-1)
    cos = jnp.repeat(jnp.cos(ang), 2, axis=-1)
    sin = jnp.repeat(jnp.sin(ang), 2, axis=-1)
    sign = jnp.asarray(np.tile(np.array([-1.0, 1.0], np.float32), NA_HEAD_DIM // 2))
    sin = sin * sign
    cos = jnp.concatenate([jnp.ones((n_ctx, NA_HEAD_DIM), F32), cos], axis=0)
    sin = jnp.concatenate([jnp.zeros((n_ctx, NA_HEAD_DIM), F32), sin], axis=0)
    reps = LANES // NA_HEAD_DIM
    return jnp.tile(cos, (1, reps)), jnp.tile(sin, (1, reps))


def _na_plan(n_ctx, grid_rows):
    span = NA_WIN_ROWS + NA_Q_ROWS - 1
    variants, var_of_step, ws_of_step = [], [], []
    for r0 in range(0, grid_rows, NA_Q_ROWS):
        rs = [min(max(r0 + dq - NA_WIN_ROWS // 2, 0), grid_rows - NA_WIN_ROWS) for dq in range(NA_Q_ROWS)]
        ws = min(rs[0], grid_rows - span)
        key = (r0 - ws,) + tuple(r - ws for r in rs)
        if key not in variants:
            variants.append(key)
        var_of_step.append(variants.index(key))
        ws_of_step.append(ws)
    ctx_steps = n_ctx // (NA_Q_ROWS * GRID_W)
    var_of_step = [len(variants)] * ctx_steps + var_of_step
    ws_of_step = [0] * ctx_steps + ws_of_step
    return variants, np.asarray(var_of_step, np.int32), np.asarray(ws_of_step, np.int32)


def _na_bias_planes(rel_bias, variants):
    span = NA_WIN_ROWS + NA_Q_ROWS - 1
    n_dr, n_dc = 2 * NA_WIN_ROWS - 1, 2 * NA_WIN_COLS - 1
    col = np.arange(GRID_W)[:, None]
    kc = np.arange(GRID_W)[None, :]
    cs = np.clip(col - NA_WIN_COLS // 2, 0, GRID_W - NA_WIN_COLS)
    col_ok = (kc >= cs) & (kc < cs + NA_WIN_COLS)
    oh_c = ((kc - col + NA_WIN_COLS - 1)[..., None] == np.arange(n_dc)) & col_ok[..., None]
    planes = jnp.einsum("hrc,xkc->hrxk", rel_bias.astype(F32), jnp.asarray(oh_c.astype(np.float32)),
                        precision=lax.Precision.HIGHEST)
    planes = jnp.where(jnp.asarray(col_ok), planes * LOG2E, MASKED)
    planes = jnp.concatenate([planes, jnp.full_like(planes[:, :1], MASKED)], axis=1)
    plane_of = np.full((len(variants) + 1, NA_Q_ROWS, span), n_dr, np.int32)
    for vi, key in enumerate(variants):
        for dq in range(NA_Q_ROWS):
            for j in range(key[1 + dq], key[1 + dq] + NA_WIN_ROWS):
                plane_of[vi, dq, j] = j - (key[0] + dq) + NA_WIN_ROWS - 1
    return planes, plane_of.reshape(-1)


def _na_kernel(var_ref, ws_ref, plane_ref, q_ref, k_ref, v_ref, b_ref, o_ref, sw_ref, sc_ref, t_ref,
               *, step_off, n_ctx):
    nq = q_ref.shape[1]
    win = sw_ref.shape[1]
    span = win // GRID_W
    j = pl.program_id(1) + step_off
    var = var_ref[j]

    @pl.when((pl.program_id(1) == 0) | (var != var_ref[jnp.maximum(j - 1, 0)]))
    def _():
        for h in range(NA_HEADS):
            for dq in range(nq // GRID_W):
                base = (var * (nq // GRID_W) + dq) * span
                row = jnp.concatenate([b_ref[h, plane_ref[base + jj]] for jj in range(span)], axis=1)
                t_ref[h, dq * GRID_W:(dq + 1) * GRID_W, :] = row

    start = pl.multiple_of(n_ctx + ws_ref[j] * GRID_W, GRID_W)
    low = lax.broadcasted_iota(jnp.int32, (nq, LANES), 1) < NA_HEAD_DIM
    for hp in range(NA_HEADS // 2):
        sl = slice(hp * LANES, (hp + 1) * LANES)
        q2 = q_ref[0, :, sl]
        kw = k_ref[0, pl.ds(start, win), sl]
        kc = k_ref[0, 0:n_ctx, sl]
        zero = jnp.zeros_like(q2)
        qs = jnp.concatenate([jnp.where(low, q2, zero), jnp.where(low, zero, q2)], axis=0)
        rows = slice(2 * hp * nq, (2 * hp + 2) * nq)
        bias = t_ref[2 * hp:2 * hp + 2].reshape(2 * nq, win)
        sw_ref[rows, :] = lax.dot_general(qs, kw, _NT, preferred_element_type=F32) + bias
        sc_ref[rows, :] = lax.dot_general(qs, kc, _NT, preferred_element_type=F32)
    sw = sw_ref[...]
    sc = sc_ref[...]
    m = jnp.maximum(jnp.max(sw, axis=-1, keepdims=True), jnp.max(sc, axis=-1, keepdims=True))
    pw = jnp.exp2(sw - m)
    pc = jnp.exp2(sc - m)
    inv = 1.0 / (jnp.sum(pw, axis=-1, keepdims=True) + jnp.sum(pc, axis=-1, keepdims=True))
    pw = pw.astype(BF16)
    pc = pc.astype(BF16)
    for hp in range(NA_HEADS // 2):
        sl = slice(hp * LANES, (hp + 1) * LANES)
        vw = v_ref[0, pl.ds(start, win), sl]
        vc = v_ref[0, 0:n_ctx, sl]
        rows = slice(2 * hp * nq, (2 * hp + 2) * nq)
        o2 = (jnp.dot(pw[rows], vw, preferred_element_type=F32)
              + jnp.dot(pc[rows], vc, preferred_element_type=F32)) * inv[rows]
        o_ref[0, :, sl] = jnp.where(low, o2[:nq], o2[nq:])


def _na_attention(q, k, v, rel_bias, n_ctx, with_ctx):
    b, lt, _ = q.shape
    nq = NA_Q_ROWS * GRID_W
    span = NA_WIN_ROWS + NA_Q_ROWS - 1
    variants, var_of_step, ws_of_step = _na_plan(n_ctx, (lt - n_ctx) // GRID_W)
    planes, plane_of = _na_bias_planes(rel_bias, variants)
    step_off = 0 if with_ctx else n_ctx // nq
    n_steps = lt // nq - step_off

    kern = functools.partial(_na_kernel, step_off=step_off, n_ctx=n_ctx)
    grid_spec = pltpu.PrefetchScalarGridSpec(
        num_scalar_prefetch=3,
        grid=(b, n_steps),
        in_specs=[pl.BlockSpec((1, nq, NA_WIDTH), lambda bi, j, *_: (bi, j + step_off, 0)),
                  pl.BlockSpec((1, lt, NA_WIDTH), lambda bi, j, *_: (bi, 0, 0)),
                  pl.BlockSpec((1, lt, NA_WIDTH), lambda bi, j, *_: (bi, 0, 0)),
                  pl.BlockSpec(planes.shape, lambda bi, j, *_: (0, 0, 0, 0))],
        out_specs=pl.BlockSpec((1, nq, NA_WIDTH), lambda bi, j, *_: (bi, j, 0)),
        scratch_shapes=[pltpu.VMEM((NA_HEADS * nq, span * GRID_W), F32),
                        pltpu.VMEM((NA_HEADS * nq, n_ctx), F32),
                        pltpu.VMEM((NA_HEADS, nq, span * GRID_W), F32)],
    )
    return pl.pallas_call(
        kern,
        grid_spec=grid_spec,
        out_shape=jax.ShapeDtypeStruct((b, n_steps * nq, NA_WIDTH), F32),
        compiler_params=_cparams(("arbitrary", "arbitrary")),
        name="na_attention",
    )(jnp.asarray(var_of_step), jnp.asarray(ws_of_step), jnp.asarray(plane_of), q, k, v, planes)


def _split3(x):
    hi = x.astype(BF16)
    r1 = x - hi.astype(F32)
    mid = r1.astype(BF16)
    lo = (r1 - mid.astype(F32)).astype(BF16)
    return hi, mid, lo


def _block_ref_rows(b, w, rev, upper_fill=None, lower_fill=None):
    c, n = b.shape
    off = w if rev else w - 1
    if upper_fill is not None or lower_fill is not None:
        parts = []
        for s in range(0, c, 2 * w):
            ref = jnp.broadcast_to(b[s + off:s + off + 1], (w, n))
            parts.append(ref if lower_fill is None else jnp.full((w, n), lower_fill, b.dtype))
            parts.append(ref if upper_fill is None else jnp.full((w, n), upper_fill, b.dtype))
        return jnp.concatenate(parts, axis=0)
    if 2 * w >= SUBLANES:
        parts = [jnp.broadcast_to(b[s + off:s + off + 1], (2 * w, n)) for s in range(0, c, 2 * w)]
        return parts[0] if len(parts) == 1 else jnp.concatenate(parts, axis=0)
    b3 = b.reshape(c // SUBLANES, SUBLANES, n)
    sub = lax.broadcasted_iota(jnp.int32, b3.shape, 1)
    r = None
    for s in range(0, SUBLANES, 2 * w):
        piece = jnp.broadcast_to(b3[:, s + off:s + off + 1, :], b3.shape)
        r = piece if r is None else jnp.where(sub >= s, piece, r)
    return r.reshape(c, n)


def _hgrn_chunks(chains, states, lmask_ref):
    n = len(chains)
    c = chains[0][0].shape[0]
    lg, kk = [], []
    for q, f, vb, lb, tri, negq_ref, negk_ref, rev, sid in chains:
        one_m_lb = 1.0 - lb
        g = lb + one_m_lb * jax.nn.sigmoid(f)
        lg.append(jnp.log(jnp.maximum(g, GATE_FLOOR)) * LOG2E)
        kk.append(one_m_lb * jax.nn.sigmoid(-f))
    pieces = [_split3(x) for x in lg]
    b = []
    for i in range(n):
        tri = chains[i][4]
        hi, mid, lo = pieces[i]
        b.append(jnp.dot(tri, hi, preferred_element_type=F32) + jnp.dot(tri, mid, preferred_element_type=F32)
                 + jnp.dot(tri, lo, preferred_element_type=F32))

    qb = [ch[0].astype(BF16) for ch in chains]
    kb = [x.astype(BF16) for x in kk]

    states = list(states)
    o = []
    for i in range(n):
        vb, rev, sid = chains[i][2], chains[i][7], chains[i][8]
        b_last = b[i][0:1] if rev else b[i][c - 1:c]
        qe = qb[i] * jnp.exp2(b[i]).astype(BF16)
        ke = kb[i] * jnp.exp2(b_last - b[i]).astype(BF16)
        st = states[sid]
        o.append(lax.dot_general(qe, st.astype(BF16), _NT, preferred_element_type=F32))
        states[sid] = st * jnp.exp2(b_last) + lax.dot_general(vb, ke, _TN, preferred_element_type=F32)

    a = [None] * n
    w = c // 2
    for li in range(lmask_ref.shape[0]):
        for i in range(n):
            negq_ref, negk_ref, rev = chains[i][5], chains[i][6], chains[i][7]
            if w >= SUBLANES:
                eq = b[i] - _block_ref_rows(b[i], w, rev, -MASKED if rev else None, None if rev else -MASKED)
                ek = _block_ref_rows(b[i], w, rev, None if rev else MASKED, MASKED if rev else None) - b[i]
            else:
                d = b[i] - _block_ref_rows(b[i], w, rev)
                eq = d + negq_ref[li]
                ek = negk_ref[li] - d
            qw = qb[i] * jnp.exp2(eq).astype(BF16)
            kw = kb[i] * jnp.exp2(ek).astype(BF16)
            p = lax.dot_general(qw, kw, _NT, preferred_element_type=F32) * lmask_ref[li]
            a[i] = p if a[i] is None else a[i] + p
        w //= 2
    for i in range(n):
        q, vb = chains[i][0], chains[i][2]
        o[i] = o[i] + jnp.dot(a[i].astype(BF16), vb, preferred_element_type=F32)
        o[i] = o[i] + jnp.sum(q * kk[i], axis=-1, keepdims=True) * vb.astype(F32)
    return o, states


def _hgrn_kernel(hq_ref, hf_ref, hb_ref, hi_ref, lbf_ref, lbb_ref, trif_ref, trir_ref,
                 negqf_ref, negkf_ref, negqr_ref, negkr_ref, lmask_ref,
                 o_ref, sf_ref, sb_ref, ob_ref, *, n_ctx_chunks, n_chunks):
    sf_ref[...] = jnp.zeros_like(sf_ref)
    sb_ref[...] = jnp.zeros_like(sb_ref)
    lbf = lbf_ref[0]
    lbb = lbb_ref[0]

    def steps(cf0, cb0, unroll):
        chains, rows = [], []
        for u in range(unroll):
            rf = pl.ds(pl.multiple_of((cf0 + u) * HG_CHUNK, HG_CHUNK), HG_CHUNK)
            rb = pl.ds(pl.multiple_of((cb0 - u) * HG_CHUNK, HG_CHUNK), HG_CHUNK)
            chains.append((hq_ref[0, rf, :], hf_ref[0, rf, :], hi_ref[0, rf, :], lbf,
                           trif_ref[...], negqf_ref, negkf_ref, False, 0))
            chains.append((hq_ref[0, rb, :], hb_ref[0, rb, :], hi_ref[0, rb, :], lbb,
                           trir_ref[...], negqr_ref, negkr_ref, True, 1))
            rows += [rf, rb]
        outs, (sf, sb) = _hgrn_chunks(chains, [sf_ref[...], sb_ref[...]], lmask_ref)
        for i, r in enumerate(rows):
            if i % 2 == 0:
                o_ref[0, r, :] = outs[i]
            else:
                ob_ref[r, :] = outs[i]
        sf_ref[...] = sf
        sb_ref[...] = sb

    n_lat = n_chunks - n_ctx_chunks
    u_ctx = math.gcd(HG_UNROLL, n_ctx_chunks)
    u_lat = math.gcd(HG_UNROLL, n_lat)

    def ctx_body(i, carry):
        steps(i * u_ctx, n_ctx_chunks - 1 - i * u_ctx, u_ctx)
        return carry

    def lat_body(i, carry):
        steps(n_ctx_chunks + i * u_lat, n_chunks - 1 - i * u_lat, u_lat)
        return carry

    lax.fori_loop(0, n_ctx_chunks // u_ctx, ctx_body, 0)
    lax.fori_loop(0, n_lat // u_lat, lat_body, 0)
    o_ref[0] = o_ref[0] + ob_ref[...]


def _hgrn_level_constants():
    c = HG_CHUNK
    t = np.arange(c)
    xor = t[:, None] ^ t[None, :]
    lmask, negq_f, negq_r = [], [], []
    w = c // 2
    while w >= 1:
        lmask.append(((xor >= w) & (xor < 2 * w)).astype(np.float32))
        upper = (t % (2 * w)) >= w
        negq_f.append(np.where(upper, 0.0, MASKED))
        negq_r.append(np.where(upper, MASKED, 0.0))
        w //= 2

    def rows(m):
        return jnp.asarray(np.broadcast_to(np.stack(m)[:, :, None], (len(m), c, HG_DIM)).astype(np.float32))

    tri_f = (t[None, :] <= t[:, None]).astype(np.float32)
    return (jnp.asarray(tri_f, BF16), jnp.asarray(tri_f.T, BF16),
            rows(negq_f), rows(negq_r), rows(negq_r), rows(negq_f), jnp.asarray(np.stack(lmask)))


def _hgrn(hq, hf, hb, hi, lb_f, lb_b, n_ctx):
    b, lt, _ = hq.shape
    kern = functools.partial(_hgrn_kernel, n_ctx_chunks=n_ctx // HG_CHUNK, n_chunks=lt // HG_CHUNK)
    seq = pl.BlockSpec((1, lt, HG_DIM), lambda bi, h: (bi, 0, h))
    lbs = pl.BlockSpec((1, 1, HG_DIM), lambda bi, h: (h, 0, 0))
    consts = _hgrn_level_constants()
    return pl.pallas_call(
        kern,
        grid=(b, HG_HEADS),
        in_specs=[seq, seq, seq, seq, lbs, lbs]
        + [pl.BlockSpec(a.shape, lambda bi, h, nd=a.ndim: (0,) * nd) for a in consts],
        out_specs=seq,
        out_shape=jax.ShapeDtypeStruct((b, lt, HG_WIDTH), F32),
        scratch_shapes=[pltpu.VMEM((HG_DIM, HG_DIM), F32), pltpu.VMEM((HG_DIM, HG_DIM), F32),
                        pltpu.VMEM((lt, HG_DIM), F32)],
        compiler_params=_cparams(("arbitrary", "arbitrary")),
        name="hgrn2",
    )(hq, hf, hb, hi, lb_f.reshape(HG_HEADS, 1, HG_DIM), lb_b.reshape(HG_HEADS, 1, HG_DIM), *consts)


def _split2(x):
    hi = x.astype(BF16)
    lo = (x - hi.astype(F32)).astype(BF16)
    return hi, lo


def _route(logits, bias):
    biased = jax.nn.sigmoid(logits) + bias
    rows = [biased[e:e + 1] for e in range(N_EXPERTS)]
    n = EXPERTS_PER_GROUP
    best = None
    g_sel = None
    for g in range(N_GROUPS):
        gs = None
        for i in range(n):
            for j2 in range(i + 1, n):
                pair = rows[g * n + i] + rows[g * n + j2]
                gs = pair if gs is None else jnp.maximum(gs, pair)
        if best is None:
            best, g_sel = gs, jnp.zeros(gs.shape, jnp.int32)
        else:
            take = gs > best
            best = jnp.where(take, gs, best)
            g_sel = jnp.where(take, g, g_sel)
    cand = []
    for i in range(n):
        c_i = rows[(N_GROUPS - 1) * n + i]
        for g in range(N_GROUPS - 2, -1, -1):
            c_i = jnp.where(g_sel == g, rows[g * n + i], c_i)
        cand.append(c_i)
    m1, i1 = cand[0], jnp.zeros(best.shape, jnp.int32)
    for i in range(1, n):
        take = cand[i] > m1
        m1 = jnp.where(take, cand[i], m1)
        i1 = jnp.where(take, i, i1)
    m2 = jnp.full(best.shape, -jnp.inf, F32)
    i2 = jnp.zeros(best.shape, jnp.int32)
    for i in range(n):
        take = (i1 != i) & (cand[i] > m2)
        m2 = jnp.where(take, cand[i], m2)
        i2 = jnp.where(take, i, i2)
    lo = jnp.minimum(i1, i2)
    hi = jnp.maximum(i1, i2)
    pair = jnp.where(lo == 0, hi - 1, jnp.where(lo == 1, hi + 1, N_PAIRS - 1))
    return g_sel * N_PAIRS + pair


def _merge_kernel(ona_ref, ohg_ref, og_ref, ga_ref, gh_ref, x0_ref, x_ref, mod_ref, hgg_ref, n2g_ref,
                  wna_ref, whg_ref, wout_ref, wr_ref, rb_ref,
                  xn_ref, h2_ref, bucket_ref, rank_ref, cnt_ref, carry_ref, hprev_ref,
                  *, n_t, n_tot, skip):
    step = pl.program_id(0)
    first_tile = (jnp.minimum(step, n_tot - 1) % n_t + skip) == 0

    @pl.when(step == 0)
    def _():
        carry_ref[...] = jnp.zeros_like(carry_ref)
        hprev_ref[...] = jnp.zeros_like(hprev_ref)

    w_hi, w_lo = _split2(wr_ref[...])
    h_hi, h_lo = _split2(hprev_ref[...])
    logits = (lax.dot_general(w_hi, h_hi, _NT, preferred_element_type=F32)
              + lax.dot_general(w_hi, h_lo, _NT, preferred_element_type=F32)
              + lax.dot_general(w_lo, h_hi, _NT, preferred_element_type=F32))

    ohg = ohg_ref[0]
    og = og_ref[0]
    gain = hgg_ref[...]
    heads = []
    for h in range(HG_HEADS):
        sl = slice(h * HG_DIM, (h + 1) * HG_DIM)
        oh = ohg[:, sl]
        yh = oh * lax.rsqrt(jnp.mean(oh * oh, axis=-1, keepdims=True) + EPS) * gain
        gt = og[:, sl]
        heads.append((yh * (gt * jax.nn.sigmoid(gt))).astype(BF16))
    hn = jnp.concatenate(heads, axis=-1)
    y_na = jnp.dot(ona_ref[0].astype(BF16), wna_ref[0], preferred_element_type=F32)
    y_hg = jnp.dot(hn, whg_ref[0], preferred_element_type=F32)

    bucket = _route(logits, rb_ref[...])
    bucket_ref[0] = bucket

    m = jax.nn.sigmoid(ga_ref[0]) * y_na + jax.nn.sigmoid(gh_ref[0]) * y_hg
    y = jnp.dot(m.astype(BF16), wout_ref[0], preferred_element_type=F32)

    t = bucket.shape[1]
    onehot = (lax.broadcasted_iota(jnp.int32, (BUCKET_ROWS, t), 0) == bucket).astype(F32)
    before = (lax.broadcasted_iota(jnp.int32, (t, t), 0)
              < lax.broadcasted_iota(jnp.int32, (t, t), 1)).astype(BF16)
    prefix = jnp.dot(onehot.astype(BF16), before, preferred_element_type=F32)
    carry = carry_ref[...]
    rank = jnp.sum(onehot * (prefix + carry[:, 0:1]), axis=0, keepdims=True)
    rank_ref[0] = rank.astype(jnp.int32)
    live = (step > 0).astype(F32)
    carry = carry + live * jnp.sum(onehot, axis=1, keepdims=True)
    carry_ref[...] = carry
    cnt_ref[...] = carry.astype(jnp.int32)

    xn = jnp.where(first_tile, x0_ref[0], x_ref[0]) + mod_ref[0, 2:3, :] * y
    xn_ref[0] = xn
    yn = xn * lax.rsqrt(jnp.mean(xn * xn, axis=-1, keepdims=True) + EPS) * n2g_ref[...]
    h2 = yn * (1.0 + mod_ref[0, 4:5, :]) + mod_ref[0, 3:4, :]
    h2_ref[...] = h2
    hprev_ref[...] = h2


def _merge(o_na, o_hg, og, ga, gh, stream, mods, hg_gain, n2_gain, w_na_o, w_hg_o, w_out, layer,
           w_router_t, router_bias, ctx_row, skip):
    first, rest, n_first = stream
    b, lt, _ = o_hg.shape
    d = first.shape[-1]
    n_t = lt // ROW_TILE - skip
    n_tot = b * n_t

    def tile(s):
        t = jnp.minimum(s, n_tot - 1)
        return t // n_t, t % n_t

    def rows(width):
        return pl.BlockSpec((1, ROW_TILE, width), lambda s: (tile(s)[0], tile(s)[1] + skip, 0))

    def full(a):
        return pl.BlockSpec(a.shape, lambda s: (0,) * a.ndim)

    def of_layer(a):
        return pl.BlockSpec((1,) + a.shape[1:], lambda s: (layer,) + (0,) * (a.ndim - 1))

    def mod_map(s):
        bi, j = tile(s)
        return (jnp.where(j + skip == 0, ctx_row, bi), 0, 0)

    tok = pl.BlockSpec((1, 1, ROW_TILE), lambda s: (jnp.maximum(s - 1, 0), 0, 0))
    consts = [hg_gain, n2_gain, w_na_o, w_hg_o, w_out, w_router_t, router_bias]
    const_specs = [full(hg_gain), full(n2_gain), of_layer(w_na_o), of_layer(w_hg_o), of_layer(w_out),
                   full(w_router_t), full(router_bias)]
    kern = functools.partial(_merge_kernel, n_t=n_t, n_tot=n_tot, skip=skip)
    return pl.pallas_call(
        kern,
        grid=(n_tot + 1,),
        in_specs=[pl.BlockSpec((1, ROW_TILE, NA_WIDTH), lambda s: tile(s) + (0,)),
                  rows(HG_WIDTH), rows(HG_WIDTH), rows(d), rows(d),
                  pl.BlockSpec((1, ROW_TILE, d), lambda s: (tile(s)[0], 0, 0)),
                  pl.BlockSpec((1, ROW_TILE, d),
                               lambda s: (tile(s)[0], jnp.maximum(tile(s)[1] + skip - n_first, 0), 0)),
                  pl.BlockSpec((1, 6, d), mod_map)] + const_specs,
        out_specs=[pl.BlockSpec((1, ROW_TILE, d), lambda s: tile(s) + (0,)),
                   pl.BlockSpec((ROW_TILE, d), lambda s: (jnp.minimum(s, n_tot - 1), 0)),
                   tok, tok,
                   pl.BlockSpec((BUCKET_ROWS, LANES), lambda s: (0, 0))],
        out_shape=[jax.ShapeDtypeStruct((b, n_t * ROW_TILE, d), F32),
                   jax.ShapeDtypeStruct((b * n_t * ROW_TILE, d), F32),
                   jax.ShapeDtypeStruct((b * n_t, 1, ROW_TILE), jnp.int32),
                   jax.ShapeDtypeStruct((b * n_t, 1, ROW_TILE), jnp.int32),
                   jax.ShapeDtypeStruct((BUCKET_ROWS, LANES), jnp.int32)],
        scratch_shapes=[pltpu.VMEM((BUCKET_ROWS, LANES), F32), pltpu.VMEM((ROW_TILE, d), F32)],
        compiler_params=_cparams(("arbitrary",)),
        name="merge_router",
    )(o_na, o_hg, og, ga, gh, first, rest, mods, *consts)


def _start_row_copies(n, row_copy):
    for r in range(n):
        row_copy(r).start()


def _scatter_kernel(dest_ref, h_ref, xs_ref, zero_ref, sem, *, n_token_steps):
    n = h_ref.shape[0]
    step = pl.program_id(0)

    @pl.when(step < n_token_steps)
    def _():
        _start_row_copies(n, lambda r: pltpu.make_async_copy(
            h_ref.at[pl.ds(r, 1)], xs_ref.at[pl.ds(dest_ref[0, 0, r], 1)], sem))

    @pl.when(step >= n_token_steps)
    def _():
        zero_ref[...] = jnp.zeros_like(zero_ref)
        _start_row_copies(n, lambda r: pltpu.make_async_copy(
            zero_ref.at[pl.ds(0, 1)], xs_ref.at[pl.ds(dest_ref[0, 0, r], 1)], sem))

    pltpu.make_async_copy(h_ref, xs_ref.at[pl.ds(0, n)], sem).wait()


def _scatter_rows(h2, dest, pad_pos):
    t, d = h2.shape
    tile = 2 * ROW_TILE if t % (2 * ROW_TILE) == 0 and pad_pos.shape[0] % (2 * ROW_TILE) == 0 else ROW_TILE
    n_t = t // tile
    n_steps = n_t + pad_pos.shape[0] // tile
    kern = functools.partial(_scatter_kernel, n_token_steps=n_t)
    return pl.pallas_call(
        kern,
        grid=(n_steps,),
        in_specs=[pl.BlockSpec((1, 1, tile), lambda i: (i, 0, 0), memory_space=pltpu.SMEM),
                  pl.BlockSpec((tile, d), lambda i: (jnp.minimum(i, n_t - 1), 0))],
        out_specs=pl.BlockSpec(memory_space=pl.ANY),
        out_shape=jax.ShapeDtypeStruct((n_steps * tile, d), F32),
        scratch_shapes=[pltpu.VMEM((SUBLANES, d), F32), pltpu.SemaphoreType.DMA(())],
        compiler_params=_cparams(("arbitrary",)),
        name="scatter_rows",
    )(jnp.concatenate([dest, pad_pos]).reshape(n_steps, 1, tile), h2)


def _final_kernel(dest_ref, dest_next_ref, xn_ref, mod_ref, g_ref, ys_ref, o_ref, buf_ref, sems,
                  *, final_norm):
    n = buf_ref.shape[1]
    step = pl.program_id(0) * pl.num_programs(1) + pl.program_id(1)
    n_steps = pl.num_programs(0) * pl.num_programs(1)

    def gather(d_ref, slot):
        _start_row_copies(n, lambda r: pltpu.make_async_copy(
            ys_ref.at[pl.ds(d_ref[0, 0, r], 1)], buf_ref.at[slot, pl.ds(r, 1)], sems.at[slot]))

    @pl.when(step == 0)
    def _():
        gather(dest_ref, 0)

    @pl.when(step + 1 < n_steps)
    def _():
        gather(dest_next_ref, (step + 1) % 2)

    slot = step % 2
    pltpu.make_async_copy(ys_ref.at[pl.ds(0, n)], buf_ref.at[slot], sems.at[slot]).wait()
    x = xn_ref[0] + mod_ref[0, 5:6, :] * buf_ref[slot]
    if final_norm:
        x = x * lax.rsqrt(jnp.mean(x * x, axis=-1, keepdims=True) + EPS) * g_ref[...]
    o_ref[0] = x


def _gather_residual(xn, mods, ys, dest, gain, ctx_row, final_norm, skip):
    b, lt, d = xn.shape
    n_t = lt // ROW_TILE

    def mod_map(bi, j):
        return (jnp.where(j + skip == 0, ctx_row, bi), 0, 0)

    kern = functools.partial(_final_kernel, final_norm=final_norm)
    last_tile = b * n_t - 1
    dest3 = dest.reshape(b * n_t, 1, ROW_TILE)
    return pl.pallas_call(
        kern,
        grid=(b, n_t),
        in_specs=[pl.BlockSpec((1, 1, ROW_TILE), lambda bi, j: (bi * n_t + j, 0, 0),
                               memory_space=pltpu.SMEM),
                  pl.BlockSpec((1, 1, ROW_TILE), lambda bi, j: (jnp.minimum(bi * n_t + j + 1, last_tile), 0, 0),
                               memory_space=pltpu.SMEM),
                  pl.BlockSpec((1, ROW_TILE, d), lambda bi, j: (bi, j, 0)),
                  pl.BlockSpec((1, 6, d), mod_map),
                  pl.BlockSpec((1, d), lambda bi, j: (0, 0)),
                  pl.BlockSpec(memory_space=pl.ANY)],
        out_specs=pl.BlockSpec((1, ROW_TILE, d), lambda bi, j: (bi, j, 0)),
        out_shape=jax.ShapeDtypeStruct((b, lt, d), F32),
        scratch_shapes=[pltpu.VMEM((2, ROW_TILE, d), F32), pltpu.SemaphoreType.DMA((2,))],
        compiler_params=_cparams(("arbitrary", "arbitrary")),
        name="gather_residual",
    )(dest3, dest3, xn, mods, gain, ys)


def _moe_kernel(ea_ref, eb_ref, nused_ref, xs_ref, wra_ref, wrb_ref,
                wga_ref, wua_ref, wda_ref, wgb_ref, wub_ref, wdb_ref, ys_ref):
    del ea_ref, eb_ref
    i = pl.program_id(0)

    @pl.when(i < nused_ref[0])
    def _():
        x = xs_ref[...]
        s_a = jax.nn.sigmoid(jnp.sum(x * wra_ref[0], axis=-1, keepdims=True))
        s_b = jax.nn.sigmoid(jnp.sum(x * wrb_ref[0], axis=-1, keepdims=True))
        tot = s_a + s_b
        xb = x.astype(BF16)

        gate_a = jnp.dot(xb, wga_ref[0], preferred_element_type=F32)
        up_a = jnp.dot(xb, wua_ref[0], preferred_element_type=F32)
        gate_b = jnp.dot(xb, wgb_ref[0], preferred_element_type=F32)
        hid_a = (gate_a * jax.nn.sigmoid(gate_a) * up_a).astype(BF16)
        up_b = jnp.dot(xb, wub_ref[0], preferred_element_type=F32)
        y_a = jnp.dot(hid_a, wda_ref[0], preferred_element_type=F32)
        hid_b = (gate_b * jax.nn.sigmoid(gate_b) * up_b).astype(BF16)
        y_b = jnp.dot(hid_b, wdb_ref[0], preferred_element_type=F32)
        ys_ref[...] = (s_a / tot) * y_a + (s_b / tot) * y_b

    @pl.when(i >= nused_ref[0])
    def _():
        ys_ref[...] = jnp.zeros_like(ys_ref)


def _moe(xs, tile_ea, tile_eb, n_used, w_router_rows, w_gate, w_up, w_down, layer):
    n_sorted, d = xs.shape
    n_tiles = n_sorted // ROW_TILE
    ff = w_gate.shape[-1]
    base = layer * N_EXPERTS

    def by_a(i, ea, eb, nu):
        return (ea[i], 0, 0)

    def by_b(i, ea, eb, nu):
        return (eb[i], 0, 0)

    def wt_a(i, ea, eb, nu):
        return (base + ea[i], 0, 0)

    def wt_b(i, ea, eb, nu):
        return (base + eb[i], 0, 0)

    grid_spec = pltpu.PrefetchScalarGridSpec(
        num_scalar_prefetch=3,
        grid=(n_tiles,),
        in_specs=[pl.BlockSpec((ROW_TILE, d), lambda i, ea, eb, nu: (i, 0)),
                  pl.BlockSpec((1, 1, d), by_a), pl.BlockSpec((1, 1, d), by_b),
                  pl.BlockSpec((1, d, ff), wt_a), pl.BlockSpec((1, d, ff), wt_a),
                  pl.BlockSpec((1, ff, d), wt_a),
                  pl.BlockSpec((1, d, ff), wt_b), pl.BlockSpec((1, d, ff), wt_b),
                  pl.BlockSpec((1, ff, d), wt_b)],
        out_specs=pl.BlockSpec((ROW_TILE, d), lambda i, ea, eb, nu: (i, 0)),
    )
    return pl.pallas_call(
        _moe_kernel,
        grid_spec=grid_spec,
        out_shape=jax.ShapeDtypeStruct((n_sorted, d), F32),
        compiler_params=_cparams(("arbitrary",)),
        name="moe_pairs",
    )(tile_ea, tile_eb, n_used, xs, w_router_rows, w_router_rows,
      w_gate, w_up, w_down, w_gate, w_up, w_down)


_PAIR_LO = np.array([0, 0, 0, 1, 1, 2], np.int32)
_PAIR_HI = np.array([1, 2, 3, 2, 3, 3], np.int32)


def _sorted_layout(bucket, rank, counts, n_tiles):
    counts = counts[:N_BUCKETS]
    padded = ((counts + ROW_TILE - 1) // ROW_TILE) * ROW_TILE
    ends = jnp.cumsum(padded)
    starts = ends - padded
    dest = starts[bucket] + rank
    tile_start = jnp.arange(n_tiles, dtype=jnp.int32) * ROW_TILE
    tile_bucket = jnp.sum((ends[None, :] <= tile_start[:, None]).astype(jnp.int32), axis=1)
    tile_bucket = jnp.minimum(tile_bucket, N_BUCKETS - 1)
    group = tile_bucket // N_PAIRS
    pair = tile_bucket % N_PAIRS
    ea = group * EXPERTS_PER_GROUP + jnp.asarray(_PAIR_LO)[pair]
    eb = group * EXPERTS_PER_GROUP + jnp.asarray(_PAIR_HI)[pair]
    n_used = (ends[-1] // ROW_TILE).astype(jnp.int32).reshape(1)
    seg_start = jnp.concatenate([starts + counts, ends[-1:]])
    seg_len = jnp.concatenate([padded - counts, n_tiles * ROW_TILE - ends[-1:]])
    seg_end = jnp.cumsum(seg_len)
    k = jnp.arange(n_tiles * ROW_TILE - bucket.shape[0], dtype=jnp.int32)
    seg = jnp.sum((seg_end[None, :] <= k[:, None]).astype(jnp.int32), axis=1)
    pad_pos = seg_start[seg] + k - (seg_end - seg_len)[seg]
    return (dest.astype(jnp.int32), pad_pos.astype(jnp.int32), ea.astype(jnp.int32), eb.astype(jnp.int32),
            n_used)


def _lower_bounds(raw):
    p = jax.nn.softmax(raw.astype(F32), axis=0)
    return jnp.cumsum(p, axis=0) - p[0:1]


def kernel(x, c, ctx, c_ctx, w_ada, b_ada, norm1_g, w_in, na_rel_bias, hg_lower_fwd, hg_lower_bwd,
           hg_norm_g, w_na_o, w_hg_o, w_out, norm2_g, w_router, router_bias, w_gate, w_up, w_down,
           final_g):
    b, seq, d = x.shape
    n_ctx = ctx.shape[1]
    depth = w_ada.shape[0]
    lt = n_ctx + seq
    assert n_ctx % ROW_TILE == 0 and seq % ROW_TILE == 0 and seq % GRID_W == 0
    assert n_ctx % HG_CHUNK == 0 and seq % HG_CHUNK == 0
    assert seq // GRID_W >= NA_WIN_ROWS + NA_Q_ROWS - 1 and (seq // GRID_W) % NA_Q_ROWS == 0
    assert n_ctx % (NA_Q_ROWS * GRID_W) == 0

    ada_rows = -(-(b + 1) // 8) * 8
    ctx_row = b
    cc = jnp.concatenate([c, c_ctx[None, :], jnp.zeros((ada_rows - b - 1, d), F32)], axis=0)
    mods = _ada(cc, w_ada, b_ada).reshape(depth, ada_rows, 6, d)

    lb_f = _lower_bounds(hg_lower_fwd)
    lb_b = _lower_bounds(hg_lower_bwd)
    cos_t, sin_t = _rope_tables(n_ctx, seq)
    w_router_t = jnp.transpose(w_router)
    w_router_rows = w_router_t.reshape(N_EXPERTS, 1, d)
    rb = router_bias.astype(F32).reshape(N_EXPERTS, 1)

    w_in_b, w_na_b, w_hg_b, w_out_b = (w.astype(BF16) for w in (w_in, w_na_o, w_hg_o, w_out))
    ff = w_gate.shape[-1]
    w_gate_b = w_gate.astype(BF16).reshape(depth * N_EXPERTS, d, ff)
    w_up_b = w_up.astype(BF16).reshape(depth * N_EXPERTS, d, ff)
    w_down_b = w_down.astype(BF16).reshape(depth * N_EXPERTS, ff, d)

    assert n_ctx == ROW_TILE
    stream = (ctx, x, 1)
    out = None
    for l in range(depth):
        last = l == depth - 1
        skip = n_ctx // ROW_TILE if last else 0
        n_tiles = b * (lt // ROW_TILE - skip) + N_BUCKETS
        q, k, v, hq, hf, hb, hi, og, ga, gh = _inproj(
            stream, lt, mods[l], norm1_g[l].reshape(1, d), w_in_b, l, cos_t, sin_t, ctx_row)
        o_na = _na_attention(q, k, v, na_rel_bias[l], n_ctx, with_ctx=not last)
        o_hg = _hgrn(hq, hf, hb, hi, lb_f[l], lb_b[l], n_ctx)
        xn, h2, bucket, rank, counts = _merge(
            o_na, o_hg, og, ga, gh, stream, mods[l], hg_norm_g[l].reshape(1, HG_DIM),
            norm2_g[l].reshape(1, d), w_na_b, w_hg_b, w_out_b, l, w_router_t, rb, ctx_row, skip)
        dest, pad_pos, tile_ea, tile_eb, n_used = _sorted_layout(
            bucket.reshape(-1), rank.reshape(-1), counts[:, 0], n_tiles)
        xs = _scatter_rows(h2, dest, pad_pos)
        ys = _moe(xs, tile_ea, tile_eb, n_used, w_router_rows, w_gate_b, w_up_b, w_down_b, l)
        res = _gather_residual(xn, mods[l], ys, dest, final_g.reshape(1, d), ctx_row, last, skip)
        if last:
            out = res
        else:
            stream = (res, res, 0)
    return out
```

```python
import functools
import math

import jax
import jax.numpy as jnp
import numpy as np
from jax import lax
from jax.experimental import pallas as pl
from jax.experimental.pallas import tpu as pltpu

F32 = jnp.float32
BF16 = jnp.bfloat16

GRID_W = 64
EPS = 1e-6
NA_HEADS = 8
NA_HEAD_DIM = 64
NA_WIDTH = NA_HEADS * NA_HEAD_DIM
NA_WIN_ROWS = 8
NA_WIN_COLS = 16
ROPE_THETA = 10000.0
HG_HEADS = 4
HG_DIM = 128
HG_WIDTH = HG_HEADS * HG_DIM
GATE_FLOOR = 1e-30
N_EXPERTS = 16
N_GROUPS = 4
EXPERTS_PER_GROUP = 4
N_PAIRS = 6
N_BUCKETS = N_GROUPS * N_PAIRS
BUCKET_ROWS = 32
MASKED = -1e30
LOG2E = 1.4426950408889634

LANES = 128
MXU_COLS = 256
ROW_TILE = 256
NA_Q_ROWS = 2
SUBLANES = 8
HG_CHUNK = 128
HG_UNROLL = 4
VMEM_LIMIT = 56 * 1024 * 1024

_NT = (((1,), (1,)), ((), ()))
_TN = (((0,), (0,)), ((), ()))


def _cparams(sem):
    return pltpu.CompilerParams(dimension_semantics=sem, vmem_limit_bytes=VMEM_LIMIT)


def _ada_kernel(c_ref, w_ref, b_ref, o_ref):
    cc = c_ref[...]
    s = cc * jax.nn.sigmoid(cc)
    o_ref[0] = jnp.dot(s, w_ref[0], preferred_element_type=F32) + b_ref[0]


def _ada(cc, w_ada, b_ada):
    depth, d, n = w_ada.shape
    rows = cc.shape[0]
    tn = 1536
    return pl.pallas_call(
        _ada_kernel,
        grid=(depth, n // tn),
        in_specs=[pl.BlockSpec((rows, d), lambda l, j: (0, 0)),
                  pl.BlockSpec((1, d, tn), lambda l, j: (l, 0, j)),
                  pl.BlockSpec((1, 1, tn), lambda l, j: (l, 0, j))],
        out_specs=pl.BlockSpec((1, rows, tn), lambda l, j: (l, 0, j)),
        out_shape=jax.ShapeDtypeStruct((depth, rows, n), F32),
        compiler_params=_cparams(("arbitrary", "arbitrary")),
        name="ada",
    )(cc, w_ada, b_ada.reshape(depth, 1, n))


def _inproj_kernel(x0_ref, x_ref, mod_ref, g_ref, w_ref, cos_ref, sin_ref, lbf_ref, lbb_ref,
                   q_ref, k_ref, v_ref, hq_ref, lfh_ref, lfl_ref, kf_ref, lbh_ref, lbl_ref, kb_ref,
                   hi_ref, og_ref, ga_ref, gh_ref):
    x = jnp.where(pl.program_id(1) == 0, x0_ref[0], x_ref[0])
    y = x * lax.rsqrt(jnp.mean(x * x, axis=-1, keepdims=True) + EPS) * g_ref[...]
    h = (y * (1.0 + mod_ref[0, 1:2, :]) + mod_ref[0, 0:1, :]).astype(BF16)

    def proj(off, width):
        return jnp.dot(h, w_ref[0, :, off:off + width], preferred_element_type=F32)

    cos = cos_ref[...]
    sin = sin_ref[...]
    even = (lax.broadcasted_iota(jnp.int32, cos.shape, 1) % 2) == 0

    def rope(z):
        swapped = jnp.where(even, pltpu.roll(z, LANES - 1, 1), pltpu.roll(z, 1, 1))
        return z * cos + swapped * sin

    scale = NA_HEAD_DIM ** -0.5 * LOG2E
    d = x.shape[-1]

    def rope_out(o_ref, mult):
        def epilogue(z, sl):
            for sub in range(MXU_COLS // LANES):
                src = slice(sub * LANES, (sub + 1) * LANES)
                dst = slice(sl.start + sub * LANES, sl.start + (sub + 1) * LANES)
                r = rope(z[:, src])
                o_ref[0, :, dst] = (r if mult is None else r * mult).astype(o_ref.dtype)
        return epilogue

    def plain_out(o_ref):
        def epilogue(z, sl):
            o_ref[0, :, sl] = z.astype(o_ref.dtype)
        return epilogue

    def silu_out(o_ref):
        def epilogue(z, sl):
            o_ref[0, :, sl] = (z * jax.nn.sigmoid(z)).astype(o_ref.dtype)
        return epilogue

    def gate_out(lb_ref, hi_ref, lo_ref, key_ref):
        def epilogue(f, sl):
            lb = lb_ref[:, sl]
            one_m_lb = 1.0 - lb
            g = lb + one_m_lb * jax.nn.sigmoid(f)
            lg = jnp.log(jnp.maximum(g, GATE_FLOOR)) * LOG2E
            hi = lg.astype(BF16)
            hi_ref[0, :, sl] = hi
            lo_ref[0, :, sl] = (lg - hi.astype(F32)).astype(BF16)
            key_ref[0, :, sl] = (one_m_lb * jax.nn.sigmoid(-f)).astype(BF16)
        return epilogue

    segments = [(NA_WIDTH, rope_out(q_ref, scale), True), (NA_WIDTH, rope_out(k_ref, None), True),
                (NA_WIDTH, plain_out(v_ref), False), (HG_WIDTH, silu_out(hq_ref), True),
                (HG_WIDTH, gate_out(lbf_ref, lfh_ref, lfl_ref, kf_ref), True),
                (HG_WIDTH, gate_out(lbb_ref, lbh_ref, lbl_ref, kb_ref), True),
                (HG_WIDTH, plain_out(hi_ref), False), (HG_WIDTH, plain_out(og_ref), False),
                (d, plain_out(ga_ref), False), (d, plain_out(gh_ref), False)]
    heavy, light, off = [], [], 0
    for width, epilogue, is_heavy in segments:
        for c in range(0, width, MXU_COLS):
            (heavy if is_heavy else light).append((off + c, slice(c, c + MXU_COLS), epilogue))
        off += width
    order = []
    while heavy or light:
        if heavy:
            order.append(heavy.pop(0))
        order.extend(light[:1])
        del light[:1]
    z = proj(order[0][0], MXU_COLS)
    for i, (_, sl, epilogue) in enumerate(order):
        z_next = proj(order[i + 1][0], MXU_COLS) if i + 1 < len(order) else None
        epilogue(z, sl)
        z = z_next


def _stream_specs(first, rest, n_first):
    d = first.shape[-1]
    return (pl.BlockSpec((1, ROW_TILE, d), lambda bi, j: (bi, 0, 0)),
            pl.BlockSpec((1, ROW_TILE, d), lambda bi, j: (bi, jnp.maximum(j - n_first, 0), 0)))


def _inproj(stream, lt, mods, g, w_all, layer, cos_t, sin_t, lb_f, lb_b, ctx_row):
    first, rest, n_first = stream
    b, _, d = first.shape
    n_t = lt // ROW_TILE

    def rows(width):
        return pl.BlockSpec((1, ROW_TILE, width), lambda bi, j: (bi, j, 0))

    def mod_map(bi, j):
        return (jnp.where(j == 0, ctx_row, bi), 0, 0)

    widths = [NA_WIDTH] * 3 + [HG_WIDTH] * 9 + [d, d]
    return pl.pallas_call(
        _inproj_kernel,
        grid=(b, n_t),
        in_specs=[*_stream_specs(first, rest, n_first),
                  pl.BlockSpec((1, 6, d), mod_map),
                  pl.BlockSpec((1, d), lambda bi, j: (0, 0)),
                  pl.BlockSpec((1,) + w_all.shape[1:], lambda bi, j: (layer, 0, 0)),
                  pl.BlockSpec((ROW_TILE, LANES), lambda bi, j: (j, 0)),
                  pl.BlockSpec((ROW_TILE, LANES), lambda bi, j: (j, 0)),
                  pl.BlockSpec((1, HG_WIDTH), lambda bi, j: (0, 0)),
                  pl.BlockSpec((1, HG_WIDTH), lambda bi, j: (0, 0))],
        out_specs=[rows(w) for w in widths],
        out_shape=[jax.ShapeDtypeStruct((b, lt, w), BF16) for w in widths],
        compiler_params=_cparams(("arbitrary", "arbitrary")),
        name="inproj",
    )(first, rest, mods, g, w_all, cos_t, sin_t, lb_f.reshape(1, HG_WIDTH), lb_b.reshape(1, HG_WIDTH))


def _rope_tables(n_ctx, seq):
    t = jnp.arange(seq, dtype=jnp.int32)
    row = (t // GRID_W).astype(F32)
    col = (t % GRID_W).astype(F32)
    rot_half = NA_HEAD_DIM // 2
    inv = ROPE_THETA ** (-jnp.arange(0, rot_half, 2, dtype=F32) / rot_half)
    ang = jnp.concatenate([row[:, None] * inv, col[:, None] * inv], axis=-1)
    cos = jnp.repeat(jnp.cos(ang), 2, axis=-1)
    sin = jnp.repeat(jnp.sin(ang), 2, axis=-1)
    sign = jnp.asarray(np.tile(np.array([-1.0, 1.0], np.float32), NA_HEAD_DIM // 2))
    sin = sin * sign
    cos = jnp.concatenate([jnp.ones((n_ctx, NA_HEAD_DIM), F32), cos], axis=0)
    sin = jnp.concatenate([jnp.zeros((n_ctx, NA_HEAD_DIM), F32), sin], axis=0)
    reps = LANES // NA_HEAD_DIM
    return jnp.tile(cos, (1, reps)), jnp.tile(sin, (1, reps))


def _na_plan(n_ctx, grid_rows):
    span = NA_WIN_ROWS + NA_Q_ROWS - 1
    variants, var_of_step, ws_of_step = [], [], []
    for r0 in range(0, grid_rows, NA_Q_ROWS):
        rs = [min(max(r0 + dq - NA_WIN_ROWS // 2, 0), grid_rows - NA_WIN_ROWS) for dq in range(NA_Q_ROWS)]
        ws = min(rs[0], grid_rows - span)
        key = (r0 - ws,) + tuple(r - ws for r in rs)
        if key not in variants:
            variants.append(key)
        var_of_step.append(variants.index(key))
        ws_of_step.append(ws)
    ctx_steps = n_ctx // (NA_Q_ROWS * GRID_W)
    var_of_step = [len(variants)] * ctx_steps + var_of_step
    ws_of_step = [0] * ctx_steps + ws_of_step
    return variants, np.asarray(var_of_step, np.int32), np.asarray(ws_of_step, np.int32)


def _na_bias_planes(rel_bias, variants):
    span = NA_WIN_ROWS + NA_Q_ROWS - 1
    n_dr, n_dc = 2 * NA_WIN_ROWS - 1, 2 * NA_WIN_COLS - 1
    col = np.arange(GRID_W)[:, None]
    kc = np.arange(GRID_W)[None, :]
    cs = np.clip(col - NA_WIN_COLS // 2, 0, GRID_W - NA_WIN_COLS)
    col_ok = (kc >= cs) & (kc < cs + NA_WIN_COLS)
    oh_c = ((kc - col + NA_WIN_COLS - 1)[..., None] == np.arange(n_dc)) & col_ok[..., None]
    planes = jnp.einsum("hrc,xkc->hrxk", rel_bias.astype(F32), jnp.asarray(oh_c.astype(np.float32)),
                        precision=lax.Precision.HIGHEST)
    planes = jnp.where(jnp.asarray(col_ok), planes * LOG2E, MASKED)
    planes = jnp.concatenate([planes, jnp.full_like(planes[:, :1], MASKED)], axis=1)
    plane_of = np.full((len(variants) + 1, NA_Q_ROWS, span), n_dr, np.int32)
    for vi, key in enumerate(variants):
        for dq in range(NA_Q_ROWS):
            for j in range(key[1 + dq], key[1 + dq] + NA_WIN_ROWS):
                plane_of[vi, dq, j] = j - (key[0] + dq) + NA_WIN_ROWS - 1
    return planes, plane_of.reshape(-1)


def _na_kernel(var_ref, ws_ref, plane_ref, q_ref, k_ref, v_ref, b_ref, o_ref, sw_ref, sc_ref, t_ref,
               *, step_off, n_ctx):
    nq = q_ref.shape[1]
    win = sw_ref.shape[1]
    span = win // GRID_W
    j = pl.program_id(1) + step_off
    var = var_ref[j]

    @pl.when((pl.program_id(1) == 0) | (var != var_ref[jnp.maximum(j - 1, 0)]))
    def _():
        for h in range(NA_HEADS):
            for dq in range(nq // GRID_W):
                base = (var * (nq // GRID_W) + dq) * span
                row = jnp.concatenate([b_ref[h, plane_ref[base + jj]] for jj in range(span)], axis=1)
                t_ref[h, dq * GRID_W:(dq + 1) * GRID_W, :] = row

    start = pl.multiple_of(n_ctx + ws_ref[j] * GRID_W, GRID_W)
    low = lax.broadcasted_iota(jnp.int32, (nq, LANES), 1) < NA_HEAD_DIM
    for hp in range(NA_HEADS // 2):
        sl = slice(hp * LANES, (hp + 1) * LANES)
        q2 = q_ref[0, :, sl]
        kw = k_ref[0, pl.ds(start, win), sl]
        kc = k_ref[0, 0:n_ctx, sl]
        zero = jnp.zeros_like(q2)
        qs = jnp.concatenate([jnp.where(low, q2, zero), jnp.where(low, zero, q2)], axis=0)
        rows = slice(2 * hp * nq, (2 * hp + 2) * nq)
        bias = t_ref[2 * hp:2 * hp + 2].reshape(2 * nq, win)
        sw_ref[rows, :] = lax.dot_general(qs, kw, _NT, preferred_element_type=F32) + bias
        sc_ref[rows, :] = lax.dot_general(qs, kc, _NT, preferred_element_type=F32)
    sw = sw_ref[...]
    sc = sc_ref[...]
    m = jnp.maximum(jnp.max(sw, axis=-1, keepdims=True), jnp.max(sc, axis=-1, keepdims=True))
    pw = jnp.exp2(sw - m)
    pc = jnp.exp2(sc - m)
    inv = 1.0 / (jnp.sum(pw, axis=-1, keepdims=True) + jnp.sum(pc, axis=-1, keepdims=True))
    pw = pw.astype(BF16)
    pc = pc.astype(BF16)
    for hp in range(NA_HEADS // 2):
        sl = slice(hp * LANES, (hp + 1) * LANES)
        vw = v_ref[0, pl.ds(start, win), sl]
        vc = v_ref[0, 0:n_ctx, sl]
        rows = slice(2 * hp * nq, (2 * hp + 2) * nq)
        o2 = (jnp.dot(pw[rows], vw, preferred_element_type=F32)
              + jnp.dot(pc[rows], vc, preferred_element_type=F32)) * inv[rows]
        o_ref[0, :, sl] = jnp.where(low, o2[:nq], o2[nq:]).astype(o_ref.dtype)


def _na_attention(q, k, v, rel_bias, n_ctx, with_ctx):
    b, lt, _ = q.shape
    nq = NA_Q_ROWS * GRID_W
    span = NA_WIN_ROWS + NA_Q_ROWS - 1
    variants, var_of_step, ws_of_step = _na_plan(n_ctx, (lt - n_ctx) // GRID_W)
    planes, plane_of = _na_bias_planes(rel_bias, variants)
    step_off = 0 if with_ctx else n_ctx // nq
    n_steps = lt // nq - step_off

    kern = functools.partial(_na_kernel, step_off=step_off, n_ctx=n_ctx)
    grid_spec = pltpu.PrefetchScalarGridSpec(
        num_scalar_prefetch=3,
        grid=(b, n_steps),
        in_specs=[pl.BlockSpec((1, nq, NA_WIDTH), lambda bi, j, *_: (bi, j + step_off, 0)),
                  pl.BlockSpec((1, lt, NA_WIDTH), lambda bi, j, *_: (bi, 0, 0)),
                  pl.BlockSpec((1, lt, NA_WIDTH), lambda bi, j, *_: (bi, 0, 0)),
                  pl.BlockSpec(planes.shape, lambda bi, j, *_: (0, 0, 0, 0))],
        out_specs=pl.BlockSpec((1, nq, NA_WIDTH), lambda bi, j, *_: (bi, j, 0)),
        scratch_shapes=[pltpu.VMEM((NA_HEADS * nq, span * GRID_W), F32),
                        pltpu.VMEM((NA_HEADS * nq, n_ctx), F32),
                        pltpu.VMEM((NA_HEADS, nq, span * GRID_W), F32)],
    )
    return pl.pallas_call(
        kern,
        grid_spec=grid_spec,
        out_shape=jax.ShapeDtypeStruct((b, n_steps * nq, NA_WIDTH), BF16),
        compiler_params=_cparams(("arbitrary", "arbitrary")),
        name="na_attention",
    )(jnp.asarray(var_of_step), jnp.asarray(ws_of_step), jnp.asarray(plane_of), q, k, v, planes)


def _block_ref_rows(b, w, rev, upper_fill=None, lower_fill=None):
    c, n = b.shape
    off = w if rev else w - 1
    if upper_fill is not None or lower_fill is not None:
        parts = []
        for s in range(0, c, 2 * w):
            ref = jnp.broadcast_to(b[s + off:s + off + 1], (w, n))
            parts.append(ref if lower_fill is None else jnp.full((w, n), lower_fill, b.dtype))
            parts.append(ref if upper_fill is None else jnp.full((w, n), upper_fill, b.dtype))
        return jnp.concatenate(parts, axis=0)
    if 2 * w >= SUBLANES:
        parts = [jnp.broadcast_to(b[s + off:s + off + 1], (2 * w, n)) for s in range(0, c, 2 * w)]
        return parts[0] if len(parts) == 1 else jnp.concatenate(parts, axis=0)
    b3 = b.reshape(c // SUBLANES, SUBLANES, n)
    sub = lax.broadcasted_iota(jnp.int32, b3.shape, 1)
    r = None
    for s in range(0, SUBLANES, 2 * w):
        piece = jnp.broadcast_to(b3[:, s + off:s + off + 1, :], b3.shape)
        r = piece if r is None else jnp.where(sub >= s, piece, r)
    return r.reshape(c, n)


def _hgrn_chunks(chains, states, lmask_ref):
    n = len(chains)
    c = chains[0][0].shape[0]
    qb = [ch[0] for ch in chains]
    kb = [ch[2] for ch in chains]
    b = []
    for i in range(n):
        (hi, lo), tri = chains[i][1], chains[i][4]
        b.append(jnp.dot(tri, hi, preferred_element_type=F32) + jnp.dot(tri, lo, preferred_element_type=F32))

    states = list(states)
    o = []
    for i in range(n):
        vb, rev, sid = chains[i][3], chains[i][7], chains[i][8]
        b_last = b[i][0:1] if rev else b[i][c - 1:c]
        qe = qb[i] * jnp.exp2(b[i]).astype(BF16)
        ke = kb[i] * jnp.exp2(b_last - b[i]).astype(BF16)
        st = states[sid]
        o.append(lax.dot_general(qe, st.astype(BF16), _NT, preferred_element_type=F32))
        states[sid] = st * jnp.exp2(b_last) + lax.dot_general(vb, ke, _TN, preferred_element_type=F32)

    a = [None] * n
    w = c // 2
    for li in range(lmask_ref.shape[0]):
        for i in range(n):
            negq_ref, negk_ref, rev = chains[i][5], chains[i][6], chains[i][7]
            if w >= SUBLANES:
                eq = b[i] - _block_ref_rows(b[i], w, rev, -MASKED if rev else None, None if rev else -MASKED)
                ek = _block_ref_rows(b[i], w, rev, None if rev else MASKED, MASKED if rev else None) - b[i]
            else:
                d = b[i] - _block_ref_rows(b[i], w, rev)
                eq = d + negq_ref[li]
                ek = negk_ref[li] - d
            qw = qb[i] * jnp.exp2(eq).astype(BF16)
            kw = kb[i] * jnp.exp2(ek).astype(BF16)
            p = lax.dot_general(qw, kw, _NT, preferred_element_type=F32) * lmask_ref[li]
            a[i] = p if a[i] is None else a[i] + p
        w //= 2
    for i in range(n):
        vb = chains[i][3]
        o[i] = o[i] + jnp.dot(a[i].astype(BF16), vb, preferred_element_type=F32)
        diag = jnp.sum(qb[i].astype(F32) * kb[i].astype(F32), axis=-1, keepdims=True)
        o[i] = o[i] + diag * vb.astype(F32)
    return o, states


def _hgrn_kernel(hq_ref, hi_ref, lfh_ref, lfl_ref, kf_ref, lbh_ref, lbl_ref, kb_ref, trif_ref, trir_ref,
                 negqf_ref, negkf_ref, negqr_ref, negkr_ref, lmask_ref,
                 o_ref, sf_ref, sb_ref, of_ref, ob_ref, *, n_ctx_chunks, n_chunks):
    sf_ref[...] = jnp.zeros_like(sf_ref)
    sb_ref[...] = jnp.zeros_like(sb_ref)

    def steps(cf0, cb0, unroll):
        chains, rows = [], []
        for u in range(unroll):
            rf = pl.ds(pl.multiple_of((cf0 + u) * HG_CHUNK, HG_CHUNK), HG_CHUNK)
            rb = pl.ds(pl.multiple_of((cb0 - u) * HG_CHUNK, HG_CHUNK), HG_CHUNK)
            chains.append((hq_ref[0, rf, :], (lfh_ref[0, rf, :], lfl_ref[0, rf, :]), kf_ref[0, rf, :],
                           hi_ref[0, rf, :], trif_ref[...], negqf_ref, negkf_ref, False, 0))
            chains.append((hq_ref[0, rb, :], (lbh_ref[0, rb, :], lbl_ref[0, rb, :]), kb_ref[0, rb, :],
                           hi_ref[0, rb, :], trir_ref[...], negqr_ref, negkr_ref, True, 1))
            rows += [rf, rb]
        outs, (sf, sb) = _hgrn_chunks(chains, [sf_ref[...], sb_ref[...]], lmask_ref)
        for i, r in enumerate(rows):
            if i % 2 == 0:
                of_ref[r, :] = outs[i]
            else:
                ob_ref[r, :] = outs[i]
        sf_ref[...] = sf
        sb_ref[...] = sb

    n_lat = n_chunks - n_ctx_chunks
    u_ctx = math.gcd(HG_UNROLL, n_ctx_chunks)
    u_lat = math.gcd(HG_UNROLL, n_lat)

    def ctx_body(i, carry):
        steps(i * u_ctx, n_ctx_chunks - 1 - i * u_ctx, u_ctx)
        return carry

    def lat_body(i, carry):
        steps(n_ctx_chunks + i * u_lat, n_chunks - 1 - i * u_lat, u_lat)
        return carry

    lax.fori_loop(0, n_ctx_chunks // u_ctx, ctx_body, 0)
    lax.fori_loop(0, n_lat // u_lat, lat_body, 0)
    o_ref[0] = (of_ref[...] + ob_ref[...]).astype(o_ref.dtype)


def _hgrn_level_constants():
    c = HG_CHUNK
    t = np.arange(c)
    xor = t[:, None] ^ t[None, :]
    lmask, negq_f, negq_r = [], [], []
    w = c // 2
    while w >= 1:
        lmask.append(((xor >= w) & (xor < 2 * w)).astype(np.float32))
        upper = (t % (2 * w)) >= w
        negq_f.append(np.where(upper, 0.0, MASKED))
        negq_r.append(np.where(upper, MASKED, 0.0))
        w //= 2

    def rows(m):
        return jnp.asarray(np.broadcast_to(np.stack(m)[:, :, None], (len(m), c, HG_DIM)).astype(np.float32))

    tri_f = (t[None, :] <= t[:, None]).astype(np.float32)
    return (jnp.asarray(tri_f, BF16), jnp.asarray(tri_f.T, BF16),
            rows(negq_f), rows(negq_r), rows(negq_r), rows(negq_f), jnp.asarray(np.stack(lmask)))


def _hgrn(hq, hi, fwd, bwd, n_ctx):
    b, lt, _ = hq.shape
    kern = functools.partial(_hgrn_kernel, n_ctx_chunks=n_ctx // HG_CHUNK, n_chunks=lt // HG_CHUNK)
    seq = pl.BlockSpec((1, lt, HG_DIM), lambda bi, h: (bi, 0, h))
    consts = _hgrn_level_constants()
    return pl.pallas_call(
        kern,
        grid=(b, HG_HEADS),
        in_specs=[seq] * 8
        + [pl.BlockSpec(a.shape, lambda bi, h, nd=a.ndim: (0,) * nd) for a in consts],
        out_specs=seq,
        out_shape=jax.ShapeDtypeStruct((b, lt, HG_WIDTH), BF16),
        scratch_shapes=[pltpu.VMEM((HG_DIM, HG_DIM), F32), pltpu.VMEM((HG_DIM, HG_DIM), F32),
                        pltpu.VMEM((lt, HG_DIM), F32), pltpu.VMEM((lt, HG_DIM), F32)],
        compiler_params=_cparams(("arbitrary", "arbitrary")),
        name="hgrn2",
    )(hq, hi, *fwd, *bwd, *consts)


def _split2(x):
    hi = x.astype(BF16)
    lo = (x - hi.astype(F32)).astype(BF16)
    return hi, lo


def _route(logits, bias):
    biased = jax.nn.sigmoid(logits) + bias
    rows = [biased[e:e + 1] for e in range(N_EXPERTS)]
    n = EXPERTS_PER_GROUP
    best = None
    g_sel = None
    for g in range(N_GROUPS):
        gs = None
        for i in range(n):
            for j2 in range(i + 1, n):
                pair = rows[g * n + i] + rows[g * n + j2]
                gs = pair if gs is None else jnp.maximum(gs, pair)
        if best is None:
            best, g_sel = gs, jnp.zeros(gs.shape, jnp.int32)
        else:
            take = gs > best
            best = jnp.where(take, gs, best)
            g_sel = jnp.where(take, g, g_sel)
    cand = []
    for i in range(n):
        c_i = rows[(N_GROUPS - 1) * n + i]
        for g in range(N_GROUPS - 2, -1, -1):
            c_i = jnp.where(g_sel == g, rows[g * n + i], c_i)
        cand.append(c_i)
    m1, i1 = cand[0], jnp.zeros(best.shape, jnp.int32)
    for i in range(1, n):
        take = cand[i] > m1
        m1 = jnp.where(take, cand[i], m1)
        i1 = jnp.where(take, i, i1)
    m2 = jnp.full(best.shape, -jnp.inf, F32)
    i2 = jnp.zeros(best.shape, jnp.int32)
    for i in range(n):
        take = (i1 != i) & (cand[i] > m2)
        m2 = jnp.where(take, cand[i], m2)
        i2 = jnp.where(take, i, i2)
    lo = jnp.minimum(i1, i2)
    hi = jnp.maximum(i1, i2)
    pair = jnp.where(lo == 0, hi - 1, jnp.where(lo == 1, hi + 1, N_PAIRS - 1))
    return g_sel * N_PAIRS + pair


def _merge_kernel(ona_ref, ohg_ref, og_ref, ga_ref, gh_ref, x0_ref, x_ref, mod_ref, hgg_ref, n2g_ref,
                  wna_ref, whg_ref, wout_ref, wr_ref, rb_ref,
                  xn_ref, h2_ref, bucket_ref, rank_ref, cnt_ref, carry_ref, hprev_ref,
                  *, n_t, n_tot, skip):
    step = pl.program_id(0)
    first_tile = (jnp.minimum(step, n_tot - 1) % n_t + skip) == 0

    @pl.when(step == 0)
    def _():
        carry_ref[...] = jnp.zeros_like(carry_ref)
        hprev_ref[...] = jnp.zeros_like(hprev_ref)

    w_hi, w_lo = _split2(wr_ref[...])
    h_hi, h_lo = _split2(hprev_ref[...])
    logits = (lax.dot_general(w_hi, h_hi, _NT, preferred_element_type=F32)
              + lax.dot_general(w_hi, h_lo, _NT, preferred_element_type=F32)
              + lax.dot_general(w_lo, h_hi, _NT, preferred_element_type=F32))

    ohg = ohg_ref[0].astype(F32)
    og = og_ref[0].astype(F32)
    gain = hgg_ref[...]
    heads = []
    for h in range(HG_HEADS):
        sl = slice(h * HG_DIM, (h + 1) * HG_DIM)
        oh = ohg[:, sl]
        yh = oh * lax.rsqrt(jnp.mean(oh * oh, axis=-1, keepdims=True) + EPS) * gain
        gt = og[:, sl]
        heads.append((yh * (gt * jax.nn.sigmoid(gt))).astype(BF16))
    hn = jnp.concatenate(heads, axis=-1)
    y_na = jnp.dot(ona_ref[0], wna_ref[0], preferred_element_type=F32)
    y_hg = jnp.dot(hn, whg_ref[0], preferred_element_type=F32)

    bucket = _route(logits, rb_ref[...])
    bucket_ref[0] = bucket

    m = jax.nn.sigmoid(ga_ref[0].astype(F32)) * y_na + jax.nn.sigmoid(gh_ref[0].astype(F32)) * y_hg
    y = jnp.dot(m.astype(BF16), wout_ref[0], preferred_element_type=F32)

    t = bucket.shape[1]
    onehot = (lax.broadcasted_iota(jnp.int32, (BUCKET_ROWS, t), 0) == bucket).astype(F32)
    before = (lax.broadcasted_iota(jnp.int32, (t, t), 0)
              < lax.broadcasted_iota(jnp.int32, (t, t), 1)).astype(BF16)
    prefix = jnp.dot(onehot.astype(BF16), before, preferred_element_type=F32)
    carry = carry_ref[...]
    rank = jnp.sum(onehot * (prefix + carry[:, 0:1]), axis=0, keepdims=True)
    rank_ref[0] = rank.astype(jnp.int32)
    live = (step > 0).astype(F32)
    carry = carry + live * jnp.sum(onehot, axis=1, keepdims=True)
    carry_ref[...] = carry
    cnt_ref[...] = carry.astype(jnp.int32)

    xn = jnp.where(first_tile, x0_ref[0], x_ref[0]) + mod_ref[0, 2:3, :] * y
    xn_ref[0] = xn
    yn = xn * lax.rsqrt(jnp.mean(xn * xn, axis=-1, keepdims=True) + EPS) * n2g_ref[...]
    h2 = yn * (1.0 + mod_ref[0, 4:5, :]) + mod_ref[0, 3:4, :]
    h2_ref[...] = h2
    hprev_ref[...] = h2


def _merge(o_na, o_hg, og, ga, gh, stream, mods, hg_gain, n2_gain, w_na_o, w_hg_o, w_out, layer,
           w_router_t, router_bias, ctx_row, skip):
    first, rest, n_first = stream
    b, lt, _ = o_hg.shape
    d = first.shape[-1]
    n_t = lt // ROW_TILE - skip
    n_tot = b * n_t

    def tile(s):
        t = jnp.minimum(s, n_tot - 1)
        return t // n_t, t % n_t

    def rows(width):
        return pl.BlockSpec((1, ROW_TILE, width), lambda s: (tile(s)[0], tile(s)[1] + skip, 0))

    def full(a):
        return pl.BlockSpec(a.shape, lambda s: (0,) * a.ndim)

    def of_layer(a):
        return pl.BlockSpec((1,) + a.shape[1:], lambda s: (layer,) + (0,) * (a.ndim - 1))

    def mod_map(s):
        bi, j = tile(s)
        return (jnp.where(j + skip == 0, ctx_row, bi), 0, 0)

    tok = pl.BlockSpec((1, 1, ROW_TILE), lambda s: (jnp.maximum(s - 1, 0), 0, 0))
    consts = [hg_gain, n2_gain, w_na_o, w_hg_o, w_out, w_router_t, router_bias]
    const_specs = [full(hg_gain), full(n2_gain), of_layer(w_na_o), of_layer(w_hg_o), of_layer(w_out),
                   full(w_router_t), full(router_bias)]
    kern = functools.partial(_merge_kernel, n_t=n_t, n_tot=n_tot, skip=skip)
    return pl.pallas_call(
        kern,
        grid=(n_tot + 1,),
        in_specs=[pl.BlockSpec((1, ROW_TILE, NA_WIDTH), lambda s: tile(s) + (0,)),
                  rows(HG_WIDTH), rows(HG_WIDTH), rows(d), rows(d),
                  pl.BlockSpec((1, ROW_TILE, d), lambda s: (tile(s)[0], 0, 0)),
                  pl.BlockSpec((1, ROW_TILE, d),
                               lambda s: (tile(s)[0], jnp.maximum(tile(s)[1] + skip - n_first, 0), 0)),
                  pl.BlockSpec((1, 6, d), mod_map)] + const_specs,
        out_specs=[pl.BlockSpec((1, ROW_TILE, d), lambda s: tile(s) + (0,)),
                   pl.BlockSpec((ROW_TILE, d), lambda s: (jnp.minimum(s, n_tot - 1), 0)),
                   tok, tok,
                   pl.BlockSpec((BUCKET_ROWS, LANES), lambda s: (0, 0))],
        out_shape=[jax.ShapeDtypeStruct((b, n_t * ROW_TILE, d), F32),
                   jax.ShapeDtypeStruct((b * n_t * ROW_TILE, d), F32),
                   jax.ShapeDtypeStruct((b * n_t, 1, ROW_TILE), jnp.int32),
                   jax.ShapeDtypeStruct((b * n_t, 1, ROW_TILE), jnp.int32),
                   jax.ShapeDtypeStruct((BUCKET_ROWS, LANES), jnp.int32)],
        scratch_shapes=[pltpu.VMEM((BUCKET_ROWS, LANES), F32), pltpu.VMEM((ROW_TILE, d), F32)],
        compiler_params=_cparams(("arbitrary",)),
        name="merge_router",
    )(o_na, o_hg, og, ga, gh, first, rest, mods, *consts)


def _start_row_copies(n, row_copy):
    for r in range(n):
        row_copy(r).start()


def _scatter_kernel(dest_ref, h_ref, xs_ref, zero_ref, sem, *, n_token_steps):
    n = h_ref.shape[0]
    step = pl.program_id(0)

    @pl.when(step < n_token_steps)
    def _():
        _start_row_copies(n, lambda r: pltpu.make_async_copy(
            h_ref.at[pl.ds(r, 1)], xs_ref.at[pl.ds(dest_ref[0, 0, r], 1)], sem))

    @pl.when(step >= n_token_steps)
    def _():
        zero_ref[...] = jnp.zeros_like(zero_ref)
        _start_row_copies(n, lambda r: pltpu.make_async_copy(
            zero_ref.at[pl.ds(0, 1)], xs_ref.at[pl.ds(dest_ref[0, 0, r], 1)], sem))

    pltpu.make_async_copy(h_ref, xs_ref.at[pl.ds(0, n)], sem).wait()


def _scatter_rows(h2, dest, pad_pos):
    t, d = h2.shape
    tile = 2 * ROW_TILE if t % (2 * ROW_TILE) == 0 and pad_pos.shape[0] % (2 * ROW_TILE) == 0 else ROW_TILE
    n_t = t // tile
    n_steps = n_t + pad_pos.shape[0] // tile
    kern = functools.partial(_scatter_kernel, n_token_steps=n_t)
    return pl.pallas_call(
        kern,
        grid=(n_steps,),
        in_specs=[pl.BlockSpec((1, 1, tile), lambda i: (i, 0, 0), memory_space=pltpu.SMEM),
                  pl.BlockSpec((tile, d), lambda i: (jnp.minimum(i, n_t - 1), 0))],
        out_specs=pl.BlockSpec(memory_space=pl.ANY),
        out_shape=jax.ShapeDtypeStruct((n_steps * tile, d), F32),
        scratch_shapes=[pltpu.VMEM((SUBLANES, d), F32), pltpu.SemaphoreType.DMA(())],
        compiler_params=_cparams(("arbitrary",)),
        name="scatter_rows",
    )(jnp.concatenate([dest, pad_pos]).reshape(n_steps, 1, tile), h2)


def _final_kernel(dest_ref, dest_next_ref, xn_ref, mod_ref, g_ref, ys_ref, o_ref, buf_ref, sems,
                  *, final_norm):
    n = buf_ref.shape[1]
    step = pl.program_id(0) * pl.num_programs(1) + pl.program_id(1)
    n_steps = pl.num_programs(0) * pl.num_programs(1)

    def gather(d_ref, slot):
        _start_row_copies(n, lambda r: pltpu.make_async_copy(
            ys_ref.at[pl.ds(d_ref[0, 0, r], 1)], buf_ref.at[slot, pl.ds(r, 1)], sems.at[slot]))

    @pl.when(step == 0)
    def _():
        gather(dest_ref, 0)

    @pl.when(step + 1 < n_steps)
    def _():
        gather(dest_next_ref, (step + 1) % 2)

    slot = step % 2
    pltpu.make_async_copy(ys_ref.at[pl.ds(0, n)], buf_ref.at[slot], sems.at[slot]).wait()
    x = xn_ref[0] + mod_ref[0, 5:6, :] * buf_ref[slot]
    if final_norm:
        x = x * lax.rsqrt(jnp.mean(x * x, axis=-1, keepdims=True) + EPS) * g_ref[...]
    o_ref[0] = x


def _gather_residual(xn, mods, ys, dest, gain, ctx_row, final_norm, skip):
    b, lt, d = xn.shape
    n_t = lt // ROW_TILE

    def mod_map(bi, j):
        return (jnp.where(j + skip == 0, ctx_row, bi), 0, 0)

    kern = functools.partial(_final_kernel, final_norm=final_norm)
    last_tile = b * n_t - 1
    dest3 = dest.reshape(b * n_t, 1, ROW_TILE)
    return pl.pallas_call(
        kern,
        grid=(b, n_t),
        in_specs=[pl.BlockSpec((1, 1, ROW_TILE), lambda bi, j: (bi * n_t + j, 0, 0),
                               memory_space=pltpu.SMEM),
                  pl.BlockSpec((1, 1, ROW_TILE), lambda bi, j: (jnp.minimum(bi * n_t + j + 1, last_tile), 0, 0),
                               memory_space=pltpu.SMEM),
                  pl.BlockSpec((1, ROW_TILE, d), lambda bi, j: (bi, j, 0)),
                  pl.BlockSpec((1, 6, d), mod_map),
                  pl.BlockSpec((1, d), lambda bi, j: (0, 0)),
                  pl.BlockSpec(memory_space=pl.ANY)],
        out_specs=pl.BlockSpec((1, ROW_TILE, d), lambda bi, j: (bi, j, 0)),
        out_shape=jax.ShapeDtypeStruct((b, lt, d), F32),
        scratch_shapes=[pltpu.VMEM((2, ROW_TILE, d), F32), pltpu.SemaphoreType.DMA((2,))],
        compiler_params=_cparams(("arbitrary", "arbitrary")),
        name="gather_residual",
    )(dest3, dest3, xn, mods, gain, ys)


def _moe_kernel(ea_ref, eb_ref, nused_ref, xs_ref, wra_ref, wrb_ref,
                wga_ref, wua_ref, wda_ref, wgb_ref, wub_ref, wdb_ref, ys_ref):
    del ea_ref, eb_ref
    i = pl.program_id(0)

    @pl.when(i < nused_ref[0])
    def _():
        x = xs_ref[...]
        s_a = jax.nn.sigmoid(jnp.sum(x * wra_ref[0], axis=-1, keepdims=True))
        s_b = jax.nn.sigmoid(jnp.sum(x * wrb_ref[0], axis=-1, keepdims=True))
        tot = s_a + s_b
        xb = x.astype(BF16)

        gate_a = jnp.dot(xb, wga_ref[0], preferred_element_type=F32)
        up_a = jnp.dot(xb, wua_ref[0], preferred_element_type=F32)
        gate_b = jnp.dot(xb, wgb_ref[0], preferred_element_type=F32)
        hid_a = (gate_a * jax.nn.sigmoid(gate_a) * up_a).astype(BF16)
        up_b = jnp.dot(xb, wub_ref[0], preferred_element_type=F32)
        y_a = jnp.dot(hid_a, wda_ref[0], preferred_element_type=F32)
        hid_b = (gate_b * jax.nn.sigmoid(gate_b) * up_b).astype(BF16)
        y_b = jnp.dot(hid_b, wdb_ref[0], preferred_element_type=F32)
        ys_ref[...] = (s_a / tot) * y_a + (s_b / tot) * y_b

    @pl.when(i >= nused_ref[0])
    def _():
        ys_ref[...] = jnp.zeros_like(ys_ref)


def _moe(xs, tile_ea, tile_eb, n_used, w_router_rows, w_gate, w_up, w_down, layer):
    n_sorted, d = xs.shape
    n_tiles = n_sorted // ROW_TILE
    ff = w_gate.shape[-1]
    base = layer * N_EXPERTS

    def by_a(i, ea, eb, nu):
        return (ea[i], 0, 0)

    def by_b(i, ea, eb, nu):
        return (eb[i], 0, 0)

    def wt_a(i, ea, eb, nu):
        return (base + ea[i], 0, 0)

    def wt_b(i, ea, eb, nu):
        return (base + eb[i], 0, 0)

    grid_spec = pltpu.PrefetchScalarGridSpec(
        num_scalar_prefetch=3,
        grid=(n_tiles,),
        in_specs=[pl.BlockSpec((ROW_TILE, d), lambda i, ea, eb, nu: (i, 0)),
                  pl.BlockSpec((1, 1, d), by_a), pl.BlockSpec((1, 1, d), by_b),
                  pl.BlockSpec((1, d, ff), wt_a), pl.BlockSpec((1, d, ff), wt_a),
                  pl.BlockSpec((1, ff, d), wt_a),
                  pl.BlockSpec((1, d, ff), wt_b), pl.BlockSpec((1, d, ff), wt_b),
                  pl.BlockSpec((1, ff, d), wt_b)],
        out_specs=pl.BlockSpec((ROW_TILE, d), lambda i, ea, eb, nu: (i, 0)),
    )
    return pl.pallas_call(
        _moe_kernel,
        grid_spec=grid_spec,
        out_shape=jax.ShapeDtypeStruct((n_sorted, d), F32),
        compiler_params=_cparams(("arbitrary",)),
        name="moe_pairs",
    )(tile_ea, tile_eb, n_used, xs, w_router_rows, w_router_rows,
      w_gate, w_up, w_down, w_gate, w_up, w_down)


_PAIR_LO = np.array([0, 0, 0, 1, 1, 2], np.int32)
_PAIR_HI = np.array([1, 2, 3, 2, 3, 3], np.int32)


def _sorted_layout(bucket, rank, counts, n_tiles):
    counts = counts[:N_BUCKETS]
    padded = ((counts + ROW_TILE - 1) // ROW_TILE) * ROW_TILE
    ends = jnp.cumsum(padded)
    starts = ends - padded
    dest = starts[bucket] + rank
    tile_start = jnp.arange(n_tiles, dtype=jnp.int32) * ROW_TILE
    tile_bucket = jnp.sum((ends[None, :] <= tile_start[:, None]).astype(jnp.int32), axis=1)
    tile_bucket = jnp.minimum(tile_bucket, N_BUCKETS - 1)
    group = tile_bucket // N_PAIRS
    pair = tile_bucket % N_PAIRS
    ea = group * EXPERTS_PER_GROUP + jnp.asarray(_PAIR_LO)[pair]
    eb = group * EXPERTS_PER_GROUP + jnp.asarray(_PAIR_HI)[pair]
    n_used = (ends[-1] // ROW_TILE).astype(jnp.int32).reshape(1)
    seg_start = jnp.concatenate([starts + counts, ends[-1:]])
    seg_len = jnp.concatenate([padded - counts, n_tiles * ROW_TILE - ends[-1:]])
    seg_end = jnp.cumsum(seg_len)
    k = jnp.arange(n_tiles * ROW_TILE - bucket.shape[0], dtype=jnp.int32)
    seg = jnp.sum((seg_end[None, :] <= k[:, None]).astype(jnp.int32), axis=1)
    pad_pos = seg_start[seg] + k - (seg_end - seg_len)[seg]
    return (dest.astype(jnp.int32), pad_pos.astype(jnp.int32), ea.astype(jnp.int32), eb.astype(jnp.int32),
            n_used)


def _lower_bounds(raw):
    p = jax.nn.softmax(raw.astype(F32), axis=0)
    return jnp.cumsum(p, axis=0) - p[0:1]


def kernel(x, c, ctx, c_ctx, w_ada, b_ada, norm1_g, w_in, na_rel_bias, hg_lower_fwd, hg_lower_bwd,
           hg_norm_g, w_na_o, w_hg_o, w_out, norm2_g, w_router, router_bias, w_gate, w_up, w_down,
           final_g):
    b, seq, d = x.shape
    n_ctx = ctx.shape[1]
    depth = w_ada.shape[0]
    lt = n_ctx + seq
    assert n_ctx % ROW_TILE == 0 and seq % ROW_TILE == 0 and seq % GRID_W == 0
    assert n_ctx % HG_CHUNK == 0 and seq % HG_CHUNK == 0
    assert seq // GRID_W >= NA_WIN_ROWS + NA_Q_ROWS - 1 and (seq // GRID_W) % NA_Q_ROWS == 0
    assert n_ctx % (NA_Q_ROWS * GRID_W) == 0

    ada_rows = -(-(b + 1) // 8) * 8
    ctx_row = b
    cc = jnp.concatenate([c, c_ctx[None, :], jnp.zeros((ada_rows - b - 1, d), F32)], axis=0)
    mods = _ada(cc, w_ada, b_ada).reshape(depth, ada_rows, 6, d)

    lb_f = _lower_bounds(hg_lower_fwd)
    lb_b = _lower_bounds(hg_lower_bwd)
    cos_t, sin_t = _rope_tables(n_ctx, seq)
    w_router_t = jnp.transpose(w_router)
    w_router_rows = w_router_t.reshape(N_EXPERTS, 1, d)
    rb = router_bias.astype(F32).reshape(N_EXPERTS, 1)

    w_in_b, w_na_b, w_hg_b, w_out_b = (w.astype(BF16) for w in (w_in, w_na_o, w_hg_o, w_out))
    ff = w_gate.shape[-1]
    w_gate_b = w_gate.astype(BF16).reshape(depth * N_EXPERTS, d, ff)
    w_up_b = w_up.astype(BF16).reshape(depth * N_EXPERTS, d, ff)
    w_down_b = w_down.astype(BF16).reshape(depth * N_EXPERTS, ff, d)

    assert n_ctx == ROW_TILE
    stream = (ctx, x, 1)
    out = None
    for l in range(depth):
        last = l == depth - 1
        skip = n_ctx // ROW_TILE if last else 0
        n_tiles = b * (lt // ROW_TILE - skip) + N_BUCKETS
        q, k, v, hq, *gates, hi, og, ga, gh = _inproj(
            stream, lt, mods[l], norm1_g[l].reshape(1, d), w_in_b, l, cos_t, sin_t, lb_f[l], lb_b[l],
            ctx_row)
        o_na = _na_attention(q, k, v, na_rel_bias[l], n_ctx, with_ctx=not last)
        o_hg = _hgrn(hq, hi, gates[:3], gates[3:], n_ctx)
        xn, h2, bucket, rank, counts = _merge(
            o_na, o_hg, og, ga, gh, stream, mods[l], hg_norm_g[l].reshape(1, HG_DIM),
            norm2_g[l].reshape(1, d), w_na_b, w_hg_b, w_out_b, l, w_router_t, rb, ctx_row, skip)
        dest, pad_pos, tile_ea, tile_eb, n_used = _sorted_layout(
            bucket.reshape(-1), rank.reshape(-1), counts[:, 0], n_tiles)
        xs = _scatter_rows(h2, dest, pad_pos)
        ys = _moe(xs, tile_ea, tile_eb, n_used, w_router_rows, w_gate_b, w_up_b, w_down_b, l)
        res = _gather_residual(xn, mods[l], ys, dest, final_g.reshape(1, d), ctx_row, last, skip)
        if last:
            out = res
        else:
            stream = (res, res, 0)
    return out
```

```python
import functools
import math

import jax
import jax.numpy as jnp
import numpy as np
from jax import lax
from jax.experimental import pallas as pl
from jax.experimental.pallas import tpu as pltpu

F32 = jnp.float32
BF16 = jnp.bfloat16

GRID_W = 64
EPS = 1e-6
NA_HEADS = 8
NA_HEAD_DIM = 64
NA_WIDTH = NA_HEADS * NA_HEAD_DIM
NA_WIN_ROWS = 8
NA_WIN_COLS = 16
ROPE_THETA = 10000.0
HG_HEADS = 4
HG_DIM = 128
HG_WIDTH = HG_HEADS * HG_DIM
GATE_FLOOR = 1e-30
N_EXPERTS = 16
N_GROUPS = 4
EXPERTS_PER_GROUP = 4
N_PAIRS = 6
N_BUCKETS = N_GROUPS * N_PAIRS
BUCKET_ROWS = 32
MASKED = -1e30
LOG2E = 1.4426950408889634

LANES = 128
MXU_COLS = 256
ROW_TILE = 256
NA_Q_ROWS = 2
SUBLANES = 8
HG_CHUNK = 128
HG_UNROLL = 4
VMEM_LIMIT = 56 * 1024 * 1024

_NT = (((1,), (1,)), ((), ()))
_TN = (((0,), (0,)), ((), ()))


def _cparams(sem):
    return pltpu.CompilerParams(dimension_semantics=sem, vmem_limit_bytes=VMEM_LIMIT)


def _ada_kernel(c_ref, w_ref, b_ref, o_ref):
    cc = c_ref[...]
    s = cc * jax.nn.sigmoid(cc)
    o_ref[0] = jnp.dot(s, w_ref[0], preferred_element_type=F32) + b_ref[0]


def _ada(cc, w_ada, b_ada):
    depth, d, n = w_ada.shape
    rows = cc.shape[0]
    tn = 1536
    return pl.pallas_call(
        _ada_kernel,
        grid=(depth, n // tn),
        in_specs=[pl.BlockSpec((rows, d), lambda l, j: (0, 0)),
                  pl.BlockSpec((1, d, tn), lambda l, j: (l, 0, j)),
                  pl.BlockSpec((1, 1, tn), lambda l, j: (l, 0, j))],
        out_specs=pl.BlockSpec((1, rows, tn), lambda l, j: (l, 0, j)),
        out_shape=jax.ShapeDtypeStruct((depth, rows, n), F32),
        compiler_params=_cparams(("arbitrary", "arbitrary")),
        name="ada",
    )(cc, w_ada, b_ada.reshape(depth, 1, n))


def _inproj_kernel(x0_ref, x_ref, mod_ref, g_ref, w_ref, cos_ref, sin_ref, lbf_ref, lbb_ref,
                   q_ref, k_ref, v_ref, hq_ref, lfh_ref, lfl_ref, kf_ref, lbh_ref, lbl_ref, kb_ref,
                   hi_ref, og_ref, ga_ref, gh_ref):
    x = jnp.where(pl.program_id(1) == 0, x0_ref[0], x_ref[0])
    y = x * lax.rsqrt(jnp.mean(x * x, axis=-1, keepdims=True) + EPS) * g_ref[...]
    h = (y * (1.0 + mod_ref[0, 1:2, :]) + mod_ref[0, 0:1, :]).astype(BF16)

    def proj(off, width):
        return jnp.dot(h, w_ref[0, :, off:off + width], preferred_element_type=F32)

    cos = cos_ref[...]
    sin = sin_ref[...]
    even = (lax.broadcasted_iota(jnp.int32, cos.shape, 1) % 2) == 0

    def rope(z):
        swapped = jnp.where(even, pltpu.roll(z, LANES - 1, 1), pltpu.roll(z, 1, 1))
        return z * cos + swapped * sin

    scale = NA_HEAD_DIM ** -0.5 * LOG2E
    d = x.shape[-1]

    def rope_out(o_ref, mult):
        def epilogue(z, sl):
            for sub in range(MXU_COLS // LANES):
                src = slice(sub * LANES, (sub + 1) * LANES)
                dst = slice(sl.start + sub * LANES, sl.start + (sub + 1) * LANES)
                r = rope(z[:, src])
                o_ref[0, :, dst] = (r if mult is None else r * mult).astype(o_ref.dtype)
        return epilogue

    def plain_out(o_ref):
        def epilogue(z, sl):
            o_ref[0, :, sl] = z.astype(o_ref.dtype)
        return epilogue

    def silu_out(o_ref):
        def epilogue(z, sl):
            o_ref[0, :, sl] = (z * jax.nn.sigmoid(z)).astype(o_ref.dtype)
        return epilogue

    def gate_out(lb_ref, hi_ref, lo_ref, key_ref):
        def epilogue(f, sl):
            lb = lb_ref[:, sl]
            one_m_lb = 1.0 - lb
            g = lb + one_m_lb * jax.nn.sigmoid(f)
            lg = jnp.log(jnp.maximum(g, GATE_FLOOR)) * LOG2E
            hi = lg.astype(BF16)
            hi_ref[0, :, sl] = hi
            lo_ref[0, :, sl] = (lg - hi.astype(F32)).astype(BF16)
            key_ref[0, :, sl] = (one_m_lb * jax.nn.sigmoid(-f)).astype(BF16)
        return epilogue

    segments = [(NA_WIDTH, rope_out(q_ref, scale), True), (NA_WIDTH, rope_out(k_ref, None), True),
                (NA_WIDTH, plain_out(v_ref), False), (HG_WIDTH, silu_out(hq_ref), True),
                (HG_WIDTH, gate_out(lbf_ref, lfh_ref, lfl_ref, kf_ref), True),
                (HG_WIDTH, gate_out(lbb_ref, lbh_ref, lbl_ref, kb_ref), True),
                (HG_WIDTH, plain_out(hi_ref), False), (HG_WIDTH, plain_out(og_ref), False),
                (d, plain_out(ga_ref), False), (d, plain_out(gh_ref), False)]
    heavy, light, off = [], [], 0
    for width, epilogue, is_heavy in segments:
        for c in range(0, width, MXU_COLS):
            (heavy if is_heavy else light).append((off + c, slice(c, c + MXU_COLS), epilogue))
        off += width
    order = []
    while heavy or light:
        if heavy:
            order.append(heavy.pop(0))
        order.extend(light[:1])
        del light[:1]
    z = proj(order[0][0], MXU_COLS)
    for i, (_, sl, epilogue) in enumerate(order):
        z_next = proj(order[i + 1][0], MXU_COLS) if i + 1 < len(order) else None
        epilogue(z, sl)
        z = z_next


def _stream_specs(first, rest, n_first):
    d = first.shape[-1]
    return (pl.BlockSpec((1, ROW_TILE, d), lambda bi, j: (bi, 0, 0)),
            pl.BlockSpec((1, ROW_TILE, d), lambda bi, j: (bi, jnp.maximum(j - n_first, 0), 0)))


def _inproj(stream, lt, mods, g, w_all, layer, cos_t, sin_t, lb_f, lb_b, ctx_row):
    first, rest, n_first = stream
    b, _, d = first.shape
    n_t = lt // ROW_TILE

    def rows(width):
        return pl.BlockSpec((1, ROW_TILE, width), lambda bi, j: (bi, j, 0))

    def mod_map(bi, j):
        return (jnp.where(j == 0, ctx_row, bi), 0, 0)

    widths = [NA_WIDTH] * 3 + [HG_WIDTH] * 9 + [d, d]
    return pl.pallas_call(
        _inproj_kernel,
        grid=(b, n_t),
        in_specs=[*_stream_specs(first, rest, n_first),
                  pl.BlockSpec((1, 6, d), mod_map),
                  pl.BlockSpec((1, d), lambda bi, j: (0, 0)),
                  pl.BlockSpec((1,) + w_all.shape[1:], lambda bi, j: (layer, 0, 0)),
                  pl.BlockSpec((ROW_TILE, LANES), lambda bi, j: (j, 0)),
                  pl.BlockSpec((ROW_TILE, LANES), lambda bi, j: (j, 0)),
                  pl.BlockSpec((1, HG_WIDTH), lambda bi, j: (0, 0)),
                  pl.BlockSpec((1, HG_WIDTH), lambda bi, j: (0, 0))],
        out_specs=[rows(w) for w in widths],
        out_shape=[jax.ShapeDtypeStruct((b, lt, w), BF16) for w in widths],
        compiler_params=_cparams(("arbitrary", "arbitrary")),
        name="inproj",
    )(first, rest, mods, g, w_all, cos_t, sin_t, lb_f.reshape(1, HG_WIDTH), lb_b.reshape(1, HG_WIDTH))


def _rope_tables(n_ctx, seq):
    t = jnp.arange(seq, dtype=jnp.int32)
    row = (t // GRID_W).astype(F32)
    col = (t % GRID_W).astype(F32)
    rot_half = NA_HEAD_DIM // 2
    inv = ROPE_THETA ** (-jnp.arange(0, rot_half, 2, dtype=F32) / rot_half)
    ang = jnp.concatenate([row[:, None] * inv, col[:, None] * inv], axis=-1)
    cos = jnp.repeat(jnp.cos(ang), 2, axis=-1)
    sin = jnp.repeat(jnp.sin(ang), 2, axis=-1)
    sign = jnp.asarray(np.tile(np.array([-1.0, 1.0], np.float32), NA_HEAD_DIM // 2))
    sin = sin * sign
    cos = jnp.concatenate([jnp.ones((n_ctx, NA_HEAD_DIM), F32), cos], axis=0)
    sin = jnp.concatenate([jnp.zeros((n_ctx, NA_HEAD_DIM), F32), sin], axis=0)
    reps = LANES // NA_HEAD_DIM
    return jnp.tile(cos, (1, reps)), jnp.tile(sin, (1, reps))


def _na_plan(n_ctx, grid_rows):
    span = NA_WIN_ROWS + NA_Q_ROWS - 1
    variants, var_of_step, ws_of_step = [], [], []
    for r0 in range(0, grid_rows, NA_Q_ROWS):
        rs = [min(max(r0 + dq - NA_WIN_ROWS // 2, 0), grid_rows - NA_WIN_ROWS) for dq in range(NA_Q_ROWS)]
        ws = min(rs[0], grid_rows - span)
        key = (r0 - ws,) + tuple(r - ws for r in rs)
        if key not in variants:
            variants.append(key)
        var_of_step.append(variants.index(key))
        ws_of_step.append(ws)
    ctx_steps = n_ctx // (NA_Q_ROWS * GRID_W)
    var_of_step = [len(variants)] * ctx_steps + var_of_step
    ws_of_step = [0] * ctx_steps + ws_of_step
    return variants, np.asarray(var_of_step, np.int32), np.asarray(ws_of_step, np.int32)


def _na_bias_planes(rel_bias, variants):
    span = NA_WIN_ROWS + NA_Q_ROWS - 1
    n_dr, n_dc = 2 * NA_WIN_ROWS - 1, 2 * NA_WIN_COLS - 1
    col = np.arange(GRID_W)[:, None]
    kc = np.arange(GRID_W)[None, :]
    cs = np.clip(col - NA_WIN_COLS // 2, 0, GRID_W - NA_WIN_COLS)
    col_ok = (kc >= cs) & (kc < cs + NA_WIN_COLS)
    oh_c = ((kc - col + NA_WIN_COLS - 1)[..., None] == np.arange(n_dc)) & col_ok[..., None]
    planes = jnp.einsum("hrc,xkc->hrxk", rel_bias.astype(F32), jnp.asarray(oh_c.astype(np.float32)),
                        precision=lax.Precision.HIGHEST)
    planes = jnp.where(jnp.asarray(col_ok), planes * LOG2E, MASKED)
    planes = jnp.concatenate([planes, jnp.full_like(planes[:, :1], MASKED)], axis=1)
    plane_of = np.full((len(variants) + 1, NA_Q_ROWS, span), n_dr, np.int32)
    for vi, key in enumerate(variants):
        for dq in range(NA_Q_ROWS):
            for j in range(key[1 + dq], key[1 + dq] + NA_WIN_ROWS):
                plane_of[vi, dq, j] = j - (key[0] + dq) + NA_WIN_ROWS - 1
    return planes, plane_of.reshape(-1)


def _na_kernel(var_ref, ws_ref, plane_ref, q_ref, k_ref, v_ref, b_ref, o_ref, sw_ref, sc_ref, t_ref,
               *, step_off, n_ctx):
    nq = q_ref.shape[1]
    win = sw_ref.shape[1]
    span = win // GRID_W
    j = pl.program_id(1) + step_off
    var = var_ref[j]

    @pl.when((pl.program_id(1) == 0) | (var != var_ref[jnp.maximum(j - 1, 0)]))
    def _():
        for h in range(NA_HEADS):
            for dq in range(nq // GRID_W):
                base = (var * (nq // GRID_W) + dq) * span
                row = jnp.concatenate([b_ref[h, plane_ref[base + jj]] for jj in range(span)], axis=1)
                t_ref[h, dq * GRID_W:(dq + 1) * GRID_W, :] = row

    start = pl.multiple_of(n_ctx + ws_ref[j] * GRID_W, GRID_W)
    low = lax.broadcasted_iota(jnp.int32, (nq, LANES), 1) < NA_HEAD_DIM
    for hp in range(NA_HEADS // 2):
        sl = slice(hp * LANES, (hp + 1) * LANES)
        q2 = q_ref[0, :, sl]
        kw = k_ref[0, pl.ds(start, win), sl]
        kc = k_ref[0, 0:n_ctx, sl]
        zero = jnp.zeros_like(q2)
        qs = jnp.concatenate([jnp.where(low, q2, zero), jnp.where(low, zero, q2)], axis=0)
        rows = slice(2 * hp * nq, (2 * hp + 2) * nq)
        bias = t_ref[2 * hp:2 * hp + 2].reshape(2 * nq, win)
        sw_ref[rows, :] = lax.dot_general(qs, kw, _NT, preferred_element_type=F32) + bias
        sc_ref[rows, :] = lax.dot_general(qs, kc, _NT, preferred_element_type=F32)
    sw = sw_ref[...]
    sc = sc_ref[...]
    m = jnp.maximum(jnp.max(sw, axis=-1, keepdims=True), jnp.max(sc, axis=-1, keepdims=True))
    pw = jnp.exp2(sw - m)
    pc = jnp.exp2(sc - m)
    inv = 1.0 / (jnp.sum(pw, axis=-1, keepdims=True) + jnp.sum(pc, axis=-1, keepdims=True))
    pw = pw.astype(BF16)
    pc = pc.astype(BF16)
    for hp in range(NA_HEADS // 2):
        sl = slice(hp * LANES, (hp + 1) * LANES)
        vw = v_ref[0, pl.ds(start, win), sl]
        vc = v_ref[0, 0:n_ctx, sl]
        rows = slice(2 * hp * nq, (2 * hp + 2) * nq)
        o2 = (jnp.dot(pw[rows], vw, preferred_element_type=F32)
              + jnp.dot(pc[rows], vc, preferred_element_type=F32)) * inv[rows]
        o_ref[0, :, sl] = jnp.where(low, o2[:nq], o2[nq:]).astype(o_ref.dtype)


def _na_attention(q, k, v, rel_bias, n_ctx, with_ctx):
    b, lt, _ = q.shape
    nq = NA_Q_ROWS * GRID_W
    span = NA_WIN_ROWS + NA_Q_ROWS - 1
    variants, var_of_step, ws_of_step = _na_plan(n_ctx, (lt - n_ctx) // GRID_W)
    planes, plane_of = _na_bias_planes(rel_bias, variants)
    step_off = 0 if with_ctx else n_ctx // nq
    n_steps = lt // nq - step_off

    kern = functools.partial(_na_kernel, step_off=step_off, n_ctx=n_ctx)
    grid_spec = pltpu.PrefetchScalarGridSpec(
        num_scalar_prefetch=3,
        grid=(b, n_steps),
        in_specs=[pl.BlockSpec((1, nq, NA_WIDTH), lambda bi, j, *_: (bi, j + step_off, 0)),
                  pl.BlockSpec((1, lt, NA_WIDTH), lambda bi, j, *_: (bi, 0, 0)),
                  pl.BlockSpec((1, lt, NA_WIDTH), lambda bi, j, *_: (bi, 0, 0)),
                  pl.BlockSpec(planes.shape, lambda bi, j, *_: (0, 0, 0, 0))],
        out_specs=pl.BlockSpec((1, nq, NA_WIDTH), lambda bi, j, *_: (bi, j, 0)),
        scratch_shapes=[pltpu.VMEM((NA_HEADS * nq, span * GRID_W), F32),
                        pltpu.VMEM((NA_HEADS * nq, n_ctx), F32),
                        pltpu.VMEM((NA_HEADS, nq, span * GRID_W), F32)],
    )
    return pl.pallas_call(
        kern,
        grid_spec=grid_spec,
        out_shape=jax.ShapeDtypeStruct((b, n_steps * nq, NA_WIDTH), BF16),
        compiler_params=_cparams(("arbitrary", "arbitrary")),
        name="na_attention",
    )(jnp.asarray(var_of_step), jnp.asarray(ws_of_step), jnp.asarray(plane_of), q, k, v, planes)


def _block_ref_rows(b, w, rev, upper_fill=None, lower_fill=None):
    c, n = b.shape
    off = w if rev else w - 1
    if upper_fill is not None or lower_fill is not None:
        parts = []
        for s in range(0, c, 2 * w):
            ref = jnp.broadcast_to(b[s + off:s + off + 1], (w, n))
            parts.append(ref if lower_fill is None else jnp.full((w, n), lower_fill, b.dtype))
            parts.append(ref if upper_fill is None else jnp.full((w, n), upper_fill, b.dtype))
        return jnp.concatenate(parts, axis=0)
    if 2 * w >= SUBLANES:
        parts = [jnp.broadcast_to(b[s + off:s + off + 1], (2 * w, n)) for s in range(0, c, 2 * w)]
        return parts[0] if len(parts) == 1 else jnp.concatenate(parts, axis=0)
    b3 = b.reshape(c // SUBLANES, SUBLANES, n)
    sub = lax.broadcasted_iota(jnp.int32, b3.shape, 1)
    r = None
    for s in range(0, SUBLANES, 2 * w):
        piece = jnp.broadcast_to(b3[:, s + off:s + off + 1, :], b3.shape)
        r = piece if r is None else jnp.where(sub >= s, piece, r)
    return r.reshape(c, n)


def _hgrn_chunks(chains, states, lmask_ref):
    n = len(chains)
    c = chains[0][0].shape[0]
    qb = [ch[0] for ch in chains]
    kb = [ch[2] for ch in chains]
    b = []
    for i in range(n):
        (hi, lo), tri = chains[i][1], chains[i][4]
        b.append(jnp.dot(tri, hi, preferred_element_type=F32) + jnp.dot(tri, lo, preferred_element_type=F32))

    states = list(states)
    o = []
    for i in range(n):
        vb, rev, sid = chains[i][3], chains[i][7], chains[i][8]
        b_last = b[i][0:1] if rev else b[i][c - 1:c]
        qe = qb[i] * jnp.exp2(b[i]).astype(BF16)
        ke = kb[i] * jnp.exp2(b_last - b[i]).astype(BF16)
        st = states[sid]
        o.append(lax.dot_general(qe, st.astype(BF16), _NT, preferred_element_type=F32))
        states[sid] = st * jnp.exp2(b_last) + lax.dot_general(vb, ke, _TN, preferred_element_type=F32)

    a = [None] * n
    w = c // 2
    for li in range(lmask_ref.shape[0]):
        for i in range(n):
            negq_ref, negk_ref, rev = chains[i][5], chains[i][6], chains[i][7]
            if w >= SUBLANES:
                eq = b[i] - _block_ref_rows(b[i], w, rev, -MASKED if rev else None, None if rev else -MASKED)
                ek = _block_ref_rows(b[i], w, rev, None if rev else MASKED, MASKED if rev else None) - b[i]
                kw = kb[i] * jnp.exp2(ek).astype(BF16)
            elif w > 1:
                d = b[i] - _block_ref_rows(b[i], w, rev)
                eq = d + negq_ref[li]
                kw = kb[i] * jnp.exp2(negk_ref[li] - d).astype(BF16)
            else:
                hi, lo = chains[i][1]
                eq = hi.astype(F32) + lo.astype(F32) + negq_ref[li]
                kw = kb[i] * chains[i][9]
            qw = qb[i] * jnp.exp2(eq).astype(BF16)
            p = lax.dot_general(qw, kw, _NT, preferred_element_type=F32).astype(BF16) * lmask_ref[li]
            a[i] = p if a[i] is None else a[i] + p
        w //= 2
    for i in range(n):
        vb = chains[i][3]
        o[i] = o[i] + jnp.dot(a[i], vb, preferred_element_type=F32)
        diag = jnp.sum(qb[i].astype(F32) * kb[i].astype(F32), axis=-1, keepdims=True)
        o[i] = o[i] + diag * vb.astype(F32)
    return o, states


def _hgrn_kernel(hq_ref, hi_ref, lfh_ref, lfl_ref, kf_ref, lbh_ref, lbl_ref, kb_ref, trif_ref, trir_ref,
                 negqf_ref, negkf_ref, negqr_ref, negkr_ref, lmask_ref, konef_ref, koner_ref,
                 o_ref, sf_ref, sb_ref, of_ref, ob_ref, *, n_ctx_chunks, n_chunks):
    sf_ref[...] = jnp.zeros_like(sf_ref)
    sb_ref[...] = jnp.zeros_like(sb_ref)

    def steps(cf0, cb0, unroll):
        chains, rows = [], []
        for u in range(unroll):
            rf = pl.ds(pl.multiple_of((cf0 + u) * HG_CHUNK, HG_CHUNK), HG_CHUNK)
            rb = pl.ds(pl.multiple_of((cb0 - u) * HG_CHUNK, HG_CHUNK), HG_CHUNK)
            chains.append((hq_ref[0, rf, :], (lfh_ref[0, rf, :], lfl_ref[0, rf, :]), kf_ref[0, rf, :],
                           hi_ref[0, rf, :], trif_ref[...], negqf_ref, negkf_ref, False, 0, konef_ref[...]))
            chains.append((hq_ref[0, rb, :], (lbh_ref[0, rb, :], lbl_ref[0, rb, :]), kb_ref[0, rb, :],
                           hi_ref[0, rb, :], trir_ref[...], negqr_ref, negkr_ref, True, 1, koner_ref[...]))
            rows += [rf, rb]
        outs, (sf, sb) = _hgrn_chunks(chains, [sf_ref[...], sb_ref[...]], lmask_ref)
        for i, r in enumerate(rows):
            if i % 2 == 0:
                of_ref[r, :] = outs[i]
            else:
                ob_ref[r, :] = outs[i]
        sf_ref[...] = sf
        sb_ref[...] = sb

    n_lat = n_chunks - n_ctx_chunks
    u_ctx = math.gcd(HG_UNROLL, n_ctx_chunks)
    u_lat = math.gcd(HG_UNROLL, n_lat)

    def ctx_body(i, carry):
        steps(i * u_ctx, n_ctx_chunks - 1 - i * u_ctx, u_ctx)
        return carry

    def lat_body(i, carry):
        steps(n_ctx_chunks + i * u_lat, n_chunks - 1 - i * u_lat, u_lat)
        return carry

    lax.fori_loop(0, n_ctx_chunks // u_ctx, ctx_body, 0)
    lax.fori_loop(0, n_lat // u_lat, lat_body, 0)
    o_ref[0] = (of_ref[...] + ob_ref[...]).astype(o_ref.dtype)


def _hgrn_level_constants():
    c = HG_CHUNK
    t = np.arange(c)
    xor = t[:, None] ^ t[None, :]
    lmask, negq_f, negq_r = [], [], []
    w = c // 2
    while w >= 1:
        lmask.append(((xor >= w) & (xor < 2 * w)).astype(np.float32))
        upper = (t % (2 * w)) >= w
        negq_f.append(np.where(upper, 0.0, MASKED))
        negq_r.append(np.where(upper, MASKED, 0.0))
        w //= 2

    def rows(m):
        return jnp.asarray(np.broadcast_to(np.stack(m)[:, :, None], (len(m), c, HG_DIM)).astype(np.float32))

    tri_f = (t[None, :] <= t[:, None]).astype(np.float32)
    odd = np.broadcast_to((t % 2 == 1)[:, None], (c, HG_DIM)).astype(np.float32)
    return (jnp.asarray(tri_f, BF16), jnp.asarray(tri_f.T, BF16),
            rows(negq_f), rows(negq_r), rows(negq_r), rows(negq_f), jnp.asarray(np.stack(lmask), BF16),
            jnp.asarray(1.0 - odd, BF16), jnp.asarray(odd, BF16))


def _hgrn(hq, hi, fwd, bwd, n_ctx):
    b, lt, _ = hq.shape
    kern = functools.partial(_hgrn_kernel, n_ctx_chunks=n_ctx // HG_CHUNK, n_chunks=lt // HG_CHUNK)
    seq = pl.BlockSpec((1, lt, HG_DIM), lambda bi, h: (bi, 0, h))
    consts = _hgrn_level_constants()
    return pl.pallas_call(
        kern,
        grid=(b, HG_HEADS),
        in_specs=[seq] * 8
        + [pl.BlockSpec(a.shape, lambda bi, h, nd=a.ndim: (0,) * nd) for a in consts],
        out_specs=seq,
        out_shape=jax.ShapeDtypeStruct((b, lt, HG_WIDTH), BF16),
        scratch_shapes=[pltpu.VMEM((HG_DIM, HG_DIM), F32), pltpu.VMEM((HG_DIM, HG_DIM), F32),
                        pltpu.VMEM((lt, HG_DIM), F32), pltpu.VMEM((lt, HG_DIM), F32)],
        compiler_params=_cparams(("arbitrary", "arbitrary")),
        name="hgrn2",
    )(hq, hi, *fwd, *bwd, *consts)


def _split2(x):
    hi = x.astype(BF16)
    lo = (x - hi.astype(F32)).astype(BF16)
    return hi, lo


def _route(logits, bias):
    biased = jax.nn.sigmoid(logits) + bias
    rows = [biased[e:e + 1] for e in range(N_EXPERTS)]
    n = EXPERTS_PER_GROUP
    best = None
    g_sel = None
    for g in range(N_GROUPS):
        gs = None
        for i in range(n):
            for j2 in range(i + 1, n):
                pair = rows[g * n + i] + rows[g * n + j2]
                gs = pair if gs is None else jnp.maximum(gs, pair)
        if best is None:
            best, g_sel = gs, jnp.zeros(gs.shape, jnp.int32)
        else:
            take = gs > best
            best = jnp.where(take, gs, best)
            g_sel = jnp.where(take, g, g_sel)
    cand = []
    for i in range(n):
        c_i = rows[(N_GROUPS - 1) * n + i]
        for g in range(N_GROUPS - 2, -1, -1):
            c_i = jnp.where(g_sel == g, rows[g * n + i], c_i)
        cand.append(c_i)
    m1, i1 = cand[0], jnp.zeros(best.shape, jnp.int32)
    for i in range(1, n):
        take = cand[i] > m1
        m1 = jnp.where(take, cand[i], m1)
        i1 = jnp.where(take, i, i1)
    m2 = jnp.full(best.shape, -jnp.inf, F32)
    i2 = jnp.zeros(best.shape, jnp.int32)
    for i in range(n):
        take = (i1 != i) & (cand[i] > m2)
        m2 = jnp.where(take, cand[i], m2)
        i2 = jnp.where(take, i, i2)
    lo = jnp.minimum(i1, i2)
    hi = jnp.maximum(i1, i2)
    pair = jnp.where(lo == 0, hi - 1, jnp.where(lo == 1, hi + 1, N_PAIRS - 1))
    return g_sel * N_PAIRS + pair


def _merge_kernel(ona_ref, ohg_ref, og_ref, ga_ref, gh_ref, x0_ref, x_ref, mod_ref, hgg_ref, n2g_ref,
                  wna_ref, whg_ref, wout_ref, wr_ref, rb_ref,
                  xn_ref, h2_ref, bucket_ref, rank_ref, cnt_ref, carry_ref, hprev_ref,
                  *, n_t, n_tot, skip):
    step = pl.program_id(0)
    first_tile = (jnp.minimum(step, n_tot - 1) % n_t + skip) == 0

    @pl.when(step == 0)
    def _():
        carry_ref[...] = jnp.zeros_like(carry_ref)
        hprev_ref[...] = jnp.zeros_like(hprev_ref)

    w_hi, w_lo = _split2(wr_ref[...])
    h_hi, h_lo = _split2(hprev_ref[...])
    logits = (lax.dot_general(w_hi, h_hi, _NT, preferred_element_type=F32)
              + lax.dot_general(w_hi, h_lo, _NT, preferred_element_type=F32)
              + lax.dot_general(w_lo, h_hi, _NT, preferred_element_type=F32))

    ohg = ohg_ref[0].astype(F32)
    og = og_ref[0].astype(F32)
    gain = hgg_ref[...]
    heads = []
    for h in range(HG_HEADS):
        sl = slice(h * HG_DIM, (h + 1) * HG_DIM)
        oh = ohg[:, sl]
        yh = oh * lax.rsqrt(jnp.mean(oh * oh, axis=-1, keepdims=True) + EPS) * gain
        gt = og[:, sl]
        heads.append((yh * (gt * jax.nn.sigmoid(gt))).astype(BF16))
    hn = jnp.concatenate(heads, axis=-1)
    y_na = jnp.dot(ona_ref[0], wna_ref[0], preferred_element_type=F32)
    y_hg = jnp.dot(hn, whg_ref[0], preferred_element_type=F32)

    bucket = _route(logits, rb_ref[...])
    bucket_ref[0] = bucket

    m = jax.nn.sigmoid(ga_ref[0].astype(F32)) * y_na + jax.nn.sigmoid(gh_ref[0].astype(F32)) * y_hg
    y = jnp.dot(m.astype(BF16), wout_ref[0], preferred_element_type=F32)

    t = bucket.shape[1]
    onehot = (lax.broadcasted_iota(jnp.int32, (BUCKET_ROWS, t), 0) == bucket).astype(F32)
    before = (lax.broadcasted_iota(jnp.int32, (t, t), 0)
              < lax.broadcasted_iota(jnp.int32, (t, t), 1)).astype(BF16)
    prefix = jnp.dot(onehot.astype(BF16), before, preferred_element_type=F32)
    carry = carry_ref[...]
    rank = jnp.sum(onehot * (prefix + carry[:, 0:1]), axis=0, keepdims=True)
    rank_ref[0] = rank.astype(jnp.int32)
    live = (step > 0).astype(F32)
    carry = carry + live * jnp.sum(onehot, axis=1, keepdims=True)
    carry_ref[...] = carry
    cnt_ref[...] = carry.astype(jnp.int32)

    xn = jnp.where(first_tile, x0_ref[0], x_ref[0]) + mod_ref[0, 2:3, :] * y
    xn_ref[0] = xn
    yn = xn * lax.rsqrt(jnp.mean(xn * xn, axis=-1, keepdims=True) + EPS) * n2g_ref[...]
    h2 = yn * (1.0 + mod_ref[0, 4:5, :]) + mod_ref[0, 3:4, :]
    h2_ref[...] = h2
    hprev_ref[...] = h2


def _merge(o_na, o_hg, og, ga, gh, stream, mods, hg_gain, n2_gain, w_na_o, w_hg_o, w_out, layer,
           w_router_t, router_bias, ctx_row, skip):
    first, rest, n_first = stream
    b, lt, _ = o_hg.shape
    d = first.shape[-1]
    n_t = lt // ROW_TILE - skip
    n_tot = b * n_t

    def tile(s):
        t = jnp.minimum(s, n_tot - 1)
        return t // n_t, t % n_t

    def rows(width):
        return pl.BlockSpec((1, ROW_TILE, width), lambda s: (tile(s)[0], tile(s)[1] + skip, 0))

    def full(a):
        return pl.BlockSpec(a.shape, lambda s: (0,) * a.ndim)

    def of_layer(a):
        return pl.BlockSpec((1,) + a.shape[1:], lambda s: (layer,) + (0,) * (a.ndim - 1))

    def mod_map(s):
        bi, j = tile(s)
        return (jnp.where(j + skip == 0, ctx_row, bi), 0, 0)

    tok = pl.BlockSpec((1, 1, ROW_TILE), lambda s: (jnp.maximum(s - 1, 0), 0, 0))
    consts = [hg_gain, n2_gain, w_na_o, w_hg_o, w_out, w_router_t, router_bias]
    const_specs = [full(hg_gain), full(n2_gain), of_layer(w_na_o), of_layer(w_hg_o), of_layer(w_out),
                   full(w_router_t), full(router_bias)]
    kern = functools.partial(_merge_kernel, n_t=n_t, n_tot=n_tot, skip=skip)
    return pl.pallas_call(
        kern,
        grid=(n_tot + 1,),
        in_specs=[pl.BlockSpec((1, ROW_TILE, NA_WIDTH), lambda s: tile(s) + (0,)),
                  rows(HG_WIDTH), rows(HG_WIDTH), rows(d), rows(d),
                  pl.BlockSpec((1, ROW_TILE, d), lambda s: (tile(s)[0], 0, 0)),
                  pl.BlockSpec((1, ROW_TILE, d),
                               lambda s: (tile(s)[0], jnp.maximum(tile(s)[1] + skip - n_first, 0), 0)),
                  pl.BlockSpec((1, 6, d), mod_map)] + const_specs,
        out_specs=[pl.BlockSpec((1, ROW_TILE, d), lambda s: tile(s) + (0,)),
                   pl.BlockSpec((ROW_TILE, d), lambda s: (jnp.minimum(s, n_tot - 1), 0)),
                   tok, tok,
                   pl.BlockSpec((BUCKET_ROWS, LANES), lambda s: (0, 0))],
        out_shape=[jax.ShapeDtypeStruct((b, n_t * ROW_TILE, d), F32),
                   jax.ShapeDtypeStruct((b * n_t * ROW_TILE, d), F32),
                   jax.ShapeDtypeStruct((b * n_t, 1, ROW_TILE), jnp.int32),
                   jax.ShapeDtypeStruct((b * n_t, 1, ROW_TILE), jnp.int32),
                   jax.ShapeDtypeStruct((BUCKET_ROWS, LANES), jnp.int32)],
        scratch_shapes=[pltpu.VMEM((BUCKET_ROWS, LANES), F32), pltpu.VMEM((ROW_TILE, d), F32)],
        compiler_params=_cparams(("arbitrary",)),
        name="merge_router",
    )(o_na, o_hg, og, ga, gh, first, rest, mods, *consts)


def _start_row_copies(n, row_copy):
    for r in range(n):
        row_copy(r).start()


def _scatter_kernel(dest_ref, h_ref, xs_ref, zero_ref, sem, *, n_token_steps):
    n = h_ref.shape[0]
    step = pl.program_id(0)

    @pl.when(step < n_token_steps)
    def _():
        _start_row_copies(n, lambda r: pltpu.make_async_copy(
            h_ref.at[pl.ds(r, 1)], xs_ref.at[pl.ds(dest_ref[0, 0, r], 1)], sem))

    @pl.when(step >= n_token_steps)
    def _():
        zero_ref[...] = jnp.zeros_like(zero_ref)
        _start_row_copies(n, lambda r: pltpu.make_async_copy(
            zero_ref.at[pl.ds(0, 1)], xs_ref.at[pl.ds(dest_ref[0, 0, r], 1)], sem))

    pltpu.make_async_copy(h_ref, xs_ref.at[pl.ds(0, n)], sem).wait()


def _scatter_rows(h2, dest, pad_pos):
    t, d = h2.shape
    tile = 2 * ROW_TILE if t % (2 * ROW_TILE) == 0 and pad_pos.shape[0] % (2 * ROW_TILE) == 0 else ROW_TILE
    n_t = t // tile
    n_steps = n_t + pad_pos.shape[0] // tile
    kern = functools.partial(_scatter_kernel, n_token_steps=n_t)
    return pl.pallas_call(
        kern,
        grid=(n_steps,),
        in_specs=[pl.BlockSpec((1, 1, tile), lambda i: (i, 0, 0), memory_space=pltpu.SMEM),
                  pl.BlockSpec((tile, d), lambda i: (jnp.minimum(i, n_t - 1), 0))],
        out_specs=pl.BlockSpec(memory_space=pl.ANY),
        out_shape=jax.ShapeDtypeStruct((n_steps * tile, d), F32),
        scratch_shapes=[pltpu.VMEM((SUBLANES, d), F32), pltpu.SemaphoreType.DMA(())],
        compiler_params=_cparams(("arbitrary",)),
        name="scatter_rows",
    )(jnp.concatenate([dest, pad_pos]).reshape(n_steps, 1, tile), h2)


def _final_kernel(dest_ref, dest_next_ref, xn_ref, mod_ref, g_ref, ys_ref, o_ref, buf_ref, sems,
                  *, final_norm):
    n = buf_ref.shape[1]
    step = pl.program_id(0) * pl.num_programs(1) + pl.program_id(1)
    n_steps = pl.num_programs(0) * pl.num_programs(1)

    def gather(d_ref, slot):
        _start_row_copies(n, lambda r: pltpu.make_async_copy(
            ys_ref.at[pl.ds(d_ref[0, 0, r], 1)], buf_ref.at[slot, pl.ds(r, 1)], sems.at[slot]))

    @pl.when(step == 0)
    def _():
        gather(dest_ref, 0)

    @pl.when(step + 1 < n_steps)
    def _():
        gather(dest_next_ref, (step + 1) % 2)

    slot = step % 2
    pltpu.make_async_copy(ys_ref.at[pl.ds(0, n)], buf_ref.at[slot], sems.at[slot]).wait()
    x = xn_ref[0] + mod_ref[0, 5:6, :] * buf_ref[slot]
    if final_norm:
        x = x * lax.rsqrt(jnp.mean(x * x, axis=-1, keepdims=True) + EPS) * g_ref[...]
    o_ref[0] = x


def _gather_residual(xn, mods, ys, dest, gain, ctx_row, final_norm, skip):
    b, lt, d = xn.shape
    n_t = lt // ROW_TILE

    def mod_map(bi, j):
        return (jnp.where(j + skip == 0, ctx_row, bi), 0, 0)

    kern = functools.partial(_final_kernel, final_norm=final_norm)
    last_tile = b * n_t - 1
    dest3 = dest.reshape(b * n_t, 1, ROW_TILE)
    return pl.pallas_call(
        kern,
        grid=(b, n_t),
        in_specs=[pl.BlockSpec((1, 1, ROW_TILE), lambda bi, j: (bi * n_t + j, 0, 0),
                               memory_space=pltpu.SMEM),
                  pl.BlockSpec((1, 1, ROW_TILE), lambda bi, j: (jnp.minimum(bi * n_t + j + 1, last_tile), 0, 0),
                               memory_space=pltpu.SMEM),
                  pl.BlockSpec((1, ROW_TILE, d), lambda bi, j: (bi, j, 0)),
                  pl.BlockSpec((1, 6, d), mod_map),
                  pl.BlockSpec((1, d), lambda bi, j: (0, 0)),
                  pl.BlockSpec(memory_space=pl.ANY)],
        out_specs=pl.BlockSpec((1, ROW_TILE, d), lambda bi, j: (bi, j, 0)),
        out_shape=jax.ShapeDtypeStruct((b, lt, d), F32),
        scratch_shapes=[pltpu.VMEM((2, ROW_TILE, d), F32), pltpu.SemaphoreType.DMA((2,))],
        compiler_params=_cparams(("arbitrary", "arbitrary")),
        name="gather_residual",
    )(dest3, dest3, xn, mods, gain, ys)


def _moe_kernel(ea_ref, eb_ref, nused_ref, xs_ref, wra_ref, wrb_ref,
                wga_ref, wua_ref, wda_ref, wgb_ref, wub_ref, wdb_ref, ys_ref):
    del ea_ref, eb_ref
    i = pl.program_id(0)

    @pl.when(i < nused_ref[0])
    def _():
        x = xs_ref[...]
        s_a = jax.nn.sigmoid(jnp.sum(x * wra_ref[0], axis=-1, keepdims=True))
        s_b = jax.nn.sigmoid(jnp.sum(x * wrb_ref[0], axis=-1, keepdims=True))
        tot = s_a + s_b
        xb = x.astype(BF16)

        gate_a = jnp.dot(xb, wga_ref[0], preferred_element_type=F32)
        up_a = jnp.dot(xb, wua_ref[0], preferred_element_type=F32)
        gate_b = jnp.dot(xb, wgb_ref[0], preferred_element_type=F32)
        hid_a = (gate_a * jax.nn.sigmoid(gate_a) * up_a).astype(BF16)
        up_b = jnp.dot(xb, wub_ref[0], preferred_element_type=F32)
        y_a = jnp.dot(hid_a, wda_ref[0], preferred_element_type=F32)
        hid_b = (gate_b * jax.nn.sigmoid(gate_b) * up_b).astype(BF16)
        y_b = jnp.dot(hid_b, wdb_ref[0], preferred_element_type=F32)
        ys_ref[...] = (s_a / tot) * y_a + (s_b / tot) * y_b

    @pl.when(i >= nused_ref[0])
    def _():
        ys_ref[...] = jnp.zeros_like(ys_ref)


def _moe(xs, tile_ea, tile_eb, n_used, w_router_rows, w_gate, w_up, w_down, layer):
    n_sorted, d = xs.shape
    n_tiles = n_sorted // ROW_TILE
    ff = w_gate.shape[-1]
    base = layer * N_EXPERTS

    def by_a(i, ea, eb, nu):
        return (ea[i], 0, 0)

    def by_b(i, ea, eb, nu):
        return (eb[i], 0, 0)

    def wt_a(i, ea, eb, nu):
        return (base + ea[i], 0, 0)

    def wt_b(i, ea, eb, nu):
        return (base + eb[i], 0, 0)

    grid_spec = pltpu.PrefetchScalarGridSpec(
        num_scalar_prefetch=3,
        grid=(n_tiles,),
        in_specs=[pl.BlockSpec((ROW_TILE, d), lambda i, ea, eb, nu: (i, 0)),
                  pl.BlockSpec((1, 1, d), by_a), pl.BlockSpec((1, 1, d), by_b),
                  pl.BlockSpec((1, d, ff), wt_a), pl.BlockSpec((1, d, ff), wt_a),
                  pl.BlockSpec((1, ff, d), wt_a),
                  pl.BlockSpec((1, d, ff), wt_b), pl.BlockSpec((1, d, ff), wt_b),
                  pl.BlockSpec((1, ff, d), wt_b)],
        out_specs=pl.BlockSpec((ROW_TILE, d), lambda i, ea, eb, nu: (i, 0)),
    )
    return pl.pallas_call(
        _moe_kernel,
        grid_spec=grid_spec,
        out_shape=jax.ShapeDtypeStruct((n_sorted, d), F32),
        compiler_params=_cparams(("arbitrary",)),
        name="moe_pairs",
    )(tile_ea, tile_eb, n_used, xs, w_router_rows, w_router_rows,
      w_gate, w_up, w_down, w_gate, w_up, w_down)


_PAIR_LO = np.array([0, 0, 0, 1, 1, 2], np.int32)
_PAIR_HI = np.array([1, 2, 3, 2, 3, 3], np.int32)


def _sorted_layout(bucket, rank, counts, n_tiles):
    counts = counts[:N_BUCKETS]
    padded = ((counts + ROW_TILE - 1) // ROW_TILE) * ROW_TILE
    ends = jnp.cumsum(padded)
    starts = ends - padded
    dest = starts[bucket] + rank
    tile_start = jnp.arange(n_tiles, dtype=jnp.int32) * ROW_TILE
    tile_bucket = jnp.sum((ends[None, :] <= tile_start[:, None]).astype(jnp.int32), axis=1)
    tile_bucket = jnp.minimum(tile_bucket, N_BUCKETS - 1)
    group = tile_bucket // N_PAIRS
    pair = tile_bucket % N_PAIRS
    ea = group * EXPERTS_PER_GROUP + jnp.asarray(_PAIR_LO)[pair]
    eb = group * EXPERTS_PER_GROUP + jnp.asarray(_PAIR_HI)[pair]
    n_used = (ends[-1] // ROW_TILE).astype(jnp.int32).reshape(1)
    seg_start = jnp.concatenate([starts + counts, ends[-1:]])
    seg_len = jnp.concatenate([padded - counts, n_tiles * ROW_TILE - ends[-1:]])
    seg_end = jnp.cumsum(seg_len)
    k = jnp.arange(n_tiles * ROW_TILE - bucket.shape[0], dtype=jnp.int32)
    seg = jnp.sum((seg_end[None, :] <= k[:, None]).astype(jnp.int32), axis=1)
    pad_pos = seg_start[seg] + k - (seg_end - seg_len)[seg]
    return (dest.astype(jnp.int32), pad_pos.astype(jnp.int32), ea.astype(jnp.int32), eb.astype(jnp.int32),
            n_used)


def _lower_bounds(raw):
    p = jax.nn.softmax(raw.astype(F32), axis=0)
    return jnp.cumsum(p, axis=0) - p[0:1]


def kernel(x, c, ctx, c_ctx, w_ada, b_ada, norm1_g, w_in, na_rel_bias, hg_lower_fwd, hg_lower_bwd,
           hg_norm_g, w_na_o, w_hg_o, w_out, norm2_g, w_router, router_bias, w_gate, w_up, w_down,
           final_g):
    b, seq, d = x.shape
    n_ctx = ctx.shape[1]
    depth = w_ada.shape[0]
    lt = n_ctx + seq
    assert n_ctx % ROW_TILE == 0 and seq % ROW_TILE == 0 and seq % GRID_W == 0
    assert n_ctx % HG_CHUNK == 0 and seq % HG_CHUNK == 0
    assert seq // GRID_W >= NA_WIN_ROWS + NA_Q_ROWS - 1 and (seq // GRID_W) % NA_Q_ROWS == 0
    assert n_ctx % (NA_Q_ROWS * GRID_W) == 0

    ada_rows = -(-(b + 1) // 8) * 8
    ctx_row = b
    cc = jnp.concatenate([c, c_ctx[None, :], jnp.zeros((ada_rows - b - 1, d), F32)], axis=0)
    mods = _ada(cc, w_ada, b_ada).reshape(depth, ada_rows, 6, d)

    lb_f = _lower_bounds(hg_lower_fwd)
    lb_b = _lower_bounds(hg_lower_bwd)
    cos_t, sin_t = _rope_tables(n_ctx, seq)
    w_router_t = jnp.transpose(w_router)
    w_router_rows = w_router_t.reshape(N_EXPERTS, 1, d)
    rb = router_bias.astype(F32).reshape(N_EXPERTS, 1)

    w_in_b, w_na_b, w_hg_b, w_out_b = (w.astype(BF16) for w in (w_in, w_na_o, w_hg_o, w_out))
    ff = w_gate.shape[-1]
    w_gate_b = w_gate.astype(BF16).reshape(depth * N_EXPERTS, d, ff)
    w_up_b = w_up.astype(BF16).reshape(depth * N_EXPERTS, d, ff)
    w_down_b = w_down.astype(BF16).reshape(depth * N_EXPERTS, ff, d)

    assert n_ctx == ROW_TILE
    stream = (ctx, x, 1)
    out = None
    for l in range(depth):
        last = l == depth - 1
        skip = n_ctx // ROW_TILE if last else 0
        n_tiles = b * (lt // ROW_TILE - skip) + N_BUCKETS
        q, k, v, hq, *gates, hi, og, ga, gh = _inproj(
            stream, lt, mods[l], norm1_g[l].reshape(1, d), w_in_b, l, cos_t, sin_t, lb_f[l], lb_b[l],
            ctx_row)
        o_na = _na_attention(q, k, v, na_rel_bias[l], n_ctx, with_ctx=not last)
        o_hg = _hgrn(hq, hi, gates[:3], gates[3:], n_ctx)
        xn, h2, bucket, rank, counts = _merge(
            o_na, o_hg, og, ga, gh, stream, mods[l], hg_norm_g[l].reshape(1, HG_DIM),
            norm2_g[l].reshape(1, d), w_na_b, w_hg_b, w_out_b, l, w_router_t, rb, ctx_row, skip)
        dest, pad_pos, tile_ea, tile_eb, n_used = _sorted_layout(
            bucket.reshape(-1), rank.reshape(-1), counts[:, 0], n_tiles)
        xs = _scatter_rows(h2, dest, pad_pos)
        ys = _moe(xs, tile_ea, tile_eb, n_used, w_router_rows, w_gate_b, w_up_b, w_down_b, l)
        res = _gather_residual(xn, mods[l], ys, dest, final_g.reshape(1, d), ctx_row, last, skip)
        if last:
            out = res
        else:
            stream = (res, res, 0)
    return out
```

```python
import functools
import math

import jax
import jax.numpy as jnp
import numpy as np
from jax import lax
from jax.experimental import pallas as pl
from jax.experimental.pallas import tpu as pltpu

F32 = jnp.float32
BF16 = jnp.bfloat16

GRID_W = 64
EPS = 1e-6
NA_HEADS = 8
NA_HEAD_DIM = 64
NA_WIDTH = NA_HEADS * NA_HEAD_DIM
NA_WIN_ROWS = 8
NA_WIN_COLS = 16
ROPE_THETA = 10000.0
HG_HEADS = 4
HG_DIM = 128
HG_WIDTH = HG_HEADS * HG_DIM
GATE_FLOOR = 1e-30
N_EXPERTS = 16
N_GROUPS = 4
EXPERTS_PER_GROUP = 4
N_PAIRS = 6
N_BUCKETS = N_GROUPS * N_PAIRS
BUCKET_ROWS = 32
MASKED = -1e30
LOG2E = 1.4426950408889634

LANES = 128
MXU_COLS = 256
ROW_TILE = 256
NA_Q_ROWS = 2
SUBLANES = 8
HG_CHUNK = 128
HG_UNROLL = 4
VMEM_LIMIT = 56 * 1024 * 1024

_NT = (((1,), (1,)), ((), ()))
_TN = (((0,), (0,)), ((), ()))


def _cparams(sem):
    return pltpu.CompilerParams(dimension_semantics=sem, vmem_limit_bytes=VMEM_LIMIT)


def _ada_kernel(c_ref, w_ref, b_ref, o_ref):
    cc = c_ref[...]
    s = cc * jax.nn.sigmoid(cc)
    o_ref[0] = jnp.dot(s, w_ref[0], preferred_element_type=F32) + b_ref[0]


def _ada(cc, w_ada, b_ada):
    depth, d, n = w_ada.shape
    rows = cc.shape[0]
    tn = 1536
    return pl.pallas_call(
        _ada_kernel,
        grid=(depth, n // tn),
        in_specs=[pl.BlockSpec((rows, d), lambda l, j: (0, 0)),
                  pl.BlockSpec((1, d, tn), lambda l, j: (l, 0, j)),
                  pl.BlockSpec((1, 1, tn), lambda l, j: (l, 0, j))],
        out_specs=pl.BlockSpec((1, rows, tn), lambda l, j: (l, 0, j)),
        out_shape=jax.ShapeDtypeStruct((depth, rows, n), F32),
        compiler_params=_cparams(("arbitrary", "arbitrary")),
        name="ada",
    )(cc, w_ada, b_ada.reshape(depth, 1, n))


def _inproj_kernel(x0_ref, x_ref, mod_ref, g_ref, w_ref, cos_ref, sin_ref, lbf_ref, lbb_ref,
                   q_ref, k_ref, v_ref, hq_ref, lfh_ref, lfl_ref, kf_ref, lbh_ref, lbl_ref, kb_ref,
                   hi_ref, og_ref, ga_ref, gh_ref):
    x = jnp.where(pl.program_id(1) == 0, x0_ref[0], x_ref[0])
    y = x * lax.rsqrt(jnp.mean(x * x, axis=-1, keepdims=True) + EPS) * g_ref[...]
    h = (y * (1.0 + mod_ref[0, 1:2, :]) + mod_ref[0, 0:1, :]).astype(BF16)

    def proj(off, width):
        return jnp.dot(h, w_ref[0, :, off:off + width], preferred_element_type=F32)

    cos = cos_ref[...]
    sin = sin_ref[...]
    even = (lax.broadcasted_iota(jnp.int32, cos.shape, 1) % 2) == 0

    def rope(z):
        swapped = jnp.where(even, pltpu.roll(z, LANES - 1, 1), pltpu.roll(z, 1, 1))
        return z * cos + swapped * sin

    scale = NA_HEAD_DIM ** -0.5 * LOG2E
    d = x.shape[-1]

    def rope_out(o_ref, mult):
        def epilogue(z, sl):
            for sub in range(MXU_COLS // LANES):
                src = slice(sub * LANES, (sub + 1) * LANES)
                dst = slice(sl.start + sub * LANES, sl.start + (sub + 1) * LANES)
                r = rope(z[:, src])
                o_ref[0, :, dst] = (r if mult is None else r * mult).astype(o_ref.dtype)
        return epilogue

    def plain_out(o_ref):
        def epilogue(z, sl):
            o_ref[0, :, sl] = z.astype(o_ref.dtype)
        return epilogue

    def silu_out(o_ref):
        def epilogue(z, sl):
            o_ref[0, :, sl] = (z * jax.nn.sigmoid(z)).astype(o_ref.dtype)
        return epilogue

    def gate_out(lb_ref, hi_ref, lo_ref, key_ref):
        def epilogue(f, sl):
            lb = lb_ref[:, sl]
            one_m_lb = 1.0 - lb
            g = lb + one_m_lb * jax.nn.sigmoid(f)
            lg = jnp.log(jnp.maximum(g, GATE_FLOOR)) * LOG2E
            hi = lg.astype(BF16)
            hi_ref[0, :, sl] = hi
            lo_ref[0, :, sl] = (lg - hi.astype(F32)).astype(BF16)
            key_ref[0, :, sl] = (one_m_lb * jax.nn.sigmoid(-f)).astype(BF16)
        return epilogue

    segments = [(NA_WIDTH, rope_out(q_ref, scale), True), (NA_WIDTH, rope_out(k_ref, None), True),
                (NA_WIDTH, plain_out(v_ref), False), (HG_WIDTH, silu_out(hq_ref), True),
                (HG_WIDTH, gate_out(lbf_ref, lfh_ref, lfl_ref, kf_ref), True),
                (HG_WIDTH, gate_out(lbb_ref, lbh_ref, lbl_ref, kb_ref), True),
                (HG_WIDTH, plain_out(hi_ref), False), (HG_WIDTH, plain_out(og_ref), False),
                (d, plain_out(ga_ref), False), (d, plain_out(gh_ref), False)]
    heavy, light, off = [], [], 0
    for width, epilogue, is_heavy in segments:
        for c in range(0, width, MXU_COLS):
            (heavy if is_heavy else light).append((off + c, slice(c, c + MXU_COLS), epilogue))
        off += width
    order = []
    while heavy or light:
        if heavy:
            order.append(heavy.pop(0))
        order.extend(light[:1])
        del light[:1]
    z = proj(order[0][0], MXU_COLS)
    for i, (_, sl, epilogue) in enumerate(order):
        z_next = proj(order[i + 1][0], MXU_COLS) if i + 1 < len(order) else None
        epilogue(z, sl)
        z = z_next


def _stream_specs(first, rest, n_first):
    d = first.shape[-1]
    return (pl.BlockSpec((1, ROW_TILE, d), lambda bi, j: (bi, 0, 0)),
            pl.BlockSpec((1, ROW_TILE, d), lambda bi, j: (bi, jnp.maximum(j - n_first, 0), 0)))


def _inproj(stream, lt, mods, g, w_all, layer, cos_t, sin_t, lb_f, lb_b, ctx_row):
    first, rest, n_first = stream
    b, _, d = first.shape
    n_t = lt // ROW_TILE

    def rows(width):
        return pl.BlockSpec((1, ROW_TILE, width), lambda bi, j: (bi, j, 0))

    def mod_map(bi, j):
        return (jnp.where(j == 0, ctx_row, bi), 0, 0)

    widths = [NA_WIDTH] * 3 + [HG_WIDTH] * 9 + [d, d]
    return pl.pallas_call(
        _inproj_kernel,
        grid=(b, n_t),
        in_specs=[*_stream_specs(first, rest, n_first),
                  pl.BlockSpec((1, 6, d), mod_map),
                  pl.BlockSpec((1, d), lambda bi, j: (0, 0)),
                  pl.BlockSpec((1,) + w_all.shape[1:], lambda bi, j: (layer, 0, 0)),
                  pl.BlockSpec((ROW_TILE, LANES), lambda bi, j: (j, 0)),
                  pl.BlockSpec((ROW_TILE, LANES), lambda bi, j: (j, 0)),
                  pl.BlockSpec((1, HG_WIDTH), lambda bi, j: (0, 0)),
                  pl.BlockSpec((1, HG_WIDTH), lambda bi, j: (0, 0))],
        out_specs=[rows(w) for w in widths],
        out_shape=[jax.ShapeDtypeStruct((b, lt, w), BF16) for w in widths],
        compiler_params=_cparams(("arbitrary", "arbitrary")),
        name="inproj",
    )(first, rest, mods, g, w_all, cos_t, sin_t, lb_f.reshape(1, HG_WIDTH), lb_b.reshape(1, HG_WIDTH))


def _rope_tables(n_ctx, seq):
    t = jnp.arange(seq, dtype=jnp.int32)
    row = (t // GRID_W).astype(F32)
    col = (t % GRID_W).astype(F32)
    rot_half = NA_HEAD_DIM // 2
    inv = ROPE_THETA ** (-jnp.arange(0, rot_half, 2, dtype=F32) / rot_half)
    ang = jnp.concatenate([row[:, None] * inv, col[:, None] * inv], axis=-1)
    cos = jnp.repeat(jnp.cos(ang), 2, axis=-1)
    sin = jnp.repeat(jnp.sin(ang), 2, axis=-1)
    sign = jnp.asarray(np.tile(np.array([-1.0, 1.0], np.float32), NA_HEAD_DIM // 2))
    sin = sin * sign
    cos = jnp.concatenate([jnp.ones((n_ctx, NA_HEAD_DIM), F32), cos], axis=0)
    sin = jnp.concatenate([jnp.zeros((n_ctx, NA_HEAD_DIM), F32), sin], axis=0)
    reps = LANES // NA_HEAD_DIM
    return jnp.tile(cos, (1, reps)), jnp.tile(sin, (1, reps))


def _na_plan(n_ctx, grid_rows):
    span = NA_WIN_ROWS + NA_Q_ROWS - 1
    variants, var_of_step, ws_of_step = [], [], []
    for r0 in range(0, grid_rows, NA_Q_ROWS):
        rs = [min(max(r0 + dq - NA_WIN_ROWS // 2, 0), grid_rows - NA_WIN_ROWS) for dq in range(NA_Q_ROWS)]
        ws = min(rs[0], grid_rows - span)
        key = (r0 - ws,) + tuple(r - ws for r in rs)
        if key not in variants:
            variants.append(key)
        var_of_step.append(variants.index(key))
        ws_of_step.append(ws)
    ctx_steps = n_ctx // (NA_Q_ROWS * GRID_W)
    var_of_step = [len(variants)] * ctx_steps + var_of_step
    ws_of_step = [0] * ctx_steps + ws_of_step
    return variants, np.asarray(var_of_step, np.int32), np.asarray(ws_of_step, np.int32)


def _na_bias_planes(rel_bias, variants):
    span = NA_WIN_ROWS + NA_Q_ROWS - 1
    n_dr, n_dc = 2 * NA_WIN_ROWS - 1, 2 * NA_WIN_COLS - 1
    col = np.arange(GRID_W)[:, None]
    kc = np.arange(GRID_W)[None, :]
    cs = np.clip(col - NA_WIN_COLS // 2, 0, GRID_W - NA_WIN_COLS)
    col_ok = (kc >= cs) & (kc < cs + NA_WIN_COLS)
    oh_c = ((kc - col + NA_WIN_COLS - 1)[..., None] == np.arange(n_dc)) & col_ok[..., None]
    planes = jnp.einsum("hrc,xkc->hrxk", rel_bias.astype(F32), jnp.asarray(oh_c.astype(np.float32)),
                        precision=lax.Precision.HIGHEST)
    planes = jnp.where(jnp.asarray(col_ok), planes * LOG2E, MASKED)
    planes = jnp.concatenate([planes, jnp.full_like(planes[:, :1], MASKED)], axis=1)
    plane_of = np.full((len(variants) + 1, NA_Q_ROWS, span), n_dr, np.int32)
    for vi, key in enumerate(variants):
        for dq in range(NA_Q_ROWS):
            for j in range(key[1 + dq], key[1 + dq] + NA_WIN_ROWS):
                plane_of[vi, dq, j] = j - (key[0] + dq) + NA_WIN_ROWS - 1
    return planes, plane_of.reshape(-1)


def _na_kernel(var_ref, ws_ref, plane_ref, q_ref, k_ref, v_ref, b_ref, o_ref, sw_ref, sc_ref, t_ref,
               *, step_off, n_ctx):
    nq = q_ref.shape[1]
    win = sw_ref.shape[1]
    span = win // GRID_W
    j = pl.program_id(1) + step_off
    var = var_ref[j]

    @pl.when((pl.program_id(1) == 0) | (var != var_ref[jnp.maximum(j - 1, 0)]))
    def _():
        for h in range(NA_HEADS):
            for dq in range(nq // GRID_W):
                base = (var * (nq // GRID_W) + dq) * span
                row = jnp.concatenate([b_ref[h, plane_ref[base + jj]] for jj in range(span)], axis=1)
                t_ref[h, dq * GRID_W:(dq + 1) * GRID_W, :] = row

    start = pl.multiple_of(n_ctx + ws_ref[j] * GRID_W, GRID_W)
    low = lax.broadcasted_iota(jnp.int32, (nq, LANES), 1) < NA_HEAD_DIM
    for hp in range(NA_HEADS // 2):
        sl = slice(hp * LANES, (hp + 1) * LANES)
        q2 = q_ref[0, :, sl]
        kw = k_ref[0, pl.ds(start, win), sl]
        kc = k_ref[0, 0:n_ctx, sl]
        zero = jnp.zeros_like(q2)
        qs = jnp.concatenate([jnp.where(low, q2, zero), jnp.where(low, zero, q2)], axis=0)
        rows = slice(2 * hp * nq, (2 * hp + 2) * nq)
        bias = t_ref[2 * hp:2 * hp + 2].reshape(2 * nq, win)
        sw_ref[rows, :] = lax.dot_general(qs, kw, _NT, preferred_element_type=F32) + bias
        sc_ref[rows, :] = lax.dot_general(qs, kc, _NT, preferred_element_type=F32)
    sw = sw_ref[...]
    sc = sc_ref[...]
    m = jnp.maximum(jnp.max(sw, axis=-1, keepdims=True), jnp.max(sc, axis=-1, keepdims=True))
    pw = jnp.exp2(sw - m)
    pc = jnp.exp2(sc - m)
    inv = 1.0 / (jnp.sum(pw, axis=-1, keepdims=True) + jnp.sum(pc, axis=-1, keepdims=True))
    pw = pw.astype(BF16)
    pc = pc.astype(BF16)
    for hp in range(NA_HEADS // 2):
        sl = slice(hp * LANES, (hp + 1) * LANES)
        vw = v_ref[0, pl.ds(start, win), sl]
        vc = v_ref[0, 0:n_ctx, sl]
        rows = slice(2 * hp * nq, (2 * hp + 2) * nq)
        o2 = (jnp.dot(pw[rows], vw, preferred_element_type=F32)
              + jnp.dot(pc[rows], vc, preferred_element_type=F32)) * inv[rows]
        o_ref[0, :, sl] = jnp.where(low, o2[:nq], o2[nq:]).astype(o_ref.dtype)


def _na_attention(q, k, v, rel_bias, n_ctx, with_ctx):
    b, lt, _ = q.shape
    nq = NA_Q_ROWS * GRID_W
    span = NA_WIN_ROWS + NA_Q_ROWS - 1
    variants, var_of_step, ws_of_step = _na_plan(n_ctx, (lt - n_ctx) // GRID_W)
    planes, plane_of = _na_bias_planes(rel_bias, variants)
    step_off = 0 if with_ctx else n_ctx // nq
    n_steps = lt // nq - step_off

    kern = functools.partial(_na_kernel, step_off=step_off, n_ctx=n_ctx)
    grid_spec = pltpu.PrefetchScalarGridSpec(
        num_scalar_prefetch=3,
        grid=(b, n_steps),
        in_specs=[pl.BlockSpec((1, nq, NA_WIDTH), lambda bi, j, *_: (bi, j + step_off, 0)),
                  pl.BlockSpec((1, lt, NA_WIDTH), lambda bi, j, *_: (bi, 0, 0)),
                  pl.BlockSpec((1, lt, NA_WIDTH), lambda bi, j, *_: (bi, 0, 0)),
                  pl.BlockSpec(planes.shape, lambda bi, j, *_: (0, 0, 0, 0))],
        out_specs=pl.BlockSpec((1, nq, NA_WIDTH), lambda bi, j, *_: (bi, j, 0)),
        scratch_shapes=[pltpu.VMEM((NA_HEADS * nq, span * GRID_W), F32),
                        pltpu.VMEM((NA_HEADS * nq, n_ctx), F32),
                        pltpu.VMEM((NA_HEADS, nq, span * GRID_W), F32)],
    )
    return pl.pallas_call(
        kern,
        grid_spec=grid_spec,
        out_shape=jax.ShapeDtypeStruct((b, n_steps * nq, NA_WIDTH), BF16),
        compiler_params=_cparams(("arbitrary", "arbitrary")),
        name="na_attention",
    )(jnp.asarray(var_of_step), jnp.asarray(ws_of_step), jnp.asarray(plane_of), q, k, v, planes)


def _block_ref_rows(b, w, rev, upper_fill=None, lower_fill=None):
    c, n = b.shape
    off = w if rev else w - 1
    if upper_fill is not None or lower_fill is not None:
        parts = []
        for s in range(0, c, 2 * w):
            ref = jnp.broadcast_to(b[s + off:s + off + 1], (w, n))
            parts.append(ref if lower_fill is None else jnp.full((w, n), lower_fill, b.dtype))
            parts.append(ref if upper_fill is None else jnp.full((w, n), upper_fill, b.dtype))
        return jnp.concatenate(parts, axis=0)
    if 2 * w >= SUBLANES:
        parts = [jnp.broadcast_to(b[s + off:s + off + 1], (2 * w, n)) for s in range(0, c, 2 * w)]
        return parts[0] if len(parts) == 1 else jnp.concatenate(parts, axis=0)
    b3 = b.reshape(c // SUBLANES, SUBLANES, n)
    sub = lax.broadcasted_iota(jnp.int32, b3.shape, 1)
    r = None
    for s in range(0, SUBLANES, 2 * w):
        piece = jnp.broadcast_to(b3[:, s + off:s + off + 1, :], b3.shape)
        r = piece if r is None else jnp.where(sub >= s, piece, r)
    return r.reshape(c, n)


def _hgrn_chunks(chains, states, lmask_ref):
    n = len(chains)
    c = chains[0][0].shape[0]
    qb = [ch[0] for ch in chains]
    kb = [ch[2] for ch in chains]
    b = []
    for i in range(n):
        (hi, lo), tri = chains[i][1], chains[i][4]
        b.append(jnp.dot(tri, hi, preferred_element_type=F32) + jnp.dot(tri, lo, preferred_element_type=F32))

    states = list(states)
    o = []
    for i in range(n):
        vb, rev, sid = chains[i][3], chains[i][7], chains[i][8]
        b_last = b[i][0:1] if rev else b[i][c - 1:c]
        qe = qb[i] * jnp.exp2(b[i]).astype(BF16)
        ke = kb[i] * jnp.exp2(b_last - b[i]).astype(BF16)
        st = states[sid]
        o.append(lax.dot_general(qe, st.astype(BF16), _NT, preferred_element_type=F32))
        states[sid] = st * jnp.exp2(b_last) + lax.dot_general(vb, ke, _TN, preferred_element_type=F32)

    a = [None] * n
    w = c // 2
    for li in range(lmask_ref.shape[0]):
        for i in range(n):
            negq_ref, negk_ref, rev = chains[i][5], chains[i][6], chains[i][7]
            if w >= SUBLANES:
                eq = b[i] - _block_ref_rows(b[i], w, rev, -MASKED if rev else None, None if rev else -MASKED)
                ek = _block_ref_rows(b[i], w, rev, None if rev else MASKED, MASKED if rev else None) - b[i]
                kw = kb[i] * jnp.exp2(ek).astype(BF16)
            elif w > 1:
                d = b[i] - _block_ref_rows(b[i], w, rev)
                eq = d + negq_ref[li]
                kw = kb[i] * jnp.exp2(negk_ref[li] - d).astype(BF16)
            else:
                hi, lo = chains[i][1]
                eq = hi.astype(F32) + lo.astype(F32) + negq_ref[li]
                kw = kb[i] * chains[i][9]
            qw = qb[i] * jnp.exp2(eq).astype(BF16)
            p = lax.dot_general(qw, kw, _NT, preferred_element_type=F32).astype(BF16) * lmask_ref[li]
            a[i] = p if a[i] is None else a[i] + p
        w //= 2
    for i in range(n):
        vb = chains[i][3]
        o[i] = o[i] + jnp.dot(a[i], vb, preferred_element_type=F32)
        diag = jnp.sum(qb[i].astype(F32) * kb[i].astype(F32), axis=-1, keepdims=True)
        o[i] = o[i] + diag * vb.astype(F32)
    return o, states


def _hgrn_kernel(hq_ref, hi_ref, lfh_ref, lfl_ref, kf_ref, lbh_ref, lbl_ref, kb_ref, trif_ref, trir_ref,
                 negqf_ref, negkf_ref, negqr_ref, negkr_ref, lmask_ref, konef_ref, koner_ref,
                 o_ref, sf_ref, sb_ref, of_ref, ob_ref, *, n_ctx_chunks, n_chunks):
    sf_ref[...] = jnp.zeros_like(sf_ref)
    sb_ref[...] = jnp.zeros_like(sb_ref)

    def steps(cf0, cb0, unroll):
        chains, rows = [], []
        for u in range(unroll):
            rf = pl.ds(pl.multiple_of((cf0 + u) * HG_CHUNK, HG_CHUNK), HG_CHUNK)
            rb = pl.ds(pl.multiple_of((cb0 - u) * HG_CHUNK, HG_CHUNK), HG_CHUNK)
            chains.append((hq_ref[0, rf, :], (lfh_ref[0, rf, :], lfl_ref[0, rf, :]), kf_ref[0, rf, :],
                           hi_ref[0, rf, :], trif_ref[...], negqf_ref, negkf_ref, False, 0, konef_ref[...]))
            chains.append((hq_ref[0, rb, :], (lbh_ref[0, rb, :], lbl_ref[0, rb, :]), kb_ref[0, rb, :],
                           hi_ref[0, rb, :], trir_ref[...], negqr_ref, negkr_ref, True, 1, koner_ref[...]))
            rows += [rf, rb]
        outs, (sf, sb) = _hgrn_chunks(chains, [sf_ref[...], sb_ref[...]], lmask_ref)
        for i, r in enumerate(rows):
            if i % 2 == 0:
                of_ref[r, :] = outs[i]
            else:
                ob_ref[r, :] = outs[i]
        sf_ref[...] = sf
        sb_ref[...] = sb

    n_lat = n_chunks - n_ctx_chunks
    u_ctx = math.gcd(HG_UNROLL, n_ctx_chunks)
    u_lat = math.gcd(HG_UNROLL, n_lat)

    def ctx_body(i, carry):
        steps(i * u_ctx, n_ctx_chunks - 1 - i * u_ctx, u_ctx)
        return carry

    def lat_body(i, carry):
        steps(n_ctx_chunks + i * u_lat, n_chunks - 1 - i * u_lat, u_lat)
        return carry

    lax.fori_loop(0, n_ctx_chunks // u_ctx, ctx_body, 0)
    lax.fori_loop(0, n_lat // u_lat, lat_body, 0)
    o_ref[0] = (of_ref[...] + ob_ref[...]).astype(o_ref.dtype)


def _hgrn_level_constants():
    c = HG_CHUNK
    t = np.arange(c)
    xor = t[:, None] ^ t[None, :]
    lmask, negq_f, negq_r = [], [], []
    w = c // 2
    while w >= 1:
        lmask.append(((xor >= w) & (xor < 2 * w)).astype(np.float32))
        upper = (t % (2 * w)) >= w
        negq_f.append(np.where(upper, 0.0, MASKED))
        negq_r.append(np.where(upper, MASKED, 0.0))
        w //= 2

    def rows(m):
        return jnp.asarray(np.broadcast_to(np.stack(m)[:, :, None], (len(m), c, HG_DIM)).astype(np.float32))

    tri_f = (t[None, :] <= t[:, None]).astype(np.float32)
    odd = np.broadcast_to((t % 2 == 1)[:, None], (c, HG_DIM)).astype(np.float32)
    return (jnp.asarray(tri_f, BF16), jnp.asarray(tri_f.T, BF16),
            rows(negq_f), rows(negq_r), rows(negq_r), rows(negq_f), jnp.asarray(np.stack(lmask), BF16),
            jnp.asarray(1.0 - odd, BF16), jnp.asarray(odd, BF16))


def _hgrn(hq, hi, fwd, bwd, n_ctx):
    b, lt, _ = hq.shape
    kern = functools.partial(_hgrn_kernel, n_ctx_chunks=n_ctx // HG_CHUNK, n_chunks=lt // HG_CHUNK)
    seq = pl.BlockSpec((1, lt, HG_DIM), lambda bi, h: (bi, 0, h))
    consts = _hgrn_level_constants()
    return pl.pallas_call(
        kern,
        grid=(b, HG_HEADS),
        in_specs=[seq] * 8
        + [pl.BlockSpec(a.shape, lambda bi, h, nd=a.ndim: (0,) * nd) for a in consts],
        out_specs=seq,
        out_shape=jax.ShapeDtypeStruct((b, lt, HG_WIDTH), BF16),
        scratch_shapes=[pltpu.VMEM((HG_DIM, HG_DIM), F32), pltpu.VMEM((HG_DIM, HG_DIM), F32),
                        pltpu.VMEM((lt, HG_DIM), F32), pltpu.VMEM((lt, HG_DIM), F32)],
        compiler_params=_cparams(("arbitrary", "arbitrary")),
        name="hgrn2",
    )(hq, hi, *fwd, *bwd, *consts)


def _split2(x):
    hi = x.astype(BF16)
    lo = (x - hi.astype(F32)).astype(BF16)
    return hi, lo


def _route(logits, bias):
    biased = jax.nn.sigmoid(logits) + bias
    rows = [biased[e:e + 1] for e in range(N_EXPERTS)]
    n = EXPERTS_PER_GROUP
    best = None
    g_sel = None
    for g in range(N_GROUPS):
        gs = None
        for i in range(n):
            for j2 in range(i + 1, n):
                pair = rows[g * n + i] + rows[g * n + j2]
                gs = pair if gs is None else jnp.maximum(gs, pair)
        if best is None:
            best, g_sel = gs, jnp.zeros(gs.shape, jnp.int32)
        else:
            take = gs > best
            best = jnp.where(take, gs, best)
            g_sel = jnp.where(take, g, g_sel)
    cand = []
    for i in range(n):
        c_i = rows[(N_GROUPS - 1) * n + i]
        for g in range(N_GROUPS - 2, -1, -1):
            c_i = jnp.where(g_sel == g, rows[g * n + i], c_i)
        cand.append(c_i)
    m1, i1 = cand[0], jnp.zeros(best.shape, jnp.int32)
    for i in range(1, n):
        take = cand[i] > m1
        m1 = jnp.where(take, cand[i], m1)
        i1 = jnp.where(take, i, i1)
    m2 = jnp.full(best.shape, -jnp.inf, F32)
    i2 = jnp.zeros(best.shape, jnp.int32)
    for i in range(n):
        take = (i1 != i) & (cand[i] > m2)
        m2 = jnp.where(take, cand[i], m2)
        i2 = jnp.where(take, i, i2)
    lo = jnp.minimum(i1, i2)
    hi = jnp.maximum(i1, i2)
    pair = jnp.where(lo == 0, hi - 1, jnp.where(lo == 1, hi + 1, N_PAIRS - 1))
    return g_sel * N_PAIRS + pair


def _merge_kernel(ona_ref, ohg_ref, og_ref, ga_ref, gh_ref, x0_ref, x_ref, mod_ref, hgg_ref, n2g_ref,
                  wna_ref, whg_ref, wout_ref, wr_ref, rb_ref,
                  xn_ref, h2_ref, bucket_ref, rank_ref, cnt_ref, carry_ref, hprev_ref,
                  *, n_t, n_tot, skip):
    step = pl.program_id(0)
    first_tile = (jnp.minimum(step, n_tot - 1) % n_t + skip) == 0

    @pl.when(step == 0)
    def _():
        carry_ref[...] = jnp.zeros_like(carry_ref)
        hprev_ref[...] = jnp.zeros_like(hprev_ref)

    w_hi, w_lo = _split2(wr_ref[...])
    h_hi, h_lo = _split2(hprev_ref[...])
    logits = (lax.dot_general(w_hi, h_hi, _NT, preferred_element_type=F32)
              + lax.dot_general(w_hi, h_lo, _NT, preferred_element_type=F32)
              + lax.dot_general(w_lo, h_hi, _NT, preferred_element_type=F32))

    ohg = ohg_ref[0].astype(F32)
    og = og_ref[0].astype(F32)
    gain = hgg_ref[...]
    heads = []
    for h in range(HG_HEADS):
        sl = slice(h * HG_DIM, (h + 1) * HG_DIM)
        oh = ohg[:, sl]
        yh = oh * lax.rsqrt(jnp.mean(oh * oh, axis=-1, keepdims=True) + EPS) * gain
        gt = og[:, sl]
        heads.append((yh * (gt * jax.nn.sigmoid(gt))).astype(BF16))
    hn = jnp.concatenate(heads, axis=-1)
    y_na = jnp.dot(ona_ref[0], wna_ref[0], preferred_element_type=F32)
    y_hg = jnp.dot(hn, whg_ref[0], preferred_element_type=F32)

    bucket = _route(logits, rb_ref[...])
    bucket_ref[0] = bucket

    m = jax.nn.sigmoid(ga_ref[0].astype(F32)) * y_na + jax.nn.sigmoid(gh_ref[0].astype(F32)) * y_hg
    y = jnp.dot(m.astype(BF16), wout_ref[0], preferred_element_type=F32)

    t = bucket.shape[1]
    onehot = (lax.broadcasted_iota(jnp.int32, (BUCKET_ROWS, t), 0) == bucket).astype(F32)
    before = (lax.broadcasted_iota(jnp.int32, (t, t), 0)
              < lax.broadcasted_iota(jnp.int32, (t, t), 1)).astype(BF16)
    prefix = jnp.dot(onehot.astype(BF16), before, preferred_element_type=F32)
    carry = carry_ref[...]
    rank = jnp.sum(onehot * (prefix + carry[:, 0:1]), axis=0, keepdims=True)
    rank_ref[0] = rank.astype(jnp.int32)
    live = (step > 0).astype(F32)
    carry = carry + live * jnp.sum(onehot, axis=1, keepdims=True)
    carry_ref[...] = carry
    cnt_ref[...] = carry.astype(jnp.int32)

    xn = jnp.where(first_tile, x0_ref[0], x_ref[0]) + mod_ref[0, 2:3, :] * y
    xn_ref[0] = xn
    yn = xn * lax.rsqrt(jnp.mean(xn * xn, axis=-1, keepdims=True) + EPS) * n2g_ref[...]
    h2 = yn * (1.0 + mod_ref[0, 4:5, :]) + mod_ref[0, 3:4, :]
    h2_ref[...] = h2
    hprev_ref[...] = h2


def _merge(o_na, o_hg, og, ga, gh, stream, mods, hg_gain, n2_gain, w_na_o, w_hg_o, w_out, layer,
           w_router_t, router_bias, ctx_row, skip):
    first, rest, n_first = stream
    b, lt, _ = o_hg.shape
    d = first.shape[-1]
    n_t = lt // ROW_TILE - skip
    n_tot = b * n_t

    def tile(s):
        t = jnp.minimum(s, n_tot - 1)
        return t // n_t, t % n_t

    def rows(width):
        return pl.BlockSpec((1, ROW_TILE, width), lambda s: (tile(s)[0], tile(s)[1] + skip, 0))

    def full(a):
        return pl.BlockSpec(a.shape, lambda s: (0,) * a.ndim)

    def of_layer(a):
        return pl.BlockSpec((1,) + a.shape[1:], lambda s: (layer,) + (0,) * (a.ndim - 1))

    def mod_map(s):
        bi, j = tile(s)
        return (jnp.where(j + skip == 0, ctx_row, bi), 0, 0)

    tok = pl.BlockSpec((1, 1, ROW_TILE), lambda s: (jnp.maximum(s - 1, 0), 0, 0))
    consts = [hg_gain, n2_gain, w_na_o, w_hg_o, w_out, w_router_t, router_bias]
    const_specs = [full(hg_gain), full(n2_gain), of_layer(w_na_o), of_layer(w_hg_o), of_layer(w_out),
                   full(w_router_t), full(router_bias)]
    kern = functools.partial(_merge_kernel, n_t=n_t, n_tot=n_tot, skip=skip)
    return pl.pallas_call(
        kern,
        grid=(n_tot + 1,),
        in_specs=[pl.BlockSpec((1, ROW_TILE, NA_WIDTH), lambda s: tile(s) + (0,)),
                  rows(HG_WIDTH), rows(HG_WIDTH), rows(d), rows(d),
                  pl.BlockSpec((1, ROW_TILE, d), lambda s: (tile(s)[0], 0, 0)),
                  pl.BlockSpec((1, ROW_TILE, d),
                               lambda s: (tile(s)[0], jnp.maximum(tile(s)[1] + skip - n_first, 0), 0)),
                  pl.BlockSpec((1, 6, d), mod_map)] + const_specs,
        out_specs=[pl.BlockSpec((1, ROW_TILE, d), lambda s: tile(s) + (0,)),
                   pl.BlockSpec((ROW_TILE, d), lambda s: (jnp.minimum(s, n_tot - 1), 0)),
                   tok, tok,
                   pl.BlockSpec((BUCKET_ROWS, LANES), lambda s: (0, 0))],
        out_shape=[jax.ShapeDtypeStruct((b, n_t * ROW_TILE, d), F32),
                   jax.ShapeDtypeStruct((b * n_t * ROW_TILE, d), F32),
                   jax.ShapeDtypeStruct((b * n_t, 1, ROW_TILE), jnp.int32),
                   jax.ShapeDtypeStruct((b * n_t, 1, ROW_TILE), jnp.int32),
                   jax.ShapeDtypeStruct((BUCKET_ROWS, LANES), jnp.int32)],
        scratch_shapes=[pltpu.VMEM((BUCKET_ROWS, LANES), F32), pltpu.VMEM((ROW_TILE, d), F32)],
        compiler_params=_cparams(("arbitrary",)),
        name="merge_router",
    )(o_na, o_hg, og, ga, gh, first, rest, mods, *consts)


def _start_row_copies(n, row_copy):
    for r in range(n):
        row_copy(r).start()


def _scatter_kernel(dest_ref, h_ref, xs_ref, zero_ref, sem, *, n_token_steps):
    n = h_ref.shape[0]
    step = pl.program_id(0)

    @pl.when(step < n_token_steps)
    def _():
        _start_row_copies(n, lambda r: pltpu.make_async_copy(
            h_ref.at[pl.ds(r, 1)], xs_ref.at[pl.ds(dest_ref[0, 0, r], 1)], sem))

    @pl.when(step >= n_token_steps)
    def _():
        zero_ref[...] = jnp.zeros_like(zero_ref)
        _start_row_copies(n, lambda r: pltpu.make_async_copy(
            zero_ref.at[pl.ds(0, 1)], xs_ref.at[pl.ds(dest_ref[0, 0, r], 1)], sem))

    pltpu.make_async_copy(h_ref, xs_ref.at[pl.ds(0, n)], sem).wait()


def _scatter_rows(h2, dest, pad_pos):
    t, d = h2.shape
    tile = 2 * ROW_TILE if t % (2 * ROW_TILE) == 0 and pad_pos.shape[0] % (2 * ROW_TILE) == 0 else ROW_TILE
    n_t = t // tile
    n_steps = n_t + pad_pos.shape[0] // tile
    kern = functools.partial(_scatter_kernel, n_token_steps=n_t)
    return pl.pallas_call(
        kern,
        grid=(n_steps,),
        in_specs=[pl.BlockSpec((1, 1, tile), lambda i: (i, 0, 0), memory_space=pltpu.SMEM),
                  pl.BlockSpec((tile, d), lambda i: (jnp.minimum(i, n_t - 1), 0))],
        out_specs=pl.BlockSpec(memory_space=pl.ANY),
        out_shape=jax.ShapeDtypeStruct((n_steps * tile, d), F32),
        scratch_shapes=[pltpu.VMEM((SUBLANES, d), F32), pltpu.SemaphoreType.DMA(())],
        compiler_params=_cparams(("arbitrary",)),
        name="scatter_rows",
    )(jnp.concatenate([dest, pad_pos]).reshape(n_steps, 1, tile), h2)


def _final_kernel(dest_ref, dest_next_ref, xn_ref, mod_ref, g_ref, ys_ref, o_ref, buf_ref, sems,
                  *, final_norm):
    n = buf_ref.shape[1]
    step = pl.program_id(0) * pl.num_programs(1) + pl.program_id(1)
    n_steps = pl.num_programs(0) * pl.num_programs(1)

    def gather(d_ref, slot):
        _start_row_copies(n, lambda r: pltpu.make_async_copy(
            ys_ref.at[pl.ds(d_ref[0, 0, r], 1)], buf_ref.at[slot, pl.ds(r, 1)], sems.at[slot]))

    @pl.when(step == 0)
    def _():
        gather(dest_ref, 0)

    @pl.when(step + 1 < n_steps)
    def _():
        gather(dest_next_ref, (step + 1) % 2)

    slot = step % 2
    pltpu.make_async_copy(ys_ref.at[pl.ds(0, n)], buf_ref.at[slot], sems.at[slot]).wait()
    x = xn_ref[0] + mod_ref[0, 5:6, :] * buf_ref[slot]
    if final_norm:
        x = x * lax.rsqrt(jnp.mean(x * x, axis=-1, keepdims=True) + EPS) * g_ref[...]
    o_ref[0] = x


def _gather_residual(xn, mods, ys, dest, gain, ctx_row, final_norm, skip):
    b, lt, d = xn.shape
    n_t = lt // ROW_TILE

    def mod_map(bi, j):
        return (jnp.where(j + skip == 0, ctx_row, bi), 0, 0)

    kern = functools.partial(_final_kernel, final_norm=final_norm)
    last_tile = b * n_t - 1
    dest3 = dest.reshape(b * n_t, 1, ROW_TILE)
    return pl.pallas_call(
        kern,
        grid=(b, n_t),
        in_specs=[pl.BlockSpec((1, 1, ROW_TILE), lambda bi, j: (bi * n_t + j, 0, 0),
                               memory_space=pltpu.SMEM),
                  pl.BlockSpec((1, 1, ROW_TILE), lambda bi, j: (jnp.minimum(bi * n_t + j + 1, last_tile), 0, 0),
                               memory_space=pltpu.SMEM),
                  pl.BlockSpec((1, ROW_TILE, d), lambda bi, j: (bi, j, 0)),
                  pl.BlockSpec((1, 6, d), mod_map),
                  pl.BlockSpec((1, d), lambda bi, j: (0, 0)),
                  pl.BlockSpec(memory_space=pl.ANY)],
        out_specs=pl.BlockSpec((1, ROW_TILE, d), lambda bi, j: (bi, j, 0)),
        out_shape=jax.ShapeDtypeStruct((b, lt, d), F32),
        scratch_shapes=[pltpu.VMEM((2, ROW_TILE, d), F32), pltpu.SemaphoreType.DMA((2,))],
        compiler_params=_cparams(("arbitrary", "arbitrary")),
        name="gather_residual",
    )(dest3, dest3, xn, mods, gain, ys)


def _moe_kernel(ea_ref, eb_ref, nused_ref, xs_ref, wra_ref, wrb_ref,
                wga_ref, wua_ref, wda_ref, wgb_ref, wub_ref, wdb_ref, ys_ref):
    del ea_ref, eb_ref
    i = pl.program_id(0)

    @pl.when(i < nused_ref[0])
    def _():
        x = xs_ref[...]
        s_a = jax.nn.sigmoid(jnp.sum(x * wra_ref[0], axis=-1, keepdims=True))
        s_b = jax.nn.sigmoid(jnp.sum(x * wrb_ref[0], axis=-1, keepdims=True))
        tot = s_a + s_b
        xb = x.astype(BF16)

        gate_a = jnp.dot(xb, wga_ref[0], preferred_element_type=F32)
        up_a = jnp.dot(xb, wua_ref[0], preferred_element_type=F32)
        gate_b = jnp.dot(xb, wgb_ref[0], preferred_element_type=F32)
        hid_a = (gate_a * jax.nn.sigmoid(gate_a) * up_a).astype(BF16)
        up_b = jnp.dot(xb, wub_ref[0], preferred_element_type=F32)
        y_a = jnp.dot(hid_a, wda_ref[0], preferred_element_type=F32)
        hid_b = (gate_b * jax.nn.sigmoid(gate_b) * up_b).astype(BF16)
        y_b = jnp.dot(hid_b, wdb_ref[0], preferred_element_type=F32)
        ys_ref[...] = (s_a / tot) * y_a + (s_b / tot) * y_b

    @pl.when(i >= nused_ref[0])
    def _():
        ys_ref[...] = jnp.zeros_like(ys_ref)


def _moe(xs, tile_ea, tile_eb, n_used, w_router_rows, w_gate, w_up, w_down, layer):
    n_sorted, d = xs.shape
    n_tiles = n_sorted // ROW_TILE
    ff = w_gate.shape[-1]
    base = layer * N_EXPERTS

    def by_a(i, ea, eb, nu):
        return (ea[i], 0, 0)

    def by_b(i, ea, eb, nu):
        return (eb[i], 0, 0)

    def wt_a(i, ea, eb, nu):
        return (base + ea[i], 0, 0)

    def wt_b(i, ea, eb, nu):
        return (base + eb[i], 0, 0)

    grid_spec = pltpu.PrefetchScalarGridSpec(
        num_scalar_prefetch=3,
        grid=(n_tiles,),
        in_specs=[pl.BlockSpec((ROW_TILE, d), lambda i, ea, eb, nu: (i, 0)),
                  pl.BlockSpec((1, 1, d), by_a), pl.BlockSpec((1, 1, d), by_b),
                  pl.BlockSpec((1, d, ff), wt_a), pl.BlockSpec((1, d, ff), wt_a),
                  pl.BlockSpec((1, ff, d), wt_a),
                  pl.BlockSpec((1, d, ff), wt_b), pl.BlockSpec((1, d, ff), wt_b),
                  pl.BlockSpec((1, ff, d), wt_b)],
        out_specs=pl.BlockSpec((ROW_TILE, d), lambda i, ea, eb, nu: (i, 0)),
    )
    return pl.pallas_call(
        _moe_kernel,
        grid_spec=grid_spec,
        out_shape=jax.ShapeDtypeStruct((n_sorted, d), F32),
        compiler_params=_cparams(("arbitrary",)),
        name="moe_pairs",
    )(tile_ea, tile_eb, n_used, xs, w_router_rows, w_router_rows,
      w_gate, w_up, w_down, w_gate, w_up, w_down)


_PAIR_LO = np.array([0, 0, 0, 1, 1, 2], np.int32)
_PAIR_HI = np.array([1, 2, 3, 2, 3, 3], np.int32)


def _sorted_layout(bucket, rank, counts, n_tiles):
    counts = counts[:N_BUCKETS]
    padded = ((counts + ROW_TILE - 1) // ROW_TILE) * ROW_TILE
    ends = jnp.cumsum(padded)
    starts = ends - padded

    def lookup(table, idx):
        hit = jnp.arange(table.shape[0], dtype=jnp.int32)[:, None] == idx[None, :]
        return jnp.sum(jnp.where(hit, table[:, None], 0), axis=0)

    dest = lookup(starts, bucket) + rank
    tile_start = jnp.arange(n_tiles, dtype=jnp.int32) * ROW_TILE
    tile_bucket = jnp.sum((ends[None, :] <= tile_start[:, None]).astype(jnp.int32), axis=1)
    tile_bucket = jnp.minimum(tile_bucket, N_BUCKETS - 1)
    group = tile_bucket // N_PAIRS
    pair = tile_bucket % N_PAIRS
    ea = group * EXPERTS_PER_GROUP + jnp.asarray(_PAIR_LO)[pair]
    eb = group * EXPERTS_PER_GROUP + jnp.asarray(_PAIR_HI)[pair]
    n_used = (ends[-1] // ROW_TILE).astype(jnp.int32).reshape(1)
    seg_start = jnp.concatenate([starts + counts, ends[-1:]])
    seg_len = jnp.concatenate([padded - counts, n_tiles * ROW_TILE - ends[-1:]])
    seg_end = jnp.cumsum(seg_len)
    k = jnp.arange(n_tiles * ROW_TILE - bucket.shape[0], dtype=jnp.int32)
    seg = jnp.sum((seg_end[:, None] <= k[None, :]).astype(jnp.int32), axis=0)
    pad_pos = lookup(seg_start - (seg_end - seg_len), seg) + k
    return (dest.astype(jnp.int32), pad_pos.astype(jnp.int32), ea.astype(jnp.int32), eb.astype(jnp.int32),
            n_used)


def _lower_bounds(raw):
    p = jax.nn.softmax(raw.astype(F32), axis=0)
    return jnp.cumsum(p, axis=0) - p[0:1]


def kernel(x, c, ctx, c_ctx, w_ada, b_ada, norm1_g, w_in, na_rel_bias, hg_lower_fwd, hg_lower_bwd,
           hg_norm_g, w_na_o, w_hg_o, w_out, norm2_g, w_router, router_bias, w_gate, w_up, w_down,
           final_g):
    b, seq, d = x.shape
    n_ctx = ctx.shape[1]
    depth = w_ada.shape[0]
    lt = n_ctx + seq
    assert n_ctx % ROW_TILE == 0 and seq % ROW_TILE == 0 and seq % GRID_W == 0
    assert n_ctx % HG_CHUNK == 0 and seq % HG_CHUNK == 0
    assert seq // GRID_W >= NA_WIN_ROWS + NA_Q_ROWS - 1 and (seq // GRID_W) % NA_Q_ROWS == 0
    assert n_ctx % (NA_Q_ROWS * GRID_W) == 0

    ada_rows = -(-(b + 1) // 8) * 8
    ctx_row = b
    cc = jnp.concatenate([c, c_ctx[None, :], jnp.zeros((ada_rows - b - 1, d), F32)], axis=0)
    mods = _ada(cc, w_ada, b_ada).reshape(depth, ada_rows, 6, d)

    lb_f = _lower_bounds(hg_lower_fwd)
    lb_b = _lower_bounds(hg_lower_bwd)
    cos_t, sin_t = _rope_tables(n_ctx, seq)
    w_router_t = jnp.transpose(w_router)
    w_router_rows = w_router_t.reshape(N_EXPERTS, 1, d)
    rb = router_bias.astype(F32).reshape(N_EXPERTS, 1)

    w_in_b, w_na_b, w_hg_b, w_out_b = (w.astype(BF16) for w in (w_in, w_na_o, w_hg_o, w_out))
    ff = w_gate.shape[-1]
    w_gate_b = w_gate.astype(BF16).reshape(depth * N_EXPERTS, d, ff)
    w_up_b = w_up.astype(BF16).reshape(depth * N_EXPERTS, d, ff)
    w_down_b = w_down.astype(BF16).reshape(depth * N_EXPERTS, ff, d)

    assert n_ctx == ROW_TILE
    stream = (ctx, x, 1)
    out = None
    for l in range(depth):
        last = l == depth - 1
        skip = n_ctx // ROW_TILE if last else 0
        n_tiles = b * (lt // ROW_TILE - skip) + N_BUCKETS
        q, k, v, hq, *gates, hi, og, ga, gh = _inproj(
            stream, lt, mods[l], norm1_g[l].reshape(1, d), w_in_b, l, cos_t, sin_t, lb_f[l], lb_b[l],
            ctx_row)
        o_na = _na_attention(q, k, v, na_rel_bias[l], n_ctx, with_ctx=not last)
        o_hg = _hgrn(hq, hi, gates[:3], gates[3:], n_ctx)
        xn, h2, bucket, rank, counts = _merge(
            o_na, o_hg, og, ga, gh, stream, mods[l], hg_norm_g[l].reshape(1, HG_DIM),
            norm2_g[l].reshape(1, d), w_na_b, w_hg_b, w_out_b, l, w_router_t, rb, ctx_row, skip)
        dest, pad_pos, tile_ea, tile_eb, n_used = _sorted_layout(
            bucket.reshape(-1), rank.reshape(-1), counts[:, 0], n_tiles)
        xs = _scatter_rows(h2, dest, pad_pos)
        ys = _moe(xs, tile_ea, tile_eb, n_used, w_router_rows, w_gate_b, w_up_b, w_down_b, l)
        res = _gather_residual(xn, mods[l], ys, dest, final_g.reshape(1, d), ctx_row, last, skip)
        if last:
            out = res
        else:
            stream = (res, res, 0)
    return out
```

```python
import functools
import math

import jax
import jax.numpy as jnp
import numpy as np
from jax import lax
from jax.experimental import pallas as pl
from jax.experimental.pallas import tpu as pltpu

F32 = jnp.float32
BF16 = jnp.bfloat16

GRID_W = 64
EPS = 1e-6
NA_HEADS = 8
NA_HEAD_DIM = 64
NA_WIDTH = NA_HEADS * NA_HEAD_DIM
NA_WIN_ROWS = 8
NA_WIN_COLS = 16
ROPE_THETA = 10000.0
HG_HEADS = 4
HG_DIM = 128
HG_WIDTH = HG_HEADS * HG_DIM
GATE_FLOOR = 1e-30
N_EXPERTS = 16
N_GROUPS = 4
EXPERTS_PER_GROUP = 4
N_PAIRS = 6
N_BUCKETS = N_GROUPS * N_PAIRS
BUCKET_ROWS = 32
MASKED = -1e30
LOG2E = 1.4426950408889634

LANES = 128
MXU_COLS = 256
ROW_TILE = 256
NA_Q_ROWS = 2
SUBLANES = 8
HG_CHUNK = 128
HG_UNROLL = 4
VMEM_LIMIT = 56 * 1024 * 1024

_NT = (((1,), (1,)), ((), ()))
_TN = (((0,), (0,)), ((), ()))


def _cparams(sem):
    return pltpu.CompilerParams(dimension_semantics=sem, vmem_limit_bytes=VMEM_LIMIT)


def _ada_kernel(c_ref, w_ref, b_ref, o_ref):
    cc = c_ref[...]
    s = cc * jax.nn.sigmoid(cc)
    o_ref[0] = jnp.dot(s, w_ref[0], preferred_element_type=F32) + b_ref[0]


def _ada(cc, w_ada, b_ada):
    depth, d, n = w_ada.shape
    rows = cc.shape[0]
    tn = 1536
    return pl.pallas_call(
        _ada_kernel,
        grid=(depth, n // tn),
        in_specs=[pl.BlockSpec((rows, d), lambda l, j: (0, 0)),
                  pl.BlockSpec((1, d, tn), lambda l, j: (l, 0, j)),
                  pl.BlockSpec((1, 1, tn), lambda l, j: (l, 0, j))],
        out_specs=pl.BlockSpec((1, rows, tn), lambda l, j: (l, 0, j)),
        out_shape=jax.ShapeDtypeStruct((depth, rows, n), F32),
        compiler_params=_cparams(("arbitrary", "arbitrary")),
        name="ada",
    )(cc, w_ada, b_ada.reshape(depth, 1, n))


def _inproj_kernel(x0_ref, x_ref, mod_ref, g_ref, w_ref, cos_ref, sin_ref, lbf_ref, lbb_ref,
                   q_ref, k_ref, v_ref, hq_ref, lfh_ref, lfl_ref, kf_ref, lbh_ref, lbl_ref, kb_ref,
                   hi_ref, og_ref, ga_ref, gh_ref):
    x = jnp.where(pl.program_id(1) == 0, x0_ref[0], x_ref[0])
    y = x * lax.rsqrt(jnp.mean(x * x, axis=-1, keepdims=True) + EPS) * g_ref[...]
    h = (y * (1.0 + mod_ref[0, 1:2, :]) + mod_ref[0, 0:1, :]).astype(BF16)

    def proj(off, width):
        return jnp.dot(h, w_ref[0, :, off:off + width], preferred_element_type=F32)

    cos = cos_ref[...]
    sin = sin_ref[...]
    even = (lax.broadcasted_iota(jnp.int32, cos.shape, 1) % 2) == 0

    def rope(z):
        swapped = jnp.where(even, pltpu.roll(z, LANES - 1, 1), pltpu.roll(z, 1, 1))
        return z * cos + swapped * sin

    scale = NA_HEAD_DIM ** -0.5 * LOG2E
    d = x.shape[-1]

    def rope_out(o_ref, mult):
        def epilogue(z, sl):
            for sub in range(MXU_COLS // LANES):
                src = slice(sub * LANES, (sub + 1) * LANES)
                dst = slice(sl.start + sub * LANES, sl.start + (sub + 1) * LANES)
                r = rope(z[:, src])
                o_ref[0, :, dst] = (r if mult is None else r * mult).astype(o_ref.dtype)
        return epilogue

    def plain_out(o_ref):
        def epilogue(z, sl):
            o_ref[0, :, sl] = z.astype(o_ref.dtype)
        return epilogue

    def silu_out(o_ref):
        def epilogue(z, sl):
            o_ref[0, :, sl] = (z * jax.nn.sigmoid(z)).astype(o_ref.dtype)
        return epilogue

    def gate_out(lb_ref, hi_ref, lo_ref, key_ref):
        def epilogue(f, sl):
            lb = lb_ref[:, sl]
            one_m_lb = 1.0 - lb
            g = lb + one_m_lb * jax.nn.sigmoid(f)
            lg = jnp.log(jnp.maximum(g, GATE_FLOOR)) * LOG2E
            hi = lg.astype(BF16)
            hi_ref[0, :, sl] = hi
            lo_ref[0, :, sl] = (lg - hi.astype(F32)).astype(BF16)
            key_ref[0, :, sl] = (one_m_lb * jax.nn.sigmoid(-f)).astype(BF16)
        return epilogue

    segments = [(NA_WIDTH, rope_out(q_ref, scale), True), (NA_WIDTH, rope_out(k_ref, None), True),
                (NA_WIDTH, plain_out(v_ref), False), (HG_WIDTH, silu_out(hq_ref), True),
                (HG_WIDTH, gate_out(lbf_ref, lfh_ref, lfl_ref, kf_ref), True),
                (HG_WIDTH, gate_out(lbb_ref, lbh_ref, lbl_ref, kb_ref), True),
                (HG_WIDTH, plain_out(hi_ref), False), (HG_WIDTH, plain_out(og_ref), False),
                (d, plain_out(ga_ref), False), (d, plain_out(gh_ref), False)]
    heavy, light, off = [], [], 0
    for width, epilogue, is_heavy in segments:
        for c in range(0, width, MXU_COLS):
            (heavy if is_heavy else light).append((off + c, slice(c, c + MXU_COLS), epilogue))
        off += width
    order = []
    while heavy or light:
        if heavy:
            order.append(heavy.pop(0))
        order.extend(light[:1])
        del light[:1]
    z = proj(order[0][0], MXU_COLS)
    for i, (_, sl, epilogue) in enumerate(order):
        z_next = proj(order[i + 1][0], MXU_COLS) if i + 1 < len(order) else None
        epilogue(z, sl)
        z = z_next


def _stream_specs(first, rest, n_first):
    d = first.shape[-1]
    return (pl.BlockSpec((1, ROW_TILE, d), lambda bi, j: (bi, 0, 0)),
            pl.BlockSpec((1, ROW_TILE, d), lambda bi, j: (bi, jnp.maximum(j - n_first, 0), 0)))


def _inproj(stream, lt, mods, g, w_all, layer, cos_t, sin_t, lb_f, lb_b, ctx_row):
    first, rest, n_first = stream
    b, _, d = first.shape
    n_t = lt // ROW_TILE

    def rows(width):
        return pl.BlockSpec((1, ROW_TILE, width), lambda bi, j: (bi, j, 0))

    def mod_map(bi, j):
        return (jnp.where(j == 0, ctx_row, bi), 0, 0)

    widths = [NA_WIDTH] * 3 + [HG_WIDTH] * 9 + [d, d]
    return pl.pallas_call(
        _inproj_kernel,
        grid=(b, n_t),
        in_specs=[*_stream_specs(first, rest, n_first),
                  pl.BlockSpec((1, 6, d), mod_map),
                  pl.BlockSpec((1, d), lambda bi, j: (0, 0)),
                  pl.BlockSpec((1,) + w_all.shape[1:], lambda bi, j: (layer, 0, 0)),
                  pl.BlockSpec((ROW_TILE, LANES), lambda bi, j: (j, 0)),
                  pl.BlockSpec((ROW_TILE, LANES), lambda bi, j: (j, 0)),
                  pl.BlockSpec((1, HG_WIDTH), lambda bi, j: (0, 0)),
                  pl.BlockSpec((1, HG_WIDTH), lambda bi, j: (0, 0))],
        out_specs=[rows(w) for w in widths],
        out_shape=[jax.ShapeDtypeStruct((b, lt, w), BF16) for w in widths],
        compiler_params=_cparams(("arbitrary", "arbitrary")),
        name="inproj",
    )(first, rest, mods, g, w_all, cos_t, sin_t, lb_f.reshape(1, HG_WIDTH), lb_b.reshape(1, HG_WIDTH))


def _rope_tables(n_ctx, seq):
    t = jnp.arange(seq, dtype=jnp.int32)
    row = (t // GRID_W).astype(F32)
    col = (t % GRID_W).astype(F32)
    rot_half = NA_HEAD_DIM // 2
    inv = ROPE_THETA ** (-jnp.arange(0, rot_half, 2, dtype=F32) / rot_half)
    ang = jnp.concatenate([row[:, None] * inv, col[:, None] * inv], axis=-1)
    cos = jnp.repeat(jnp.cos(ang), 2, axis=-1)
    sin = jnp.repeat(jnp.sin(ang), 2, axis=-1)
    sign = jnp.asarray(np.tile(np.array([-1.0, 1.0], np.float32), NA_HEAD_DIM // 2))
    sin = sin * sign
    cos = jnp.concatenate([jnp.ones((n_ctx, NA_HEAD_DIM), F32), cos], axis=0)
    sin = jnp.concatenate([jnp.zeros((n_ctx, NA_HEAD_DIM), F32), sin], axis=0)
    reps = LANES // NA_HEAD_DIM
    return jnp.tile(cos, (1, reps)), jnp.tile(sin, (1, reps))


def _na_plan(n_ctx, grid_rows):
    span = NA_WIN_ROWS + NA_Q_ROWS - 1
    variants, var_of_step, ws_of_step = [], [], []
    for r0 in range(0, grid_rows, NA_Q_ROWS):
        rs = [min(max(r0 + dq - NA_WIN_ROWS // 2, 0), grid_rows - NA_WIN_ROWS) for dq in range(NA_Q_ROWS)]
        ws = min(rs[0], grid_rows - span)
        key = (r0 - ws,) + tuple(r - ws for r in rs)
        if key not in variants:
            variants.append(key)
        var_of_step.append(variants.index(key))
        ws_of_step.append(ws)
    ctx_steps = n_ctx // (NA_Q_ROWS * GRID_W)
    var_of_step = [len(variants)] * ctx_steps + var_of_step
    ws_of_step = [0] * ctx_steps + ws_of_step
    return variants, np.asarray(var_of_step, np.int32), np.asarray(ws_of_step, np.int32)


def _na_bias_planes(rel_bias, variants):
    span = NA_WIN_ROWS + NA_Q_ROWS - 1
    n_dr, n_dc = 2 * NA_WIN_ROWS - 1, 2 * NA_WIN_COLS - 1
    col = np.arange(GRID_W)[:, None]
    kc = np.arange(GRID_W)[None, :]
    cs = np.clip(col - NA_WIN_COLS // 2, 0, GRID_W - NA_WIN_COLS)
    col_ok = (kc >= cs) & (kc < cs + NA_WIN_COLS)
    oh_c = ((kc - col + NA_WIN_COLS - 1)[..., None] == np.arange(n_dc)) & col_ok[..., None]
    planes = jnp.einsum("hrc,xkc->hrxk", rel_bias.astype(F32), jnp.asarray(oh_c.astype(np.float32)),
                        precision=lax.Precision.HIGHEST)
    planes = jnp.where(jnp.asarray(col_ok), planes * LOG2E, MASKED)
    planes = jnp.concatenate([planes, jnp.full_like(planes[:, :1], MASKED)], axis=1)
    plane_of = np.full((len(variants) + 1, NA_Q_ROWS, span), n_dr, np.int32)
    for vi, key in enumerate(variants):
        for dq in range(NA_Q_ROWS):
            for j in range(key[1 + dq], key[1 + dq] + NA_WIN_ROWS):
                plane_of[vi, dq, j] = j - (key[0] + dq) + NA_WIN_ROWS - 1
    return planes, plane_of.reshape(-1)


def _na_kernel(var_ref, ws_ref, plane_ref, q_ref, k_ref, v_ref, b_ref, o_ref, sw_ref, sc_ref, t_ref,
               *, step_off, n_ctx):
    nq = q_ref.shape[1]
    win = sw_ref.shape[1]
    span = win // GRID_W
    j = pl.program_id(1) + step_off
    var = var_ref[j]

    @pl.when((pl.program_id(0) == 0) & (pl.program_id(1) == 0))
    def _():
        for vi in range(t_ref.shape[0]):
            for h in range(NA_HEADS):
                for dq in range(nq // GRID_W):
                    base = (vi * (nq // GRID_W) + dq) * span
                    row = jnp.concatenate([b_ref[h, plane_ref[base + jj]] for jj in range(span)], axis=1)
                    t_ref[vi, h, dq * GRID_W:(dq + 1) * GRID_W, :] = row

    start = pl.multiple_of(n_ctx + ws_ref[j] * GRID_W, GRID_W)
    low = lax.broadcasted_iota(jnp.int32, (nq, LANES), 1) < NA_HEAD_DIM
    for hp in range(NA_HEADS // 2):
        sl = slice(hp * LANES, (hp + 1) * LANES)
        q2 = q_ref[0, :, sl]
        kw = k_ref[0, pl.ds(start, win), sl]
        kc = k_ref[0, 0:n_ctx, sl]
        zero = jnp.zeros_like(q2)
        qs = jnp.concatenate([jnp.where(low, q2, zero), jnp.where(low, zero, q2)], axis=0)
        rows = slice(2 * hp * nq, (2 * hp + 2) * nq)
        bias = t_ref[var, 2 * hp:2 * hp + 2].reshape(2 * nq, win)
        sw_ref[rows, :] = lax.dot_general(qs, kw, _NT, preferred_element_type=F32) + bias
        sc_ref[rows, :] = lax.dot_general(qs, kc, _NT, preferred_element_type=F32)
    sw = sw_ref[...]
    sc = sc_ref[...]
    m = jnp.maximum(jnp.max(sw, axis=-1, keepdims=True), jnp.max(sc, axis=-1, keepdims=True))
    pw = jnp.exp2(sw - m)
    pc = jnp.exp2(sc - m)
    inv = 1.0 / (jnp.sum(pw, axis=-1, keepdims=True) + jnp.sum(pc, axis=-1, keepdims=True))
    pw = pw.astype(BF16)
    pc = pc.astype(BF16)
    for hp in range(NA_HEADS // 2):
        sl = slice(hp * LANES, (hp + 1) * LANES)
        vw = v_ref[0, pl.ds(start, win), sl]
        vc = v_ref[0, 0:n_ctx, sl]
        rows = slice(2 * hp * nq, (2 * hp + 2) * nq)
        o2 = (jnp.dot(pw[rows], vw, preferred_element_type=F32)
              + jnp.dot(pc[rows], vc, preferred_element_type=F32)) * inv[rows]
        o_ref[0, :, sl] = jnp.where(low, o2[:nq], o2[nq:]).astype(o_ref.dtype)


def _na_attention(q, k, v, rel_bias, n_ctx, with_ctx):
    b, lt, _ = q.shape
    nq = NA_Q_ROWS * GRID_W
    span = NA_WIN_ROWS + NA_Q_ROWS - 1
    variants, var_of_step, ws_of_step = _na_plan(n_ctx, (lt - n_ctx) // GRID_W)
    planes, plane_of = _na_bias_planes(rel_bias, variants)
    step_off = 0 if with_ctx else n_ctx // nq
    n_steps = lt // nq - step_off

    kern = functools.partial(_na_kernel, step_off=step_off, n_ctx=n_ctx)
    grid_spec = pltpu.PrefetchScalarGridSpec(
        num_scalar_prefetch=3,
        grid=(b, n_steps),
        in_specs=[pl.BlockSpec((1, nq, NA_WIDTH), lambda bi, j, *_: (bi, j + step_off, 0)),
                  pl.BlockSpec((1, lt, NA_WIDTH), lambda bi, j, *_: (bi, 0, 0)),
                  pl.BlockSpec((1, lt, NA_WIDTH), lambda bi, j, *_: (bi, 0, 0)),
                  pl.BlockSpec(planes.shape, lambda bi, j, *_: (0, 0, 0, 0))],
        out_specs=pl.BlockSpec((1, nq, NA_WIDTH), lambda bi, j, *_: (bi, j, 0)),
        scratch_shapes=[pltpu.VMEM((NA_HEADS * nq, span * GRID_W), F32),
                        pltpu.VMEM((NA_HEADS * nq, n_ctx), F32),
                        pltpu.VMEM((len(variants) + 1, NA_HEADS, nq, span * GRID_W), F32)],
    )
    return pl.pallas_call(
        kern,
        grid_spec=grid_spec,
        out_shape=jax.ShapeDtypeStruct((b, n_steps * nq, NA_WIDTH), BF16),
        compiler_params=_cparams(("arbitrary", "arbitrary")),
        name="na_attention",
    )(jnp.asarray(var_of_step), jnp.asarray(ws_of_step), jnp.asarray(plane_of), q, k, v, planes)


def _block_ref_rows(b, w, rev, upper_fill=None, lower_fill=None):
    c, n = b.shape
    off = w if rev else w - 1
    if upper_fill is not None or lower_fill is not None:
        parts = []
        for s in range(0, c, 2 * w):
            ref = jnp.broadcast_to(b[s + off:s + off + 1], (w, n))
            parts.append(ref if lower_fill is None else jnp.full((w, n), lower_fill, b.dtype))
            parts.append(ref if upper_fill is None else jnp.full((w, n), upper_fill, b.dtype))
        return jnp.concatenate(parts, axis=0)
    if 2 * w >= SUBLANES:
        parts = [jnp.broadcast_to(b[s + off:s + off + 1], (2 * w, n)) for s in range(0, c, 2 * w)]
        return parts[0] if len(parts) == 1 else jnp.concatenate(parts, axis=0)
    b3 = b.reshape(c // SUBLANES, SUBLANES, n)
    sub = lax.broadcasted_iota(jnp.int32, b3.shape, 1)
    r = None
    for s in range(0, SUBLANES, 2 * w):
        piece = jnp.broadcast_to(b3[:, s + off:s + off + 1, :], b3.shape)
        r = piece if r is None else jnp.where(sub >= s, piece, r)
    return r.reshape(c, n)


def _hgrn_chunks(chains, states, lmask_ref):
    n = len(chains)
    c = chains[0][0].shape[0]
    qb = [ch[0] for ch in chains]
    kb = [ch[2] for ch in chains]
    b = []
    for i in range(n):
        (hi, lo), tri = chains[i][1], chains[i][4]
        b.append(jnp.dot(tri, hi, preferred_element_type=F32) + jnp.dot(tri, lo, preferred_element_type=F32))

    states = list(states)
    o = []
    for i in range(n):
        vb, rev, sid = chains[i][3], chains[i][7], chains[i][8]
        b_last = b[i][0:1] if rev else b[i][c - 1:c]
        qe = qb[i] * jnp.exp2(b[i]).astype(BF16)
        ke = kb[i] * jnp.exp2(b_last - b[i]).astype(BF16)
        st = states[sid]
        o.append(lax.dot_general(qe, st.astype(BF16), _NT, preferred_element_type=F32))
        states[sid] = st * jnp.exp2(b_last) + lax.dot_general(vb, ke, _TN, preferred_element_type=F32)

    a = [None] * n
    w = c // 2
    for li in range(lmask_ref.shape[0]):
        for i in range(n):
            negq_ref, negk_ref, rev = chains[i][5], chains[i][6], chains[i][7]
            if w >= SUBLANES:
                eq = b[i] - _block_ref_rows(b[i], w, rev, -MASKED if rev else None, None if rev else -MASKED)
                ek = _block_ref_rows(b[i], w, rev, None if rev else MASKED, MASKED if rev else None) - b[i]
                kw = kb[i] * jnp.exp2(ek).astype(BF16)
            elif w > 1:
                d = b[i] - _block_ref_rows(b[i], w, rev)
                eq = d + negq_ref[li]
                kw = kb[i] * jnp.exp2(negk_ref[li] - d).astype(BF16)
            else:
                hi, lo = chains[i][1]
                eq = hi.astype(F32) + lo.astype(F32) + negq_ref[li]
                kw = kb[i] * chains[i][9]
            qw = qb[i] * jnp.exp2(eq).astype(BF16)
            p = lax.dot_general(qw, kw, _NT, preferred_element_type=F32).astype(BF16) * lmask_ref[li]
            a[i] = p if a[i] is None else a[i] + p
        w //= 2
    for i in range(n):
        vb = chains[i][3]
        o[i] = o[i] + jnp.dot(a[i], vb, preferred_element_type=F32)
        diag = jnp.sum(qb[i].astype(F32) * kb[i].astype(F32), axis=-1, keepdims=True)
        o[i] = o[i] + diag * vb.astype(F32)
    return o, states


def _hgrn_kernel(hq_ref, hi_ref, lfh_ref, lfl_ref, kf_ref, lbh_ref, lbl_ref, kb_ref, trif_ref, trir_ref,
                 negqf_ref, negkf_ref, negqr_ref, negkr_ref, lmask_ref, konef_ref, koner_ref,
                 o_ref, sf_ref, sb_ref, of_ref, ob_ref, *, n_ctx_chunks, n_chunks):
    sf_ref[...] = jnp.zeros_like(sf_ref)
    sb_ref[...] = jnp.zeros_like(sb_ref)

    def steps(cf0, cb0, unroll):
        chains, rows = [], []
        for u in range(unroll):
            rf = pl.ds(pl.multiple_of((cf0 + u) * HG_CHUNK, HG_CHUNK), HG_CHUNK)
            rb = pl.ds(pl.multiple_of((cb0 - u) * HG_CHUNK, HG_CHUNK), HG_CHUNK)
            chains.append((hq_ref[0, rf, :], (lfh_ref[0, rf, :], lfl_ref[0, rf, :]), kf_ref[0, rf, :],
                           hi_ref[0, rf, :], trif_ref[...], negqf_ref, negkf_ref, False, 0, konef_ref[...]))
            chains.append((hq_ref[0, rb, :], (lbh_ref[0, rb, :], lbl_ref[0, rb, :]), kb_ref[0, rb, :],
                           hi_ref[0, rb, :], trir_ref[...], negqr_ref, negkr_ref, True, 1, koner_ref[...]))
            rows += [rf, rb]
        outs, (sf, sb) = _hgrn_chunks(chains, [sf_ref[...], sb_ref[...]], lmask_ref)
        for i, r in enumerate(rows):
            if i % 2 == 0:
                of_ref[r, :] = outs[i]
            else:
                ob_ref[r, :] = outs[i]
        sf_ref[...] = sf
        sb_ref[...] = sb

    n_lat = n_chunks - n_ctx_chunks
    u_ctx = math.gcd(HG_UNROLL, n_ctx_chunks)
    u_lat = math.gcd(HG_UNROLL, n_lat)

    def ctx_body(i, carry):
        steps(i * u_ctx, n_ctx_chunks - 1 - i * u_ctx, u_ctx)
        return carry

    def lat_body(i, carry):
        steps(n_ctx_chunks + i * u_lat, n_chunks - 1 - i * u_lat, u_lat)
        return carry

    lax.fori_loop(0, n_ctx_chunks // u_ctx, ctx_body, 0)
    lax.fori_loop(0, n_lat // u_lat, lat_body, 0)
    o_ref[0] = (of_ref[...] + ob_ref[...]).astype(o_ref.dtype)


def _hgrn_level_constants():
    c = HG_CHUNK
    t = np.arange(c)
    xor = t[:, None] ^ t[None, :]
    lmask, negq_f, negq_r = [], [], []
    w = c // 2
    while w >= 1:
        lmask.append(((xor >= w) & (xor < 2 * w)).astype(np.float32))
        upper = (t % (2 * w)) >= w
        negq_f.append(np.where(upper, 0.0, MASKED))
        negq_r.append(np.where(upper, MASKED, 0.0))
        w //= 2

    def rows(m):
        return jnp.asarray(np.broadcast_to(np.stack(m)[:, :, None], (len(m), c, HG_DIM)).astype(np.float32))

    tri_f = (t[None, :] <= t[:, None]).astype(np.float32)
    odd = np.broadcast_to((t % 2 == 1)[:, None], (c, HG_DIM)).astype(np.float32)
    return (jnp.asarray(tri_f, BF16), jnp.asarray(tri_f.T, BF16),
            rows(negq_f), rows(negq_r), rows(negq_r), rows(negq_f), jnp.asarray(np.stack(lmask), BF16),
            jnp.asarray(1.0 - odd, BF16), jnp.asarray(odd, BF16))


def _hgrn(hq, hi, fwd, bwd, n_ctx):
    b, lt, _ = hq.shape
    kern = functools.partial(_hgrn_kernel, n_ctx_chunks=n_ctx // HG_CHUNK, n_chunks=lt // HG_CHUNK)
    seq = pl.BlockSpec((1, lt, HG_DIM), lambda bi, h: (bi, 0, h))
    consts = _hgrn_level_constants()
    return pl.pallas_call(
        kern,
        grid=(b, HG_HEADS),
        in_specs=[seq] * 8
        + [pl.BlockSpec(a.shape, lambda bi, h, nd=a.ndim: (0,) * nd) for a in consts],
        out_specs=seq,
        out_shape=jax.ShapeDtypeStruct((b, lt, HG_WIDTH), BF16),
        scratch_shapes=[pltpu.VMEM((HG_DIM, HG_DIM), F32), pltpu.VMEM((HG_DIM, HG_DIM), F32),
                        pltpu.VMEM((lt, HG_DIM), F32), pltpu.VMEM((lt, HG_DIM), F32)],
        compiler_params=_cparams(("arbitrary", "arbitrary")),
        name="hgrn2",
    )(hq, hi, *fwd, *bwd, *consts)


def _split2(x):
    hi = x.astype(BF16)
    lo = (x - hi.astype(F32)).astype(BF16)
    return hi, lo


def _route(logits, bias):
    biased = jax.nn.sigmoid(logits) + bias
    rows = [biased[e:e + 1] for e in range(N_EXPERTS)]
    n = EXPERTS_PER_GROUP
    best = None
    g_sel = None
    for g in range(N_GROUPS):
        gs = None
        for i in range(n):
            for j2 in range(i + 1, n):
                pair = rows[g * n + i] + rows[g * n + j2]
                gs = pair if gs is None else jnp.maximum(gs, pair)
        if best is None:
            best, g_sel = gs, jnp.zeros(gs.shape, jnp.int32)
        else:
            take = gs > best
            best = jnp.where(take, gs, best)
            g_sel = jnp.where(take, g, g_sel)
    cand = []
    for i in range(n):
        c_i = rows[(N_GROUPS - 1) * n + i]
        for g in range(N_GROUPS - 2, -1, -1):
            c_i = jnp.where(g_sel == g, rows[g * n + i], c_i)
        cand.append(c_i)
    m1, i1 = cand[0], jnp.zeros(best.shape, jnp.int32)
    for i in range(1, n):
        take = cand[i] > m1
        m1 = jnp.where(take, cand[i], m1)
        i1 = jnp.where(take, i, i1)
    m2 = jnp.full(best.shape, -jnp.inf, F32)
    i2 = jnp.zeros(best.shape, jnp.int32)
    for i in range(n):
        take = (i1 != i) & (cand[i] > m2)
        m2 = jnp.where(take, cand[i], m2)
        i2 = jnp.where(take, i, i2)
    lo = jnp.minimum(i1, i2)
    hi = jnp.maximum(i1, i2)
    pair = jnp.where(lo == 0, hi - 1, jnp.where(lo == 1, hi + 1, N_PAIRS - 1))
    return g_sel * N_PAIRS + pair


def _merge_kernel(ona_ref, ohg_ref, og_ref, ga_ref, gh_ref, x0_ref, x_ref, mod_ref, hgg_ref, n2g_ref,
                  wna_ref, whg_ref, wout_ref, wr_ref, rb_ref,
                  xn_ref, h2_ref, bucket_ref, rank_ref, cnt_ref, carry_ref, hprev_ref,
                  *, n_t, n_tot, skip):
    step = pl.program_id(0)
    first_tile = (jnp.minimum(step, n_tot - 1) % n_t + skip) == 0

    @pl.when(step == 0)
    def _():
        carry_ref[...] = jnp.zeros_like(carry_ref)
        hprev_ref[...] = jnp.zeros_like(hprev_ref)

    w_hi, w_lo = _split2(wr_ref[...])
    h_hi, h_lo = _split2(hprev_ref[...])
    logits = (lax.dot_general(w_hi, h_hi, _NT, preferred_element_type=F32)
              + lax.dot_general(w_hi, h_lo, _NT, preferred_element_type=F32)
              + lax.dot_general(w_lo, h_hi, _NT, preferred_element_type=F32))

    ohg = ohg_ref[0].astype(F32)
    og = og_ref[0].astype(F32)
    gain = hgg_ref[...]
    heads = []
    for h in range(HG_HEADS):
        sl = slice(h * HG_DIM, (h + 1) * HG_DIM)
        oh = ohg[:, sl]
        yh = oh * lax.rsqrt(jnp.mean(oh * oh, axis=-1, keepdims=True) + EPS) * gain
        gt = og[:, sl]
        heads.append((yh * (gt * jax.nn.sigmoid(gt))).astype(BF16))
    hn = jnp.concatenate(heads, axis=-1)
    y_na = jnp.dot(ona_ref[0], wna_ref[0], preferred_element_type=F32)
    y_hg = jnp.dot(hn, whg_ref[0], preferred_element_type=F32)

    bucket = _route(logits, rb_ref[...])
    bucket_ref[0] = bucket

    m = jax.nn.sigmoid(ga_ref[0].astype(F32)) * y_na + jax.nn.sigmoid(gh_ref[0].astype(F32)) * y_hg
    y = jnp.dot(m.astype(BF16), wout_ref[0], preferred_element_type=F32)

    t = bucket.shape[1]
    onehot = (lax.broadcasted_iota(jnp.int32, (BUCKET_ROWS, t), 0) == bucket).astype(F32)
    before = (lax.broadcasted_iota(jnp.int32, (t, t), 0)
              < lax.broadcasted_iota(jnp.int32, (t, t), 1)).astype(BF16)
    prefix = jnp.dot(onehot.astype(BF16), before, preferred_element_type=F32)
    carry = carry_ref[...]
    rank = jnp.sum(onehot * (prefix + carry[:, 0:1]), axis=0, keepdims=True)
    rank_ref[0] = rank.astype(jnp.int32)
    live = (step > 0).astype(F32)
    carry = carry + live * jnp.sum(onehot, axis=1, keepdims=True)
    carry_ref[...] = carry
    cnt_ref[...] = carry.astype(jnp.int32)

    xn = jnp.where(first_tile, x0_ref[0], x_ref[0]) + mod_ref[0, 2:3, :] * y
    xn_ref[0] = xn
    yn = xn * lax.rsqrt(jnp.mean(xn * xn, axis=-1, keepdims=True) + EPS) * n2g_ref[...]
    h2 = yn * (1.0 + mod_ref[0, 4:5, :]) + mod_ref[0, 3:4, :]
    h2_ref[...] = h2
    hprev_ref[...] = h2


def _merge(o_na, o_hg, og, ga, gh, stream, mods, hg_gain, n2_gain, w_na_o, w_hg_o, w_out, layer,
           w_router_t, router_bias, ctx_row, skip):
    first, rest, n_first = stream
    b, lt, _ = o_hg.shape
    d = first.shape[-1]
    n_t = lt // ROW_TILE - skip
    n_tot = b * n_t

    def tile(s):
        t = jnp.minimum(s, n_tot - 1)
        return t // n_t, t % n_t

    def rows(width):
        return pl.BlockSpec((1, ROW_TILE, width), lambda s: (tile(s)[0], tile(s)[1] + skip, 0))

    def full(a):
        return pl.BlockSpec(a.shape, lambda s: (0,) * a.ndim)

    def of_layer(a):
        return pl.BlockSpec((1,) + a.shape[1:], lambda s: (layer,) + (0,) * (a.ndim - 1))

    def mod_map(s):
        bi, j = tile(s)
        return (jnp.where(j + skip == 0, ctx_row, bi), 0, 0)

    tok = pl.BlockSpec((1, 1, ROW_TILE), lambda s: (jnp.maximum(s - 1, 0), 0, 0))
    consts = [hg_gain, n2_gain, w_na_o, w_hg_o, w_out, w_router_t, router_bias]
    const_specs = [full(hg_gain), full(n2_gain), of_layer(w_na_o), of_layer(w_hg_o), of_layer(w_out),
                   full(w_router_t), full(router_bias)]
    kern = functools.partial(_merge_kernel, n_t=n_t, n_tot=n_tot, skip=skip)
    return pl.pallas_call(
        kern,
        grid=(n_tot + 1,),
        in_specs=[pl.BlockSpec((1, ROW_TILE, NA_WIDTH), lambda s: tile(s) + (0,)),
                  rows(HG_WIDTH), rows(HG_WIDTH), rows(d), rows(d),
                  pl.BlockSpec((1, ROW_TILE, d), lambda s: (tile(s)[0], 0, 0)),
                  pl.BlockSpec((1, ROW_TILE, d),
                               lambda s: (tile(s)[0], jnp.maximum(tile(s)[1] + skip - n_first, 0), 0)),
                  pl.BlockSpec((1, 6, d), mod_map)] + const_specs,
        out_specs=[pl.BlockSpec((1, ROW_TILE, d), lambda s: tile(s) + (0,)),
                   pl.BlockSpec((ROW_TILE, d), lambda s: (jnp.minimum(s, n_tot - 1), 0)),
                   tok, tok,
                   pl.BlockSpec((BUCKET_ROWS, LANES), lambda s: (0, 0))],
        out_shape=[jax.ShapeDtypeStruct((b, n_t * ROW_TILE, d), F32),
                   jax.ShapeDtypeStruct((b * n_t * ROW_TILE, d), F32),
                   jax.ShapeDtypeStruct((b * n_t, 1, ROW_TILE), jnp.int32),
                   jax.ShapeDtypeStruct((b * n_t, 1, ROW_TILE), jnp.int32),
                   jax.ShapeDtypeStruct((BUCKET_ROWS, LANES), jnp.int32)],
        scratch_shapes=[pltpu.VMEM((BUCKET_ROWS, LANES), F32), pltpu.VMEM((ROW_TILE, d), F32)],
        compiler_params=_cparams(("arbitrary",)),
        name="merge_router",
    )(o_na, o_hg, og, ga, gh, first, rest, mods, *consts)


def _start_row_copies(n, row_copy):
    for r in range(n):
        row_copy(r).start()


def _scatter_kernel(dest_ref, h_ref, xs_ref, zero_ref, sem, *, n_token_steps):
    n = h_ref.shape[0]
    step = pl.program_id(0)

    @pl.when(step < n_token_steps)
    def _():
        _start_row_copies(n, lambda r: pltpu.make_async_copy(
            h_ref.at[pl.ds(r, 1)], xs_ref.at[pl.ds(dest_ref[0, 0, r], 1)], sem))

    @pl.when(step >= n_token_steps)
    def _():
        zero_ref[...] = jnp.zeros_like(zero_ref)
        _start_row_copies(n, lambda r: pltpu.make_async_copy(
            zero_ref.at[pl.ds(0, 1)], xs_ref.at[pl.ds(dest_ref[0, 0, r], 1)], sem))

    pltpu.make_async_copy(h_ref, xs_ref.at[pl.ds(0, n)], sem).wait()


def _scatter_rows(h2, dest, pad_pos):
    t, d = h2.shape
    tile = 2 * ROW_TILE if t % (2 * ROW_TILE) == 0 and pad_pos.shape[0] % (2 * ROW_TILE) == 0 else ROW_TILE
    n_t = t // tile
    n_steps = n_t + pad_pos.shape[0] // tile
    kern = functools.partial(_scatter_kernel, n_token_steps=n_t)
    return pl.pallas_call(
        kern,
        grid=(n_steps,),
        in_specs=[pl.BlockSpec((1, 1, tile), lambda i: (i, 0, 0), memory_space=pltpu.SMEM),
                  pl.BlockSpec((tile, d), lambda i: (jnp.minimum(i, n_t - 1), 0))],
        out_specs=pl.BlockSpec(memory_space=pl.ANY),
        out_shape=jax.ShapeDtypeStruct((n_steps * tile, d), F32),
        scratch_shapes=[pltpu.VMEM((SUBLANES, d), F32), pltpu.SemaphoreType.DMA(())],
        compiler_params=_cparams(("arbitrary",)),
        name="scatter_rows",
    )(jnp.concatenate([dest, pad_pos]).reshape(n_steps, 1, tile), h2)


def _final_kernel(dest_ref, dest_next_ref, xn_ref, mod_ref, g_ref, ys_ref, o_ref, buf_ref, sems,
                  *, final_norm):
    n = buf_ref.shape[1]
    step = pl.program_id(0) * pl.num_programs(1) + pl.program_id(1)
    n_steps = pl.num_programs(0) * pl.num_programs(1)

    def gather(d_ref, slot):
        _start_row_copies(n, lambda r: pltpu.make_async_copy(
            ys_ref.at[pl.ds(d_ref[0, 0, r], 1)], buf_ref.at[slot, pl.ds(r, 1)], sems.at[slot]))

    @pl.when(step == 0)
    def _():
        gather(dest_ref, 0)

    @pl.when(step + 1 < n_steps)
    def _():
        gather(dest_next_ref, (step + 1) % 2)

    slot = step % 2
    pltpu.make_async_copy(ys_ref.at[pl.ds(0, n)], buf_ref.at[slot], sems.at[slot]).wait()
    x = xn_ref[0] + mod_ref[0, 5:6, :] * buf_ref[slot]
    if final_norm:
        x = x * lax.rsqrt(jnp.mean(x * x, axis=-1, keepdims=True) + EPS) * g_ref[...]
    o_ref[0] = x


def _gather_residual(xn, mods, ys, dest, gain, ctx_row, final_norm, skip):
    b, lt, d = xn.shape
    n_t = lt // ROW_TILE

    def mod_map(bi, j):
        return (jnp.where(j + skip == 0, ctx_row, bi), 0, 0)

    kern = functools.partial(_final_kernel, final_norm=final_norm)
    last_tile = b * n_t - 1
    dest3 = dest.reshape(b * n_t, 1, ROW_TILE)
    return pl.pallas_call(
        kern,
        grid=(b, n_t),
        in_specs=[pl.BlockSpec((1, 1, ROW_TILE), lambda bi, j: (bi * n_t + j, 0, 0),
                               memory_space=pltpu.SMEM),
                  pl.BlockSpec((1, 1, ROW_TILE), lambda bi, j: (jnp.minimum(bi * n_t + j + 1, last_tile), 0, 0),
                               memory_space=pltpu.SMEM),
                  pl.BlockSpec((1, ROW_TILE, d), lambda bi, j: (bi, j, 0)),
                  pl.BlockSpec((1, 6, d), mod_map),
                  pl.BlockSpec((1, d), lambda bi, j: (0, 0)),
                  pl.BlockSpec(memory_space=pl.ANY)],
        out_specs=pl.BlockSpec((1, ROW_TILE, d), lambda bi, j: (bi, j, 0)),
        out_shape=jax.ShapeDtypeStruct((b, lt, d), F32),
        scratch_shapes=[pltpu.VMEM((2, ROW_TILE, d), F32), pltpu.SemaphoreType.DMA((2,))],
        compiler_params=_cparams(("arbitrary", "arbitrary")),
        name="gather_residual",
    )(dest3, dest3, xn, mods, gain, ys)


def _moe_kernel(ea_ref, eb_ref, nused_ref, xs_ref, wra_ref, wrb_ref,
                wga_ref, wua_ref, wda_ref, wgb_ref, wub_ref, wdb_ref, ys_ref):
    del ea_ref, eb_ref
    i = pl.program_id(0)

    @pl.when(i < nused_ref[0])
    def _():
        x = xs_ref[...]
        s_a = jax.nn.sigmoid(jnp.sum(x * wra_ref[0], axis=-1, keepdims=True))
        s_b = jax.nn.sigmoid(jnp.sum(x * wrb_ref[0], axis=-1, keepdims=True))
        tot = s_a + s_b
        xb = x.astype(BF16)

        gate_a = jnp.dot(xb, wga_ref[0], preferred_element_type=F32)
        up_a = jnp.dot(xb, wua_ref[0], preferred_element_type=F32)
        gate_b = jnp.dot(xb, wgb_ref[0], preferred_element_type=F32)
        hid_a = (gate_a * jax.nn.sigmoid(gate_a) * up_a).astype(BF16)
        up_b = jnp.dot(xb, wub_ref[0], preferred_element_type=F32)
        y_a = jnp.dot(hid_a, wda_ref[0], preferred_element_type=F32)
        hid_b = (gate_b * jax.nn.sigmoid(gate_b) * up_b).astype(BF16)
        y_b = jnp.dot(hid_b, wdb_ref[0], preferred_element_type=F32)
        ys_ref[...] = (s_a / tot) * y_a + (s_b / tot) * y_b

    @pl.when(i >= nused_ref[0])
    def _():
        ys_ref[...] = jnp.zeros_like(ys_ref)


def _moe(xs, tile_ea, tile_eb, n_used, w_router_rows, w_gate, w_up, w_down, layer):
    n_sorted, d = xs.shape
    n_tiles = n_sorted // ROW_TILE
    ff = w_gate.shape[-1]
    base = layer * N_EXPERTS

    def by_a(i, ea, eb, nu):
        return (ea[i], 0, 0)

    def by_b(i, ea, eb, nu):
        return (eb[i], 0, 0)

    def wt_a(i, ea, eb, nu):
        return (base + ea[i], 0, 0)

    def wt_b(i, ea, eb, nu):
        return (base + eb[i], 0, 0)

    grid_spec = pltpu.PrefetchScalarGridSpec(
        num_scalar_prefetch=3,
        grid=(n_tiles,),
        in_specs=[pl.BlockSpec((ROW_TILE, d), lambda i, ea, eb, nu: (i, 0)),
                  pl.BlockSpec((1, 1, d), by_a), pl.BlockSpec((1, 1, d), by_b),
                  pl.BlockSpec((1, d, ff), wt_a), pl.BlockSpec((1, d, ff), wt_a),
                  pl.BlockSpec((1, ff, d), wt_a),
                  pl.BlockSpec((1, d, ff), wt_b), pl.BlockSpec((1, d, ff), wt_b),
                  pl.BlockSpec((1, ff, d), wt_b)],
        out_specs=pl.BlockSpec((ROW_TILE, d), lambda i, ea, eb, nu: (i, 0)),
    )
    return pl.pallas_call(
        _moe_kernel,
        grid_spec=grid_spec,
        out_shape=jax.ShapeDtypeStruct((n_sorted, d), F32),
        compiler_params=_cparams(("arbitrary",)),
        name="moe_pairs",
    )(tile_ea, tile_eb, n_used, xs, w_router_rows, w_router_rows,
      w_gate, w_up, w_down, w_gate, w_up, w_down)


_PAIR_LO = np.array([0, 0, 0, 1, 1, 2], np.int32)
_PAIR_HI = np.array([1, 2, 3, 2, 3, 3], np.int32)


def _sorted_layout(bucket, rank, counts, n_tiles):
    counts = counts[:N_BUCKETS]
    padded = ((counts + ROW_TILE - 1) // ROW_TILE) * ROW_TILE
    ends = jnp.cumsum(padded)
    starts = ends - padded

    def lookup(table, idx):
        hit = jnp.arange(table.shape[0], dtype=jnp.int32)[:, None] == idx[None, :]
        return jnp.sum(jnp.where(hit, table[:, None], 0), axis=0)

    dest = lookup(starts, bucket) + rank
    tile_start = jnp.arange(n_tiles, dtype=jnp.int32) * ROW_TILE
    tile_bucket = jnp.sum((ends[None, :] <= tile_start[:, None]).astype(jnp.int32), axis=1)
    tile_bucket = jnp.minimum(tile_bucket, N_BUCKETS - 1)
    group = tile_bucket // N_PAIRS
    pair = tile_bucket % N_PAIRS
    ea = group * EXPERTS_PER_GROUP + jnp.asarray(_PAIR_LO)[pair]
    eb = group * EXPERTS_PER_GROUP + jnp.asarray(_PAIR_HI)[pair]
    n_used = (ends[-1] // ROW_TILE).astype(jnp.int32).reshape(1)
    seg_start = jnp.concatenate([starts + counts, ends[-1:]])
    seg_len = jnp.concatenate([padded - counts, n_tiles * ROW_TILE - ends[-1:]])
    seg_end = jnp.cumsum(seg_len)
    k = jnp.arange(n_tiles * ROW_TILE - bucket.shape[0], dtype=jnp.int32)
    seg = jnp.sum((seg_end[:, None] <= k[None, :]).astype(jnp.int32), axis=0)
    pad_pos = lookup(seg_start - (seg_end - seg_len), seg) + k
    return (dest.astype(jnp.int32), pad_pos.astype(jnp.int32), ea.astype(jnp.int32), eb.astype(jnp.int32),
            n_used)


def _lower_bounds(raw):
    p = jax.nn.softmax(raw.astype(F32), axis=0)
    return jnp.cumsum(p, axis=0) - p[0:1]


def kernel(x, c, ctx, c_ctx, w_ada, b_ada, norm1_g, w_in, na_rel_bias, hg_lower_fwd, hg_lower_bwd,
           hg_norm_g, w_na_o, w_hg_o, w_out, norm2_g, w_router, router_bias, w_gate, w_up, w_down,
           final_g):
    b, seq, d = x.shape
    n_ctx = ctx.shape[1]
    depth = w_ada.shape[0]
    lt = n_ctx + seq
    assert n_ctx % ROW_TILE == 0 and seq % ROW_TILE == 0 and seq % GRID_W == 0
    assert n_ctx % HG_CHUNK == 0 and seq % HG_CHUNK == 0
    assert seq // GRID_W >= NA_WIN_ROWS + NA_Q_ROWS - 1 and (seq // GRID_W) % NA_Q_ROWS == 0
    assert n_ctx % (NA_Q_ROWS * GRID_W) == 0

    ada_rows = -(-(b + 1) // 8) * 8
    ctx_row = b
    cc = jnp.concatenate([c, c_ctx[None, :], jnp.zeros((ada_rows - b - 1, d), F32)], axis=0)
    mods = _ada(cc, w_ada, b_ada).reshape(depth, ada_rows, 6, d)

    lb_f = _lower_bounds(hg_lower_fwd)
    lb_b = _lower_bounds(hg_lower_bwd)
    cos_t, sin_t = _rope_tables(n_ctx, seq)
    w_router_t = jnp.transpose(w_router)
    w_router_rows = w_router_t.reshape(N_EXPERTS, 1, d)
    rb = router_bias.astype(F32).reshape(N_EXPERTS, 1)

    w_in_b, w_na_b, w_hg_b, w_out_b = (w.astype(BF16) for w in (w_in, w_na_o, w_hg_o, w_out))
    ff = w_gate.shape[-1]
    w_gate_b = w_gate.astype(BF16).reshape(depth * N_EXPERTS, d, ff)
    w_up_b = w_up.astype(BF16).reshape(depth * N_EXPERTS, d, ff)
    w_down_b = w_down.astype(BF16).reshape(depth * N_EXPERTS, ff, d)

    assert n_ctx == ROW_TILE
    stream = (ctx, x, 1)
    out = None
    for l in range(depth):
        last = l == depth - 1
        skip = n_ctx // ROW_TILE if last else 0
        n_tiles = b * (lt // ROW_TILE - skip) + N_BUCKETS
        q, k, v, hq, *gates, hi, og, ga, gh = _inproj(
            stream, lt, mods[l], norm1_g[l].reshape(1, d), w_in_b, l, cos_t, sin_t, lb_f[l], lb_b[l],
            ctx_row)
        o_na = _na_attention(q, k, v, na_rel_bias[l], n_ctx, with_ctx=not last)
        o_hg = _hgrn(hq, hi, gates[:3], gates[3:], n_ctx)
        xn, h2, bucket, rank, counts = _merge(
            o_na, o_hg, og, ga, gh, stream, mods[l], hg_norm_g[l].reshape(1, HG_DIM),
            norm2_g[l].reshape(1, d), w_na_b, w_hg_b, w_out_b, l, w_router_t, rb, ctx_row, skip)
        dest, pad_pos, tile_ea, tile_eb, n_used = _sorted_layout(
            bucket.reshape(-1), rank.reshape(-1), counts[:, 0], n_tiles)
        xs = _scatter_rows(h2, dest, pad_pos)
        ys = _moe(xs, tile_ea, tile_eb, n_used, w_router_rows, w_gate_b, w_up_b, w_down_b, l)
        res = _gather_residual(xn, mods[l], ys, dest, final_g.reshape(1, d), ctx_row, last, skip)
        if last:
            out = res
        else:
            stream = (res, res, 0)
    return out
```

```python
import functools
import math

import jax
import jax.numpy as jnp
import numpy as np
from jax import lax
from jax.experimental import pallas as pl
from jax.experimental.pallas import tpu as pltpu

F32 = jnp.float32
BF16 = jnp.bfloat16

GRID_W = 64
EPS = 1e-6
NA_HEADS = 8
NA_HEAD_DIM = 64
NA_WIDTH = NA_HEADS * NA_HEAD_DIM
NA_WIN_ROWS = 8
NA_WIN_COLS = 16
ROPE_THETA = 10000.0
HG_HEADS = 4
HG_DIM = 128
HG_WIDTH = HG_HEADS * HG_DIM
GATE_FLOOR = 1e-30
N_EXPERTS = 16
N_GROUPS = 4
EXPERTS_PER_GROUP = 4
N_PAIRS = 6
N_BUCKETS = N_GROUPS * N_PAIRS
BUCKET_ROWS = 32
MASKED = -1e30
LOG2E = 1.4426950408889634

LANES = 128
MXU_COLS = 256
ROW_TILE = 256
ADA_COLS = 1536
NA_Q_ROWS = 2
SUBLANES = 8
HG_CHUNK = 128
HG_UNROLL = 4
VMEM_LIMIT = 56 * 1024 * 1024

_NT = (((1,), (1,)), ((), ()))
_TN = (((0,), (0,)), ((), ()))


def _cparams(sem):
    return pltpu.CompilerParams(dimension_semantics=sem, vmem_limit_bytes=VMEM_LIMIT)


def _ada_kernel(c_ref, w_ref, b_ref, o_ref):
    cc = c_ref[...]
    s = cc * jax.nn.sigmoid(cc)
    o_ref[0] = jnp.dot(s, w_ref[0], preferred_element_type=F32) + b_ref[0]


def _ada(cc, w_ada, b_ada):
    depth, d, n = w_ada.shape
    rows = cc.shape[0]
    tn = ADA_COLS
    return pl.pallas_call(
        _ada_kernel,
        grid=(depth, n // tn),
        in_specs=[pl.BlockSpec((rows, d), lambda l, j: (0, 0)),
                  pl.BlockSpec((1, d, tn), lambda l, j: (l, 0, j)),
                  pl.BlockSpec((1, 1, tn), lambda l, j: (l, 0, j))],
        out_specs=pl.BlockSpec((1, rows, tn), lambda l, j: (l, 0, j)),
        out_shape=jax.ShapeDtypeStruct((depth, rows, n), F32),
        compiler_params=_cparams(("arbitrary", "arbitrary")),
        name="ada",
    )(cc, w_ada, b_ada.reshape(depth, 1, n))


def _inproj_kernel(x0_ref, x_ref, mod_ref, g_ref, w_ref, cos_ref, sin_ref, lbf_ref, lbb_ref,
                   q_ref, k_ref, v_ref, hq_ref, lfh_ref, lfl_ref, kf_ref, lbh_ref, lbl_ref, kb_ref,
                   hi_ref, og_ref, ga_ref, gh_ref):
    x = jnp.where(pl.program_id(1) == 0, x0_ref[0], x_ref[0])
    y = x * lax.rsqrt(jnp.mean(x * x, axis=-1, keepdims=True) + EPS) * g_ref[...]
    h = (y * (1.0 + mod_ref[0, 1:2, :]) + mod_ref[0, 0:1, :]).astype(BF16)

    def proj(off, width):
        return jnp.dot(h, w_ref[0, :, off:off + width], preferred_element_type=F32)

    cos = cos_ref[...]
    sin = sin_ref[...]
    even = (lax.broadcasted_iota(jnp.int32, cos.shape, 1) % 2) == 0

    def rope(z):
        swapped = jnp.where(even, pltpu.roll(z, LANES - 1, 1), pltpu.roll(z, 1, 1))
        return z * cos + swapped * sin

    scale = NA_HEAD_DIM ** -0.5 * LOG2E
    d = x.shape[-1]

    def rope_out(o_ref, mult):
        def epilogue(z, sl):
            for sub in range(MXU_COLS // LANES):
                src = slice(sub * LANES, (sub + 1) * LANES)
                dst = slice(sl.start + sub * LANES, sl.start + (sub + 1) * LANES)
                r = rope(z[:, src])
                o_ref[0, :, dst] = (r if mult is None else r * mult).astype(o_ref.dtype)
        return epilogue

    def plain_out(o_ref):
        def epilogue(z, sl):
            o_ref[0, :, sl] = z.astype(o_ref.dtype)
        return epilogue

    def silu_out(o_ref):
        def epilogue(z, sl):
            o_ref[0, :, sl] = (z * jax.nn.sigmoid(z)).astype(o_ref.dtype)
        return epilogue

    def gate_out(lb_ref, hi_ref, lo_ref, key_ref):
        def epilogue(f, sl):
            lb = lb_ref[:, sl]
            one_m_lb = 1.0 - lb
            g = lb + one_m_lb * jax.nn.sigmoid(f)
            lg = jnp.log(jnp.maximum(g, GATE_FLOOR)) * LOG2E
            hi = lg.astype(BF16)
            hi_ref[0, :, sl] = hi
            lo_ref[0, :, sl] = (lg - hi.astype(F32)).astype(BF16)
            key_ref[0, :, sl] = (one_m_lb * jax.nn.sigmoid(-f)).astype(BF16)
        return epilogue

    segments = [(NA_WIDTH, rope_out(q_ref, scale), True), (NA_WIDTH, rope_out(k_ref, None), True),
                (NA_WIDTH, plain_out(v_ref), False), (HG_WIDTH, silu_out(hq_ref), True),
                (HG_WIDTH, gate_out(lbf_ref, lfh_ref, lfl_ref, kf_ref), True),
                (HG_WIDTH, gate_out(lbb_ref, lbh_ref, lbl_ref, kb_ref), True),
                (HG_WIDTH, plain_out(hi_ref), False), (HG_WIDTH, plain_out(og_ref), False),
                (d, plain_out(ga_ref), False), (d, plain_out(gh_ref), False)]
    heavy, light, off = [], [], 0
    for width, epilogue, is_heavy in segments:
        for c in range(0, width, MXU_COLS):
            (heavy if is_heavy else light).append((off + c, slice(c, c + MXU_COLS), epilogue))
        off += width
    order = []
    while heavy or light:
        if heavy:
            order.append(heavy.pop(0))
        order.extend(light[:1])
        del light[:1]
    z = proj(order[0][0], MXU_COLS)
    for i, (_, sl, epilogue) in enumerate(order):
        z_next = proj(order[i + 1][0], MXU_COLS) if i + 1 < len(order) else None
        epilogue(z, sl)
        z = z_next


def _stream_specs(first, rest, n_first):
    d = first.shape[-1]
    return (pl.BlockSpec((1, ROW_TILE, d), lambda bi, j: (bi, 0, 0)),
            pl.BlockSpec((1, ROW_TILE, d), lambda bi, j: (bi, jnp.maximum(j - n_first, 0), 0)))


def _inproj(stream, lt, mods, g, w_all, layer, cos_t, sin_t, lb_f, lb_b, ctx_row):
    first, rest, n_first = stream
    b, _, d = first.shape
    n_t = lt // ROW_TILE

    def rows(width):
        return pl.BlockSpec((1, ROW_TILE, width), lambda bi, j: (bi, j, 0))

    def mod_map(bi, j):
        return (jnp.where(j == 0, ctx_row, bi), 0, 0)

    widths = [NA_WIDTH] * 3 + [HG_WIDTH] * 9 + [d, d]
    return pl.pallas_call(
        _inproj_kernel,
        grid=(b, n_t),
        in_specs=[*_stream_specs(first, rest, n_first),
                  pl.BlockSpec((1, 6, d), mod_map),
                  pl.BlockSpec((1, d), lambda bi, j: (0, 0)),
                  pl.BlockSpec((1,) + w_all.shape[1:], lambda bi, j: (layer, 0, 0)),
                  pl.BlockSpec((ROW_TILE, LANES), lambda bi, j: (j, 0)),
                  pl.BlockSpec((ROW_TILE, LANES), lambda bi, j: (j, 0)),
                  pl.BlockSpec((1, HG_WIDTH), lambda bi, j: (0, 0)),
                  pl.BlockSpec((1, HG_WIDTH), lambda bi, j: (0, 0))],
        out_specs=[rows(w) for w in widths],
        out_shape=[jax.ShapeDtypeStruct((b, lt, w), BF16) for w in widths],
        compiler_params=_cparams(("arbitrary", "arbitrary")),
        name="inproj",
    )(first, rest, mods, g, w_all, cos_t, sin_t, lb_f.reshape(1, HG_WIDTH), lb_b.reshape(1, HG_WIDTH))


def _rope_tables(n_ctx, seq):
    t = jnp.arange(seq, dtype=jnp.int32)
    row = (t // GRID_W).astype(F32)
    col = (t % GRID_W).astype(F32)
    rot_half = NA_HEAD_DIM // 2
    inv = ROPE_THETA ** (-jnp.arange(0, rot_half, 2, dtype=F32) / rot_half)
    ang = jnp.concatenate([row[:, None] * inv, col[:, None] * inv], axis=-1)
    cos = jnp.repeat(jnp.cos(ang), 2, axis=-1)
    sin = jnp.repeat(jnp.sin(ang), 2, axis=-1)
    sign = jnp.asarray(np.tile(np.array([-1.0, 1.0], np.float32), NA_HEAD_DIM // 2))
    sin = sin * sign
    cos = jnp.concatenate([jnp.ones((n_ctx, NA_HEAD_DIM), F32), cos], axis=0)
    sin = jnp.concatenate([jnp.zeros((n_ctx, NA_HEAD_DIM), F32), sin], axis=0)
    reps = LANES // NA_HEAD_DIM
    return jnp.tile(cos, (1, reps)), jnp.tile(sin, (1, reps))


def _na_plan(n_ctx, grid_rows):
    span = NA_WIN_ROWS + NA_Q_ROWS - 1
    variants, var_of_step, ws_of_step = [], [], []
    for r0 in range(0, grid_rows, NA_Q_ROWS):
        rs = [min(max(r0 + dq - NA_WIN_ROWS // 2, 0), grid_rows - NA_WIN_ROWS) for dq in range(NA_Q_ROWS)]
        ws = min(rs[0], grid_rows - span)
        key = (r0 - ws,) + tuple(r - ws for r in rs)
        if key not in variants:
            variants.append(key)
        var_of_step.append(variants.index(key))
        ws_of_step.append(ws)
    ctx_steps = n_ctx // (NA_Q_ROWS * GRID_W)
    var_of_step = [len(variants)] * ctx_steps + var_of_step
    ws_of_step = [0] * ctx_steps + ws_of_step
    return variants, np.asarray(var_of_step, np.int32), np.asarray(ws_of_step, np.int32)


def _na_bias_planes(rel_bias, variants):
    span = NA_WIN_ROWS + NA_Q_ROWS - 1
    n_dr, n_dc = 2 * NA_WIN_ROWS - 1, 2 * NA_WIN_COLS - 1
    col = np.arange(GRID_W)[:, None]
    kc = np.arange(GRID_W)[None, :]
    cs = np.clip(col - NA_WIN_COLS // 2, 0, GRID_W - NA_WIN_COLS)
    col_ok = (kc >= cs) & (kc < cs + NA_WIN_COLS)
    oh_c = ((kc - col + NA_WIN_COLS - 1)[..., None] == np.arange(n_dc)) & col_ok[..., None]
    planes = jnp.einsum("hrc,xkc->hrxk", rel_bias.astype(F32), jnp.asarray(oh_c.astype(np.float32)),
                        precision=lax.Precision.HIGHEST)
    planes = jnp.where(jnp.asarray(col_ok), planes * LOG2E, MASKED)
    planes = jnp.concatenate([planes, jnp.full_like(planes[:, :1], MASKED)], axis=1)
    plane_of = np.full((len(variants) + 1, NA_Q_ROWS, span), n_dr, np.int32)
    for vi, key in enumerate(variants):
        for dq in range(NA_Q_ROWS):
            for j in range(key[1 + dq], key[1 + dq] + NA_WIN_ROWS):
                plane_of[vi, dq, j] = j - (key[0] + dq) + NA_WIN_ROWS - 1
    return planes, plane_of.reshape(-1)


def _na_kernel(var_ref, ws_ref, plane_ref, q_ref, k_ref, v_ref, b_ref, o_ref, sw_ref, sc_ref, t_ref,
               *, step_off, n_ctx):
    nq = q_ref.shape[1]
    win = sw_ref.shape[1]
    span = win // GRID_W
    j = pl.program_id(1) + step_off
    var = var_ref[j]

    @pl.when((pl.program_id(0) == 0) & (pl.program_id(1) == 0))
    def _():
        for vi in range(t_ref.shape[0]):
            for h in range(NA_HEADS):
                for dq in range(nq // GRID_W):
                    base = (vi * (nq // GRID_W) + dq) * span
                    row = jnp.concatenate([b_ref[h, plane_ref[base + jj]] for jj in range(span)], axis=1)
                    t_ref[vi, h, dq * GRID_W:(dq + 1) * GRID_W, :] = row

    start = pl.multiple_of(n_ctx + ws_ref[j] * GRID_W, GRID_W)
    low = lax.broadcasted_iota(jnp.int32, (nq, LANES), 1) < NA_HEAD_DIM
    for hp in range(NA_HEADS // 2):
        sl = slice(hp * LANES, (hp + 1) * LANES)
        q2 = q_ref[0, :, sl]
        kw = k_ref[0, pl.ds(start, win), sl]
        kc = k_ref[0, 0:n_ctx, sl]
        zero = jnp.zeros_like(q2)
        qs = jnp.concatenate([jnp.where(low, q2, zero), jnp.where(low, zero, q2)], axis=0)
        rows = slice(2 * hp * nq, (2 * hp + 2) * nq)
        bias = t_ref[var, 2 * hp:2 * hp + 2].reshape(2 * nq, win)
        sw_ref[rows, :] = lax.dot_general(qs, kw, _NT, preferred_element_type=F32) + bias
        sc_ref[rows, :] = lax.dot_general(qs, kc, _NT, preferred_element_type=F32)
    sw = sw_ref[...]
    sc = sc_ref[...]
    m = jnp.maximum(jnp.max(sw, axis=-1, keepdims=True), jnp.max(sc, axis=-1, keepdims=True))
    pw = jnp.exp2(sw - m)
    pc = jnp.exp2(sc - m)
    inv = 1.0 / (jnp.sum(pw, axis=-1, keepdims=True) + jnp.sum(pc, axis=-1, keepdims=True))
    pw = pw.astype(BF16)
    pc = pc.astype(BF16)
    for hp in range(NA_HEADS // 2):
        sl = slice(hp * LANES, (hp + 1) * LANES)
        vw = v_ref[0, pl.ds(start, win), sl]
        vc = v_ref[0, 0:n_ctx, sl]
        rows = slice(2 * hp * nq, (2 * hp + 2) * nq)
        o2 = (jnp.dot(pw[rows], vw, preferred_element_type=F32)
              + jnp.dot(pc[rows], vc, preferred_element_type=F32)) * inv[rows]
        o_ref[0, :, sl] = jnp.where(low, o2[:nq], o2[nq:]).astype(o_ref.dtype)


def _na_attention(q, k, v, rel_bias, n_ctx, with_ctx):
    b, lt, _ = q.shape
    nq = NA_Q_ROWS * GRID_W
    span = NA_WIN_ROWS + NA_Q_ROWS - 1
    variants, var_of_step, ws_of_step = _na_plan(n_ctx, (lt - n_ctx) // GRID_W)
    planes, plane_of = _na_bias_planes(rel_bias, variants)
    step_off = 0 if with_ctx else n_ctx // nq
    n_steps = lt // nq - step_off

    kern = functools.partial(_na_kernel, step_off=step_off, n_ctx=n_ctx)
    grid_spec = pltpu.PrefetchScalarGridSpec(
        num_scalar_prefetch=3,
        grid=(b, n_steps),
        in_specs=[pl.BlockSpec((1, nq, NA_WIDTH), lambda bi, j, *_: (bi, j + step_off, 0)),
                  pl.BlockSpec((1, lt, NA_WIDTH), lambda bi, j, *_: (bi, 0, 0)),
                  pl.BlockSpec((1, lt, NA_WIDTH), lambda bi, j, *_: (bi, 0, 0)),
                  pl.BlockSpec(planes.shape, lambda bi, j, *_: (0, 0, 0, 0))],
        out_specs=pl.BlockSpec((1, nq, NA_WIDTH), lambda bi, j, *_: (bi, j, 0)),
        scratch_shapes=[pltpu.VMEM((NA_HEADS * nq, span * GRID_W), F32),
                        pltpu.VMEM((NA_HEADS * nq, n_ctx), F32),
                        pltpu.VMEM((len(variants) + 1, NA_HEADS, nq, span * GRID_W), F32)],
    )
    return pl.pallas_call(
        kern,
        grid_spec=grid_spec,
        out_shape=jax.ShapeDtypeStruct((b, n_steps * nq, NA_WIDTH), BF16),
        compiler_params=_cparams(("arbitrary", "arbitrary")),
        name="na_attention",
    )(jnp.asarray(var_of_step), jnp.asarray(ws_of_step), jnp.asarray(plane_of), q, k, v, planes)


def _block_ref_rows(b, w, rev, upper_fill=None, lower_fill=None):
    c, n = b.shape
    off = w if rev else w - 1
    if upper_fill is not None or lower_fill is not None:
        parts = []
        for s in range(0, c, 2 * w):
            ref = jnp.broadcast_to(b[s + off:s + off + 1], (w, n))
            parts.append(ref if lower_fill is None else jnp.full((w, n), lower_fill, b.dtype))
            parts.append(ref if upper_fill is None else jnp.full((w, n), upper_fill, b.dtype))
        return jnp.concatenate(parts, axis=0)
    if 2 * w >= SUBLANES:
        parts = [jnp.broadcast_to(b[s + off:s + off + 1], (2 * w, n)) for s in range(0, c, 2 * w)]
        return parts[0] if len(parts) == 1 else jnp.concatenate(parts, axis=0)
    b3 = b.reshape(c // SUBLANES, SUBLANES, n)
    sub = lax.broadcasted_iota(jnp.int32, b3.shape, 1)
    r = None
    for s in range(0, SUBLANES, 2 * w):
        piece = jnp.broadcast_to(b3[:, s + off:s + off + 1, :], b3.shape)
        r = piece if r is None else jnp.where(sub >= s, piece, r)
    return r.reshape(c, n)


def _hgrn_chunks(chains, states, lmask_ref):
    n = len(chains)
    c = chains[0][0].shape[0]
    qb = [ch[0] for ch in chains]
    kb = [ch[2] for ch in chains]
    b = []
    for i in range(n):
        (hi, lo), tri = chains[i][1], chains[i][4]
        b.append(jnp.dot(tri, hi, preferred_element_type=F32) + jnp.dot(tri, lo, preferred_element_type=F32))

    states = list(states)
    o = []
    for i in range(n):
        vb, rev, sid = chains[i][3], chains[i][7], chains[i][8]
        b_last = b[i][0:1] if rev else b[i][c - 1:c]
        qe = qb[i] * jnp.exp2(b[i]).astype(BF16)
        ke = kb[i] * jnp.exp2(b_last - b[i]).astype(BF16)
        st = states[sid]
        o.append(lax.dot_general(qe, st.astype(BF16), _NT, preferred_element_type=F32))
        states[sid] = st * jnp.exp2(b_last) + lax.dot_general(vb, ke, _TN, preferred_element_type=F32)

    a = [None] * n
    w = c // 2
    for li in range(lmask_ref.shape[0]):
        for i in range(n):
            negq_ref, negk_ref, rev = chains[i][5], chains[i][6], chains[i][7]
            if w >= SUBLANES:
                eq = b[i] - _block_ref_rows(b[i], w, rev, -MASKED if rev else None, None if rev else -MASKED)
                ek = _block_ref_rows(b[i], w, rev, None if rev else MASKED, MASKED if rev else None) - b[i]
                kw = kb[i] * jnp.exp2(ek).astype(BF16)
            elif w > 1:
                d = b[i] - _block_ref_rows(b[i], w, rev)
                eq = d + negq_ref[li]
                kw = kb[i] * jnp.exp2(negk_ref[li] - d).astype(BF16)
            else:
                hi, lo = chains[i][1]
                eq = hi.astype(F32) + lo.astype(F32) + negq_ref[li]
                kw = kb[i] * chains[i][9]
            qw = qb[i] * jnp.exp2(eq).astype(BF16)
            p = lax.dot_general(qw, kw, _NT, preferred_element_type=F32).astype(BF16) * lmask_ref[li]
            a[i] = p if a[i] is None else a[i] + p
        w //= 2
    for i in range(n):
        vb = chains[i][3]
        o[i] = o[i] + jnp.dot(a[i], vb, preferred_element_type=F32)
        diag = jnp.sum(qb[i].astype(F32) * kb[i].astype(F32), axis=-1, keepdims=True)
        o[i] = o[i] + diag * vb.astype(F32)
    return o, states


def _hgrn_kernel(hq_ref, hi_ref, lfh_ref, lfl_ref, kf_ref, lbh_ref, lbl_ref, kb_ref, trif_ref, trir_ref,
                 negqf_ref, negkf_ref, negqr_ref, negkr_ref, lmask_ref, konef_ref, koner_ref,
                 o_ref, sf_ref, sb_ref, of_ref, ob_ref, *, n_ctx_chunks, n_chunks):
    sf_ref[...] = jnp.zeros_like(sf_ref)
    sb_ref[...] = jnp.zeros_like(sb_ref)

    def steps(cf0, cb0, unroll):
        chains, rows = [], []
        for u in range(unroll):
            rf = pl.ds(pl.multiple_of((cf0 + u) * HG_CHUNK, HG_CHUNK), HG_CHUNK)
            rb = pl.ds(pl.multiple_of((cb0 - u) * HG_CHUNK, HG_CHUNK), HG_CHUNK)
            chains.append((hq_ref[0, rf, :], (lfh_ref[0, rf, :], lfl_ref[0, rf, :]), kf_ref[0, rf, :],
                           hi_ref[0, rf, :], trif_ref[...], negqf_ref, negkf_ref, False, 0, konef_ref[...]))
            chains.append((hq_ref[0, rb, :], (lbh_ref[0, rb, :], lbl_ref[0, rb, :]), kb_ref[0, rb, :],
                           hi_ref[0, rb, :], trir_ref[...], negqr_ref, negkr_ref, True, 1, koner_ref[...]))
            rows += [rf, rb]
        outs, (sf, sb) = _hgrn_chunks(chains, [sf_ref[...], sb_ref[...]], lmask_ref)
        for i, r in enumerate(rows):
            if i % 2 == 0:
                of_ref[r, :] = outs[i]
            else:
                ob_ref[r, :] = outs[i]
        sf_ref[...] = sf
        sb_ref[...] = sb

    n_lat = n_chunks - n_ctx_chunks
    u_ctx = math.gcd(HG_UNROLL, n_ctx_chunks)
    u_lat = math.gcd(HG_UNROLL, n_lat)

    def ctx_body(i, carry):
        steps(i * u_ctx, n_ctx_chunks - 1 - i * u_ctx, u_ctx)
        return carry

    def lat_body(i, carry):
        steps(n_ctx_chunks + i * u_lat, n_chunks - 1 - i * u_lat, u_lat)
        return carry

    lax.fori_loop(0, n_ctx_chunks // u_ctx, ctx_body, 0)
    lax.fori_loop(0, n_lat // u_lat, lat_body, 0)
    o_ref[0] = (of_ref[...] + ob_ref[...]).astype(o_ref.dtype)


def _hgrn_level_constants():
    c = HG_CHUNK
    t = np.arange(c)
    xor = t[:, None] ^ t[None, :]
    lmask, negq_f, negq_r = [], [], []
    w = c // 2
    while w >= 1:
        lmask.append(((xor >= w) & (xor < 2 * w)).astype(np.float32))
        upper = (t % (2 * w)) >= w
        negq_f.append(np.where(upper, 0.0, MASKED))
        negq_r.append(np.where(upper, MASKED, 0.0))
        w //= 2

    def rows(m):
        return jnp.asarray(np.broadcast_to(np.stack(m)[:, :, None], (len(m), c, HG_DIM)).astype(np.float32))

    tri_f = (t[None, :] <= t[:, None]).astype(np.float32)
    odd = np.broadcast_to((t % 2 == 1)[:, None], (c, HG_DIM)).astype(np.float32)
    return (jnp.asarray(tri_f, BF16), jnp.asarray(tri_f.T, BF16),
            rows(negq_f), rows(negq_r), rows(negq_r), rows(negq_f), jnp.asarray(np.stack(lmask), BF16),
            jnp.asarray(1.0 - odd, BF16), jnp.asarray(odd, BF16))


def _hgrn(hq, hi, fwd, bwd, n_ctx):
    b, lt, _ = hq.shape
    kern = functools.partial(_hgrn_kernel, n_ctx_chunks=n_ctx // HG_CHUNK, n_chunks=lt // HG_CHUNK)
    seq = pl.BlockSpec((1, lt, HG_DIM), lambda bi, h: (bi, 0, h))
    consts = _hgrn_level_constants()
    return pl.pallas_call(
        kern,
        grid=(b, HG_HEADS),
        in_specs=[seq] * 8
        + [pl.BlockSpec(a.shape, lambda bi, h, nd=a.ndim: (0,) * nd) for a in consts],
        out_specs=seq,
        out_shape=jax.ShapeDtypeStruct((b, lt, HG_WIDTH), BF16),
        scratch_shapes=[pltpu.VMEM((HG_DIM, HG_DIM), F32), pltpu.VMEM((HG_DIM, HG_DIM), F32),
                        pltpu.VMEM((lt, HG_DIM), F32), pltpu.VMEM((lt, HG_DIM), F32)],
        compiler_params=_cparams(("arbitrary", "arbitrary")),
        name="hgrn2",
    )(hq, hi, *fwd, *bwd, *consts)


def _split2(x):
    hi = x.astype(BF16)
    lo = (x - hi.astype(F32)).astype(BF16)
    return hi, lo


def _route(logits, bias):
    biased = jax.nn.sigmoid(logits) + bias
    rows = [biased[e:e + 1] for e in range(N_EXPERTS)]
    n = EXPERTS_PER_GROUP
    best = None
    g_sel = None
    for g in range(N_GROUPS):
        gs = None
        for i in range(n):
            for j2 in range(i + 1, n):
                pair = rows[g * n + i] + rows[g * n + j2]
                gs = pair if gs is None else jnp.maximum(gs, pair)
        if best is None:
            best, g_sel = gs, jnp.zeros(gs.shape, jnp.int32)
        else:
            take = gs > best
            best = jnp.where(take, gs, best)
            g_sel = jnp.where(take, g, g_sel)
    cand = []
    for i in range(n):
        c_i = rows[(N_GROUPS - 1) * n + i]
        for g in range(N_GROUPS - 2, -1, -1):
            c_i = jnp.where(g_sel == g, rows[g * n + i], c_i)
        cand.append(c_i)
    m1, i1 = cand[0], jnp.zeros(best.shape, jnp.int32)
    for i in range(1, n):
        take = cand[i] > m1
        m1 = jnp.where(take, cand[i], m1)
        i1 = jnp.where(take, i, i1)
    m2 = jnp.full(best.shape, -jnp.inf, F32)
    i2 = jnp.zeros(best.shape, jnp.int32)
    for i in range(n):
        take = (i1 != i) & (cand[i] > m2)
        m2 = jnp.where(take, cand[i], m2)
        i2 = jnp.where(take, i, i2)
    lo = jnp.minimum(i1, i2)
    hi = jnp.maximum(i1, i2)
    pair = jnp.where(lo == 0, hi - 1, jnp.where(lo == 1, hi + 1, N_PAIRS - 1))
    return g_sel * N_PAIRS + pair


def _merge_kernel(ona_ref, ohg_ref, og_ref, ga_ref, gh_ref, x0_ref, x_ref, mod_ref, hgg_ref, n2g_ref,
                  wna_ref, whg_ref, wout_ref, wr_ref, rb_ref,
                  xn_ref, h2_ref, bucket_ref, rank_ref, cnt_ref, carry_ref, hprev_ref,
                  *, n_t, n_tot, skip):
    step = pl.program_id(0)
    first_tile = (jnp.minimum(step, n_tot - 1) % n_t + skip) == 0

    @pl.when(step == 0)
    def _():
        carry_ref[...] = jnp.zeros_like(carry_ref)
        hprev_ref[...] = jnp.zeros_like(hprev_ref)

    w_hi, w_lo = _split2(wr_ref[...])
    h_hi, h_lo = _split2(hprev_ref[...])
    logits = (lax.dot_general(w_hi, h_hi, _NT, preferred_element_type=F32)
              + lax.dot_general(w_hi, h_lo, _NT, preferred_element_type=F32)
              + lax.dot_general(w_lo, h_hi, _NT, preferred_element_type=F32))

    ohg = ohg_ref[0].astype(F32)
    og = og_ref[0].astype(F32)
    gain = hgg_ref[...]
    heads = []
    for h in range(HG_HEADS):
        sl = slice(h * HG_DIM, (h + 1) * HG_DIM)
        oh = ohg[:, sl]
        yh = oh * lax.rsqrt(jnp.mean(oh * oh, axis=-1, keepdims=True) + EPS) * gain
        gt = og[:, sl]
        heads.append((yh * (gt * jax.nn.sigmoid(gt))).astype(BF16))
    hn = jnp.concatenate(heads, axis=-1)
    y_na = jnp.dot(ona_ref[0], wna_ref[0], preferred_element_type=F32)
    y_hg = jnp.dot(hn, whg_ref[0], preferred_element_type=F32)

    bucket = _route(logits, rb_ref[...])
    bucket_ref[0] = bucket

    m = jax.nn.sigmoid(ga_ref[0].astype(F32)) * y_na + jax.nn.sigmoid(gh_ref[0].astype(F32)) * y_hg
    y = jnp.dot(m.astype(BF16), wout_ref[0], preferred_element_type=F32)

    t = bucket.shape[1]
    onehot = (lax.broadcasted_iota(jnp.int32, (BUCKET_ROWS, t), 0) == bucket).astype(F32)
    before = (lax.broadcasted_iota(jnp.int32, (t, t), 0)
              < lax.broadcasted_iota(jnp.int32, (t, t), 1)).astype(BF16)
    prefix = jnp.dot(onehot.astype(BF16), before, preferred_element_type=F32)
    carry = carry_ref[...]
    rank = jnp.sum(onehot * (prefix + carry[:, 0:1]), axis=0, keepdims=True)
    rank_ref[0] = rank.astype(jnp.int32)
    live = (step > 0).astype(F32)
    carry = carry + live * jnp.sum(onehot, axis=1, keepdims=True)
    carry_ref[...] = carry
    cnt_ref[...] = carry.astype(jnp.int32)

    xn = jnp.where(first_tile, x0_ref[0], x_ref[0]) + mod_ref[0, 2:3, :] * y
    xn_ref[0] = xn
    yn = xn * lax.rsqrt(jnp.mean(xn * xn, axis=-1, keepdims=True) + EPS) * n2g_ref[...]
    h2 = yn * (1.0 + mod_ref[0, 4:5, :]) + mod_ref[0, 3:4, :]
    h2_ref[...] = h2
    hprev_ref[...] = h2


def _merge(o_na, o_hg, og, ga, gh, stream, mods, hg_gain, n2_gain, w_na_o, w_hg_o, w_out, layer,
           w_router_t, router_bias, ctx_row, skip):
    first, rest, n_first = stream
    b, lt, _ = o_hg.shape
    d = first.shape[-1]
    n_t = lt // ROW_TILE - skip
    n_tot = b * n_t

    def tile(s):
        t = jnp.minimum(s, n_tot - 1)
        return t // n_t, t % n_t

    def rows(width):
        return pl.BlockSpec((1, ROW_TILE, width), lambda s: (tile(s)[0], tile(s)[1] + skip, 0))

    def full(a):
        return pl.BlockSpec(a.shape, lambda s: (0,) * a.ndim)

    def of_layer(a):
        return pl.BlockSpec((1,) + a.shape[1:], lambda s: (layer,) + (0,) * (a.ndim - 1))

    def mod_map(s):
        bi, j = tile(s)
        return (jnp.where(j + skip == 0, ctx_row, bi), 0, 0)

    tok = pl.BlockSpec((1, 1, ROW_TILE), lambda s: (jnp.maximum(s - 1, 0), 0, 0))
    consts = [hg_gain, n2_gain, w_na_o, w_hg_o, w_out, w_router_t, router_bias]
    const_specs = [full(hg_gain), full(n2_gain), of_layer(w_na_o), of_layer(w_hg_o), of_layer(w_out),
                   full(w_router_t), full(router_bias)]
    kern = functools.partial(_merge_kernel, n_t=n_t, n_tot=n_tot, skip=skip)
    return pl.pallas_call(
        kern,
        grid=(n_tot + 1,),
        in_specs=[pl.BlockSpec((1, ROW_TILE, NA_WIDTH), lambda s: tile(s) + (0,)),
                  rows(HG_WIDTH), rows(HG_WIDTH), rows(d), rows(d),
                  pl.BlockSpec((1, ROW_TILE, d), lambda s: (tile(s)[0], 0, 0)),
                  pl.BlockSpec((1, ROW_TILE, d),
                               lambda s: (tile(s)[0], jnp.maximum(tile(s)[1] + skip - n_first, 0), 0)),
                  pl.BlockSpec((1, 6, d), mod_map)] + const_specs,
        out_specs=[pl.BlockSpec((1, ROW_TILE, d), lambda s: tile(s) + (0,)),
                   pl.BlockSpec((ROW_TILE, d), lambda s: (jnp.minimum(s, n_tot - 1), 0)),
                   tok, tok,
                   pl.BlockSpec((BUCKET_ROWS, LANES), lambda s: (0, 0))],
        out_shape=[jax.ShapeDtypeStruct((b, n_t * ROW_TILE, d), F32),
                   jax.ShapeDtypeStruct((b * n_t * ROW_TILE, d), F32),
                   jax.ShapeDtypeStruct((b * n_t, 1, ROW_TILE), jnp.int32),
                   jax.ShapeDtypeStruct((b * n_t, 1, ROW_TILE), jnp.int32),
                   jax.ShapeDtypeStruct((BUCKET_ROWS, LANES), jnp.int32)],
        scratch_shapes=[pltpu.VMEM((BUCKET_ROWS, LANES), F32), pltpu.VMEM((ROW_TILE, d), F32)],
        compiler_params=_cparams(("arbitrary",)),
        name="merge_router",
    )(o_na, o_hg, og, ga, gh, first, rest, mods, *consts)


def _start_row_copies(n, row_copy):
    for r in range(n):
        row_copy(r).start()


def _scatter_kernel(dest_ref, h_ref, xs_ref, zero_ref, sem, *, n_token_steps):
    n = h_ref.shape[0]
    step = pl.program_id(0)

    @pl.when(step < n_token_steps)
    def _():
        _start_row_copies(n, lambda r: pltpu.make_async_copy(
            h_ref.at[pl.ds(r, 1)], xs_ref.at[pl.ds(dest_ref[0, 0, r], 1)], sem))

    @pl.when(step >= n_token_steps)
    def _():
        zero_ref[...] = jnp.zeros_like(zero_ref)
        _start_row_copies(n, lambda r: pltpu.make_async_copy(
            zero_ref.at[pl.ds(0, 1)], xs_ref.at[pl.ds(dest_ref[0, 0, r], 1)], sem))

    pltpu.make_async_copy(h_ref, xs_ref.at[pl.ds(0, n)], sem).wait()


def _scatter_rows(h2, dest, pad_pos):
    t, d = h2.shape
    tile = 2 * ROW_TILE if t % (2 * ROW_TILE) == 0 and pad_pos.shape[0] % (2 * ROW_TILE) == 0 else ROW_TILE
    n_t = t // tile
    n_steps = n_t + pad_pos.shape[0] // tile
    kern = functools.partial(_scatter_kernel, n_token_steps=n_t)
    return pl.pallas_call(
        kern,
        grid=(n_steps,),
        in_specs=[pl.BlockSpec((1, 1, tile), lambda i: (i, 0, 0), memory_space=pltpu.SMEM),
                  pl.BlockSpec((tile, d), lambda i: (jnp.minimum(i, n_t - 1), 0))],
        out_specs=pl.BlockSpec(memory_space=pl.ANY),
        out_shape=jax.ShapeDtypeStruct((n_steps * tile, d), F32),
        scratch_shapes=[pltpu.VMEM((SUBLANES, d), F32), pltpu.SemaphoreType.DMA(())],
        compiler_params=_cparams(("arbitrary",)),
        name="scatter_rows",
    )(jnp.concatenate([dest, pad_pos]).reshape(n_steps, 1, tile), h2)


def _final_kernel(dest_ref, dest_next_ref, xn_ref, mod_ref, g_ref, ys_ref, o_ref, buf_ref, sems,
                  *, final_norm):
    n = buf_ref.shape[1]
    step = pl.program_id(0) * pl.num_programs(1) + pl.program_id(1)
    n_steps = pl.num_programs(0) * pl.num_programs(1)

    def gather(d_ref, slot):
        _start_row_copies(n, lambda r: pltpu.make_async_copy(
            ys_ref.at[pl.ds(d_ref[0, 0, r], 1)], buf_ref.at[slot, pl.ds(r, 1)], sems.at[slot]))

    @pl.when(step == 0)
    def _():
        gather(dest_ref, 0)

    @pl.when(step + 1 < n_steps)
    def _():
        gather(dest_next_ref, (step + 1) % 2)

    slot = step % 2
    pltpu.make_async_copy(ys_ref.at[pl.ds(0, n)], buf_ref.at[slot], sems.at[slot]).wait()
    x = xn_ref[0] + mod_ref[0, 5:6, :] * buf_ref[slot]
    if final_norm:
        x = x * lax.rsqrt(jnp.mean(x * x, axis=-1, keepdims=True) + EPS) * g_ref[...]
    o_ref[0] = x


def _gather_residual(xn, mods, ys, dest, gain, ctx_row, final_norm, skip):
    b, lt, d = xn.shape
    n_t = lt // ROW_TILE

    def mod_map(bi, j):
        return (jnp.where(j + skip == 0, ctx_row, bi), 0, 0)

    kern = functools.partial(_final_kernel, final_norm=final_norm)
    last_tile = b * n_t - 1
    dest3 = dest.reshape(b * n_t, 1, ROW_TILE)
    return pl.pallas_call(
        kern,
        grid=(b, n_t),
        in_specs=[pl.BlockSpec((1, 1, ROW_TILE), lambda bi, j: (bi * n_t + j, 0, 0),
                               memory_space=pltpu.SMEM),
                  pl.BlockSpec((1, 1, ROW_TILE), lambda bi, j: (jnp.minimum(bi * n_t + j + 1, last_tile), 0, 0),
                               memory_space=pltpu.SMEM),
                  pl.BlockSpec((1, ROW_TILE, d), lambda bi, j: (bi, j, 0)),
                  pl.BlockSpec((1, 6, d), mod_map),
                  pl.BlockSpec((1, d), lambda bi, j: (0, 0)),
                  pl.BlockSpec(memory_space=pl.ANY)],
        out_specs=pl.BlockSpec((1, ROW_TILE, d), lambda bi, j: (bi, j, 0)),
        out_shape=jax.ShapeDtypeStruct((b, lt, d), F32),
        scratch_shapes=[pltpu.VMEM((2, ROW_TILE, d), F32), pltpu.SemaphoreType.DMA((2,))],
        compiler_params=_cparams(("arbitrary", "arbitrary")),
        name="gather_residual",
    )(dest3, dest3, xn, mods, gain, ys)


def _moe_kernel(ea_ref, eb_ref, nused_ref, xs_ref, wra_ref, wrb_ref,
                wga_ref, wua_ref, wda_ref, wgb_ref, wub_ref, wdb_ref, ys_ref):
    del ea_ref, eb_ref
    i = pl.program_id(0)

    @pl.when(i < nused_ref[0])
    def _():
        x = xs_ref[...]
        s_a = jax.nn.sigmoid(jnp.sum(x * wra_ref[0], axis=-1, keepdims=True))
        s_b = jax.nn.sigmoid(jnp.sum(x * wrb_ref[0], axis=-1, keepdims=True))
        tot = s_a + s_b
        xb = x.astype(BF16)

        gate_a = jnp.dot(xb, wga_ref[0], preferred_element_type=F32)
        up_a = jnp.dot(xb, wua_ref[0], preferred_element_type=F32)
        gate_b = jnp.dot(xb, wgb_ref[0], preferred_element_type=F32)
        hid_a = (gate_a * jax.nn.sigmoid(gate_a) * up_a).astype(BF16)
        up_b = jnp.dot(xb, wub_ref[0], preferred_element_type=F32)
        y_a = jnp.dot(hid_a, wda_ref[0], preferred_element_type=F32)
        hid_b = (gate_b * jax.nn.sigmoid(gate_b) * up_b).astype(BF16)
        y_b = jnp.dot(hid_b, wdb_ref[0], preferred_element_type=F32)
        ys_ref[...] = (s_a / tot) * y_a + (s_b / tot) * y_b

    @pl.when(i >= nused_ref[0])
    def _():
        ys_ref[...] = jnp.zeros_like(ys_ref)


def _moe(xs, tile_ea, tile_eb, n_used, w_router_rows, w_gate, w_up, w_down, layer):
    n_sorted, d = xs.shape
    n_tiles = n_sorted // ROW_TILE
    ff = w_gate.shape[-1]
    base = layer * N_EXPERTS

    def by_a(i, ea, eb, nu):
        return (ea[i], 0, 0)

    def by_b(i, ea, eb, nu):
        return (eb[i], 0, 0)

    def wt_a(i, ea, eb, nu):
        return (base + ea[i], 0, 0)

    def wt_b(i, ea, eb, nu):
        return (base + eb[i], 0, 0)

    grid_spec = pltpu.PrefetchScalarGridSpec(
        num_scalar_prefetch=3,
        grid=(n_tiles,),
        in_specs=[pl.BlockSpec((ROW_TILE, d), lambda i, ea, eb, nu: (i, 0)),
                  pl.BlockSpec((1, 1, d), by_a), pl.BlockSpec((1, 1, d), by_b),
                  pl.BlockSpec((1, d, ff), wt_a), pl.BlockSpec((1, d, ff), wt_a),
                  pl.BlockSpec((1, ff, d), wt_a),
                  pl.BlockSpec((1, d, ff), wt_b), pl.BlockSpec((1, d, ff), wt_b),
                  pl.BlockSpec((1, ff, d), wt_b)],
        out_specs=pl.BlockSpec((ROW_TILE, d), lambda i, ea, eb, nu: (i, 0)),
    )
    return pl.pallas_call(
        _moe_kernel,
        grid_spec=grid_spec,
        out_shape=jax.ShapeDtypeStruct((n_sorted, d), F32),
        compiler_params=_cparams(("arbitrary",)),
        name="moe_pairs",
    )(tile_ea, tile_eb, n_used, xs, w_router_rows, w_router_rows,
      w_gate, w_up, w_down, w_gate, w_up, w_down)


_PAIR_LO = np.array([0, 0, 0, 1, 1, 2], np.int32)
_PAIR_HI = np.array([1, 2, 3, 2, 3, 3], np.int32)


def _sorted_layout(bucket, rank, counts, n_tiles):
    counts = counts[:N_BUCKETS]
    padded = ((counts + ROW_TILE - 1) // ROW_TILE) * ROW_TILE
    ends = jnp.cumsum(padded)
    starts = ends - padded

    def lookup(table, idx):
        hit = jnp.arange(table.shape[0], dtype=jnp.int32)[:, None] == idx[None, :]
        return jnp.sum(jnp.where(hit, table[:, None], 0), axis=0)

    dest = lookup(starts, bucket) + rank
    tile_start = jnp.arange(n_tiles, dtype=jnp.int32) * ROW_TILE
    tile_bucket = jnp.sum((ends[None, :] <= tile_start[:, None]).astype(jnp.int32), axis=1)
    tile_bucket = jnp.minimum(tile_bucket, N_BUCKETS - 1)
    group = tile_bucket // N_PAIRS
    pair = tile_bucket % N_PAIRS
    ea = group * EXPERTS_PER_GROUP + jnp.asarray(_PAIR_LO)[pair]
    eb = group * EXPERTS_PER_GROUP + jnp.asarray(_PAIR_HI)[pair]
    n_used = (ends[-1] // ROW_TILE).astype(jnp.int32).reshape(1)
    seg_start = jnp.concatenate([starts + counts, ends[-1:]])
    seg_len = jnp.concatenate([padded - counts, n_tiles * ROW_TILE - ends[-1:]])
    seg_end = jnp.cumsum(seg_len)
    k = jnp.arange(n_tiles * ROW_TILE - bucket.shape[0], dtype=jnp.int32)
    seg = jnp.sum((seg_end[:, None] <= k[None, :]).astype(jnp.int32), axis=0)
    pad_pos = lookup(seg_start - (seg_end - seg_len), seg) + k
    return (dest.astype(jnp.int32), pad_pos.astype(jnp.int32), ea.astype(jnp.int32), eb.astype(jnp.int32),
            n_used)


def _lower_bounds(raw):
    p = jax.nn.softmax(raw.astype(F32), axis=0)
    return jnp.cumsum(p, axis=0) - p[0:1]


def kernel(x, c, ctx, c_ctx, w_ada, b_ada, norm1_g, w_in, na_rel_bias, hg_lower_fwd, hg_lower_bwd,
           hg_norm_g, w_na_o, w_hg_o, w_out, norm2_g, w_router, router_bias, w_gate, w_up, w_down,
           final_g):
    b, seq, d = x.shape
    n_ctx = ctx.shape[1]
    depth = w_ada.shape[0]
    lt = n_ctx + seq
    assert n_ctx % ROW_TILE == 0 and seq % ROW_TILE == 0 and seq % GRID_W == 0
    assert n_ctx % HG_CHUNK == 0 and seq % HG_CHUNK == 0
    assert seq // GRID_W >= NA_WIN_ROWS + NA_Q_ROWS - 1 and (seq // GRID_W) % NA_Q_ROWS == 0
    assert n_ctx % (NA_Q_ROWS * GRID_W) == 0

    ada_rows = -(-(b + 1) // SUBLANES) * SUBLANES
    ctx_row = b
    cc = jnp.concatenate([c, c_ctx[None, :], jnp.zeros((ada_rows - b - 1, d), F32)], axis=0)
    mods = _ada(cc, w_ada, b_ada).reshape(depth, ada_rows, 6, d)

    lb_f = _lower_bounds(hg_lower_fwd)
    lb_b = _lower_bounds(hg_lower_bwd)
    cos_t, sin_t = _rope_tables(n_ctx, seq)
    w_router_t = jnp.transpose(w_router)
    w_router_rows = w_router_t.reshape(N_EXPERTS, 1, d)
    rb = router_bias.astype(F32).reshape(N_EXPERTS, 1)

    w_in_b, w_na_b, w_hg_b, w_out_b = (w.astype(BF16) for w in (w_in, w_na_o, w_hg_o, w_out))
    ff = w_gate.shape[-1]
    w_gate_b = w_gate.astype(BF16).reshape(depth * N_EXPERTS, d, ff)
    w_up_b = w_up.astype(BF16).reshape(depth * N_EXPERTS, d, ff)
    w_down_b = w_down.astype(BF16).reshape(depth * N_EXPERTS, ff, d)

    assert n_ctx == ROW_TILE
    stream = (ctx, x, 1)
    out = None
    for l in range(depth):
        last = l == depth - 1
        skip = n_ctx // ROW_TILE if last else 0
        n_tiles = b * (lt // ROW_TILE - skip) + N_BUCKETS
        q, k, v, hq, *gates, hi, og, ga, gh = _inproj(
            stream, lt, mods[l], norm1_g[l].reshape(1, d), w_in_b, l, cos_t, sin_t, lb_f[l], lb_b[l],
            ctx_row)
        o_na = _na_attention(q, k, v, na_rel_bias[l], n_ctx, with_ctx=not last)
        o_hg = _hgrn(hq, hi, gates[:3], gates[3:], n_ctx)
        xn, h2, bucket, rank, counts = _merge(
            o_na, o_hg, og, ga, gh, stream, mods[l], hg_norm_g[l].reshape(1, HG_DIM),
            norm2_g[l].reshape(1, d), w_na_b, w_hg_b, w_out_b, l, w_router_t, rb, ctx_row, skip)
        dest, pad_pos, tile_ea, tile_eb, n_used = _sorted_layout(
            bucket.reshape(-1), rank.reshape(-1), counts[:, 0], n_tiles)
        xs = _scatter_rows(h2, dest, pad_pos)
        ys = _moe(xs, tile_ea, tile_eb, n_used, w_router_rows, w_gate_b, w_up_b, w_down_b, l)
        res = _gather_residual(xn, mods[l], ys, dest, final_g.reshape(1, d), ctx_row, last, skip)
        if last:
            out = res
        else:
            stream = (res, res, 0)
    return out
```

```python
import functools
import math

import jax
import jax.numpy as jnp
import numpy as np
from jax import lax
from jax.experimental import pallas as pl
from jax.experimental.pallas import tpu as pltpu

F32 = jnp.float32
BF16 = jnp.bfloat16

GRID_W = 64
EPS = 1e-6
NA_HEADS = 8
NA_HEAD_DIM = 64
NA_WIDTH = NA_HEADS * NA_HEAD_DIM
NA_WIN_ROWS = 8
NA_WIN_COLS = 16
ROPE_THETA = 10000.0
HG_HEADS = 4
HG_DIM = 128
HG_WIDTH = HG_HEADS * HG_DIM
GATE_FLOOR = 1e-30
N_EXPERTS = 16
N_GROUPS = 4
EXPERTS_PER_GROUP = 4
N_PAIRS = 6
N_BUCKETS = N_GROUPS * N_PAIRS
BUCKET_ROWS = 32
MASKED = -1e30
LOG2E = 1.4426950408889634

LANES = 128
MXU_COLS = 256
ROW_TILE = 256
ADA_COLS = 1536
NA_Q_ROWS = 2
SUBLANES = 8
HG_CHUNK = 128
HG_UNROLL = 4
VMEM_LIMIT = 56 * 1024 * 1024

_NT = (((1,), (1,)), ((), ()))
_TN = (((0,), (0,)), ((), ()))


def _cparams(sem):
    return pltpu.CompilerParams(dimension_semantics=sem, vmem_limit_bytes=VMEM_LIMIT)


def _ada_kernel(c_ref, w_ref, b_ref, o_ref):
    cc = c_ref[...]
    s = cc * jax.nn.sigmoid(cc)
    o_ref[0] = jnp.dot(s, w_ref[0], preferred_element_type=F32) + b_ref[0]


def _ada(cc, w_ada, b_ada):
    depth, d, n = w_ada.shape
    rows = cc.shape[0]
    tn = ADA_COLS
    return pl.pallas_call(
        _ada_kernel,
        grid=(depth, n // tn),
        in_specs=[pl.BlockSpec((rows, d), lambda l, j: (0, 0)),
                  pl.BlockSpec((1, d, tn), lambda l, j: (l, 0, j)),
                  pl.BlockSpec((1, 1, tn), lambda l, j: (l, 0, j))],
        out_specs=pl.BlockSpec((1, rows, tn), lambda l, j: (l, 0, j)),
        out_shape=jax.ShapeDtypeStruct((depth, rows, n), F32),
        compiler_params=_cparams(("arbitrary", "arbitrary")),
        name="ada",
    )(cc, w_ada, b_ada.reshape(depth, 1, n))


def _inproj_kernel(x0_ref, x_ref, mod_ref, g_ref, w_ref, cos_ref, sin_ref, lbf_ref, lbb_ref,
                   q_ref, k_ref, v_ref, hq_ref, lfh_ref, lfl_ref, kf_ref, lbh_ref, lbl_ref, kb_ref,
                   hi_ref, og_ref, ga_ref, gh_ref):
    x = jnp.where(pl.program_id(1) == 0, x0_ref[0], x_ref[0])
    y = x * lax.rsqrt(jnp.mean(x * x, axis=-1, keepdims=True) + EPS) * g_ref[...]
    h = (y * (1.0 + mod_ref[0, 1:2, :]) + mod_ref[0, 0:1, :]).astype(BF16)

    def proj(off, width):
        return jnp.dot(h, w_ref[0, :, off:off + width], preferred_element_type=F32)

    cos = cos_ref[...]
    sin = sin_ref[...]
    even = (lax.broadcasted_iota(jnp.int32, cos.shape, 1) % 2) == 0

    def rope(z):
        swapped = jnp.where(even, pltpu.roll(z, LANES - 1, 1), pltpu.roll(z, 1, 1))
        return z * cos + swapped * sin

    scale = NA_HEAD_DIM ** -0.5 * LOG2E
    d = x.shape[-1]

    def rope_out(o_ref, mult):
        def epilogue(z, sl):
            for sub in range(MXU_COLS // LANES):
                src = slice(sub * LANES, (sub + 1) * LANES)
                dst = slice(sl.start + sub * LANES, sl.start + (sub + 1) * LANES)
                r = rope(z[:, src])
                o_ref[0, :, dst] = (r if mult is None else r * mult).astype(o_ref.dtype)
        return epilogue

    def plain_out(o_ref):
        def epilogue(z, sl):
            o_ref[0, :, sl] = z.astype(o_ref.dtype)
        return epilogue

    def silu_out(o_ref):
        def epilogue(z, sl):
            o_ref[0, :, sl] = (z * jax.nn.sigmoid(z)).astype(o_ref.dtype)
        return epilogue

    def gate_out(lb_ref, hi_ref, lo_ref, key_ref):
        def epilogue(f, sl):
            lb = lb_ref[:, sl]
            one_m_lb = 1.0 - lb
            g = lb + one_m_lb * jax.nn.sigmoid(f)
            lg = jnp.log(jnp.maximum(g, GATE_FLOOR)) * LOG2E
            hi = lg.astype(BF16)
            hi_ref[0, :, sl] = hi
            lo_ref[0, :, sl] = (lg - hi.astype(F32)).astype(BF16)
            key_ref[0, :, sl] = (one_m_lb * jax.nn.sigmoid(-f)).astype(BF16)
        return epilogue

    segments = [(NA_WIDTH, rope_out(q_ref, scale), True), (NA_WIDTH, rope_out(k_ref, None), True),
                (NA_WIDTH, plain_out(v_ref), False), (HG_WIDTH, silu_out(hq_ref), True),
                (HG_WIDTH, gate_out(lbf_ref, lfh_ref, lfl_ref, kf_ref), True),
                (HG_WIDTH, gate_out(lbb_ref, lbh_ref, lbl_ref, kb_ref), True),
                (HG_WIDTH, plain_out(hi_ref), False), (HG_WIDTH, plain_out(og_ref), False),
                (d, plain_out(ga_ref), False), (d, plain_out(gh_ref), False)]
    heavy, light, off = [], [], 0
    for width, epilogue, is_heavy in segments:
        for c in range(0, width, MXU_COLS):
            (heavy if is_heavy else light).append((off + c, slice(c, c + MXU_COLS), epilogue))
        off += width
    order = []
    while heavy or light:
        if heavy:
            order.append(heavy.pop(0))
        order.extend(light[:1])
        del light[:1]
    z = proj(order[0][0], MXU_COLS)
    for i, (_, sl, epilogue) in enumerate(order):
        z_next = proj(order[i + 1][0], MXU_COLS) if i + 1 < len(order) else None
        epilogue(z, sl)
        z = z_next


def _stream_specs(first, rest, n_first):
    d = first.shape[-1]
    return (pl.BlockSpec((1, ROW_TILE, d), lambda bi, j: (bi, 0, 0)),
            pl.BlockSpec((1, ROW_TILE, d), lambda bi, j: (bi, jnp.maximum(j - n_first, 0), 0)))


def _inproj(stream, lt, mods, g, w_all, layer, cos_t, sin_t, lb_f, lb_b, ctx_row):
    first, rest, n_first = stream
    b, _, d = first.shape
    n_t = lt // ROW_TILE

    def rows(width):
        return pl.BlockSpec((1, ROW_TILE, width), lambda bi, j: (bi, j, 0))

    def mod_map(bi, j):
        return (jnp.where(j == 0, ctx_row, bi), 0, 0)

    widths = [NA_WIDTH] * 3 + [HG_WIDTH] * 9 + [d, d]
    return pl.pallas_call(
        _inproj_kernel,
        grid=(b, n_t),
        in_specs=[*_stream_specs(first, rest, n_first),
                  pl.BlockSpec((1, 6, d), mod_map),
                  pl.BlockSpec((1, d), lambda bi, j: (0, 0)),
                  pl.BlockSpec((1,) + w_all.shape[1:], lambda bi, j: (layer, 0, 0)),
                  pl.BlockSpec((ROW_TILE, LANES), lambda bi, j: (j, 0)),
                  pl.BlockSpec((ROW_TILE, LANES), lambda bi, j: (j, 0)),
                  pl.BlockSpec((1, HG_WIDTH), lambda bi, j: (0, 0)),
                  pl.BlockSpec((1, HG_WIDTH), lambda bi, j: (0, 0))],
        out_specs=[rows(w) for w in widths],
        out_shape=[jax.ShapeDtypeStruct((b, lt, w), BF16) for w in widths],
        compiler_params=_cparams(("arbitrary", "arbitrary")),
        name="inproj",
    )(first, rest, mods, g, w_all, cos_t, sin_t, lb_f.reshape(1, HG_WIDTH), lb_b.reshape(1, HG_WIDTH))


def _rope_tables(n_ctx, seq):
    t = jnp.arange(seq, dtype=jnp.int32)
    row = (t // GRID_W).astype(F32)
    col = (t % GRID_W).astype(F32)
    rot_half = NA_HEAD_DIM // 2
    inv = ROPE_THETA ** (-jnp.arange(0, rot_half, 2, dtype=F32) / rot_half)
    ang = jnp.concatenate([row[:, None] * inv, col[:, None] * inv], axis=-1)
    cos = jnp.repeat(jnp.cos(ang), 2, axis=-1)
    sin = jnp.repeat(jnp.sin(ang), 2, axis=-1)
    sign = jnp.asarray(np.tile(np.array([-1.0, 1.0], np.float32), NA_HEAD_DIM // 2))
    sin = sin * sign
    cos = jnp.concatenate([jnp.ones((n_ctx, NA_HEAD_DIM), F32), cos], axis=0)
    sin = jnp.concatenate([jnp.zeros((n_ctx, NA_HEAD_DIM), F32), sin], axis=0)
    reps = LANES // NA_HEAD_DIM
    return jnp.tile(cos, (1, reps)), jnp.tile(sin, (1, reps))


def _na_plan(n_ctx, grid_rows):
    span = NA_WIN_ROWS + NA_Q_ROWS - 1
    variants, var_of_step, ws_of_step = [], [], []
    for r0 in range(0, grid_rows, NA_Q_ROWS):
        rs = [min(max(r0 + dq - NA_WIN_ROWS // 2, 0), grid_rows - NA_WIN_ROWS) for dq in range(NA_Q_ROWS)]
        ws = min(rs[0], grid_rows - span)
        key = (r0 - ws,) + tuple(r - ws for r in rs)
        if key not in variants:
            variants.append(key)
        var_of_step.append(variants.index(key))
        ws_of_step.append(ws)
    ctx_steps = n_ctx // (NA_Q_ROWS * GRID_W)
    var_of_step = [len(variants)] * ctx_steps + var_of_step
    ws_of_step = [0] * ctx_steps + ws_of_step
    return variants, np.asarray(var_of_step, np.int32), np.asarray(ws_of_step, np.int32)


def _na_bias_planes(rel_bias, variants):
    span = NA_WIN_ROWS + NA_Q_ROWS - 1
    n_dr, n_dc = 2 * NA_WIN_ROWS - 1, 2 * NA_WIN_COLS - 1
    col = np.arange(GRID_W)[:, None]
    kc = np.arange(GRID_W)[None, :]
    cs = np.clip(col - NA_WIN_COLS // 2, 0, GRID_W - NA_WIN_COLS)
    col_ok = (kc >= cs) & (kc < cs + NA_WIN_COLS)
    oh_c = ((kc - col + NA_WIN_COLS - 1)[..., None] == np.arange(n_dc)) & col_ok[..., None]
    planes = jnp.einsum("hrc,xkc->hrxk", rel_bias.astype(F32), jnp.asarray(oh_c.astype(np.float32)),
                        precision=lax.Precision.HIGHEST)
    planes = jnp.where(jnp.asarray(col_ok), planes * LOG2E, MASKED)
    planes = jnp.concatenate([planes, jnp.full_like(planes[:, :1], MASKED)], axis=1)
    plane_of = np.full((len(variants) + 1, NA_Q_ROWS, span), n_dr, np.int32)
    for vi, key in enumerate(variants):
        for dq in range(NA_Q_ROWS):
            for j in range(key[1 + dq], key[1 + dq] + NA_WIN_ROWS):
                plane_of[vi, dq, j] = j - (key[0] + dq) + NA_WIN_ROWS - 1
    return planes, plane_of.reshape(-1)


def _na_kernel(var_ref, ws_ref, plane_ref, q_ref, k_ref, v_ref, b_ref, o_ref, sw_ref, sc_ref, t_ref,
               *, step_off, n_ctx):
    nq = q_ref.shape[1]
    win = sw_ref.shape[1]
    span = win // GRID_W
    j = pl.program_id(1) + step_off
    var = var_ref[j]

    @pl.when((pl.program_id(0) == 0) & (pl.program_id(1) == 0))
    def _():
        for vi in range(t_ref.shape[0]):
            for h in range(NA_HEADS):
                for dq in range(nq // GRID_W):
                    base = (vi * (nq // GRID_W) + dq) * span
                    row = jnp.concatenate([b_ref[h, plane_ref[base + jj]] for jj in range(span)], axis=1)
                    t_ref[vi, h, dq * GRID_W:(dq + 1) * GRID_W, :] = row

    start = pl.multiple_of(n_ctx + ws_ref[j] * GRID_W, GRID_W)
    low = lax.broadcasted_iota(jnp.int32, (nq, LANES), 1) < NA_HEAD_DIM
    for hp in range(NA_HEADS // 2):
        sl = slice(hp * LANES, (hp + 1) * LANES)
        q2 = q_ref[0, :, sl]
        kw = k_ref[0, pl.ds(start, win), sl]
        kc = k_ref[0, 0:n_ctx, sl]
        zero = jnp.zeros_like(q2)
        qs = jnp.concatenate([jnp.where(low, q2, zero), jnp.where(low, zero, q2)], axis=0)
        rows = slice(2 * hp * nq, (2 * hp + 2) * nq)
        bias = t_ref[var, 2 * hp:2 * hp + 2].reshape(2 * nq, win)
        sw_ref[rows, :] = lax.dot_general(qs, kw, _NT, preferred_element_type=F32) + bias
        sc_ref[rows, :] = lax.dot_general(qs, kc, _NT, preferred_element_type=F32)
    sw = sw_ref[...]
    sc = sc_ref[...]
    m = jnp.maximum(jnp.max(sw, axis=-1, keepdims=True), jnp.max(sc, axis=-1, keepdims=True))
    pw = jnp.exp2(sw - m)
    pc = jnp.exp2(sc - m)
    inv = 1.0 / (jnp.sum(pw, axis=-1, keepdims=True) + jnp.sum(pc, axis=-1, keepdims=True))
    pw = pw.astype(BF16)
    pc = pc.astype(BF16)
    for hp in range(NA_HEADS // 2):
        sl = slice(hp * LANES, (hp + 1) * LANES)
        vw = v_ref[0, pl.ds(start, win), sl]
        vc = v_ref[0, 0:n_ctx, sl]
        rows = slice(2 * hp * nq, (2 * hp + 2) * nq)
        o2 = (jnp.dot(pw[rows], vw, preferred_element_type=F32)
              + jnp.dot(pc[rows], vc, preferred_element_type=F32)) * inv[rows]
        o_ref[0, :, sl] = jnp.where(low, o2[:nq], o2[nq:]).astype(o_ref.dtype)


def _na_attention(q, k, v, rel_bias, n_ctx, with_ctx):
    b, lt, _ = q.shape
    nq = NA_Q_ROWS * GRID_W
    span = NA_WIN_ROWS + NA_Q_ROWS - 1
    variants, var_of_step, ws_of_step = _na_plan(n_ctx, (lt - n_ctx) // GRID_W)
    planes, plane_of = _na_bias_planes(rel_bias, variants)
    step_off = 0 if with_ctx else n_ctx // nq
    n_steps = lt // nq - step_off

    kern = functools.partial(_na_kernel, step_off=step_off, n_ctx=n_ctx)
    grid_spec = pltpu.PrefetchScalarGridSpec(
        num_scalar_prefetch=3,
        grid=(b, n_steps),
        in_specs=[pl.BlockSpec((1, nq, NA_WIDTH), lambda bi, j, *_: (bi, j + step_off, 0)),
                  pl.BlockSpec((1, lt, NA_WIDTH), lambda bi, j, *_: (bi, 0, 0)),
                  pl.BlockSpec((1, lt, NA_WIDTH), lambda bi, j, *_: (bi, 0, 0)),
                  pl.BlockSpec(planes.shape, lambda bi, j, *_: (0, 0, 0, 0))],
        out_specs=pl.BlockSpec((1, nq, NA_WIDTH), lambda bi, j, *_: (bi, j, 0)),
        scratch_shapes=[pltpu.VMEM((NA_HEADS * nq, span * GRID_W), F32),
                        pltpu.VMEM((NA_HEADS * nq, n_ctx), F32),
                        pltpu.VMEM((len(variants) + 1, NA_HEADS, nq, span * GRID_W), F32)],
    )
    return pl.pallas_call(
        kern,
        grid_spec=grid_spec,
        out_shape=jax.ShapeDtypeStruct((b, n_steps * nq, NA_WIDTH), BF16),
        compiler_params=_cparams(("arbitrary", "arbitrary")),
        name="na_attention",
    )(jnp.asarray(var_of_step), jnp.asarray(ws_of_step), jnp.asarray(plane_of), q, k, v, planes)


def _block_ref_rows(b, w, rev, upper_fill=None, lower_fill=None):
    c, n = b.shape
    off = w if rev else w - 1
    if upper_fill is not None or lower_fill is not None:
        parts = []
        for s in range(0, c, 2 * w):
            ref = jnp.broadcast_to(b[s + off:s + off + 1], (w, n))
            parts.append(ref if lower_fill is None else jnp.full((w, n), lower_fill, b.dtype))
            parts.append(ref if upper_fill is None else jnp.full((w, n), upper_fill, b.dtype))
        return jnp.concatenate(parts, axis=0)
    if 2 * w >= SUBLANES:
        parts = [jnp.broadcast_to(b[s + off:s + off + 1], (2 * w, n)) for s in range(0, c, 2 * w)]
        return parts[0] if len(parts) == 1 else jnp.concatenate(parts, axis=0)
    b3 = b.reshape(c // SUBLANES, SUBLANES, n)
    sub = lax.broadcasted_iota(jnp.int32, b3.shape, 1)
    r = None
    for s in range(0, SUBLANES, 2 * w):
        piece = jnp.broadcast_to(b3[:, s + off:s + off + 1, :], b3.shape)
        r = piece if r is None else jnp.where(sub >= s, piece, r)
    return r.reshape(c, n)


def _hgrn_chunks(chains, states, lmask_ref):
    n = len(chains)
    c = chains[0][0].shape[0]
    qb = [ch[0] for ch in chains]
    kb = [ch[2] for ch in chains]
    b = []
    for i in range(n):
        (hi, lo), tri = chains[i][1], chains[i][4]
        b.append(jnp.dot(tri, hi, preferred_element_type=F32) + jnp.dot(tri, lo, preferred_element_type=F32))

    states = list(states)
    o = []
    for i in range(n):
        vb, rev, sid = chains[i][3], chains[i][7], chains[i][8]
        b_last = b[i][0:1] if rev else b[i][c - 1:c]
        qe = qb[i] * jnp.exp2(b[i]).astype(BF16)
        ke = kb[i] * jnp.exp2(b_last - b[i]).astype(BF16)
        st = states[sid]
        o.append(lax.dot_general(qe, st.astype(BF16), _NT, preferred_element_type=F32))
        states[sid] = st * jnp.exp2(b_last) + lax.dot_general(vb, ke, _TN, preferred_element_type=F32)

    a = [None] * n
    w = c // 2
    for li in range(lmask_ref.shape[0]):
        for i in range(n):
            negq_ref, negk_ref, rev = chains[i][5], chains[i][6], chains[i][7]
            if w >= SUBLANES:
                eq = b[i] - _block_ref_rows(b[i], w, rev, -MASKED if rev else None, None if rev else -MASKED)
                ek = _block_ref_rows(b[i], w, rev, None if rev else MASKED, MASKED if rev else None) - b[i]
                kw = kb[i] * jnp.exp2(ek).astype(BF16)
            elif w > 1:
                d = b[i] - _block_ref_rows(b[i], w, rev)
                eq = d + negq_ref[li]
                kw = kb[i] * jnp.exp2(negk_ref[li] - d).astype(BF16)
            else:
                hi, lo = chains[i][1]
                eq = hi.astype(F32) + lo.astype(F32) + negq_ref[li]
                kw = kb[i] * chains[i][9]
            qw = qb[i] * jnp.exp2(eq).astype(BF16)
            p = lax.dot_general(qw, kw, _NT, preferred_element_type=F32).astype(BF16) * lmask_ref[li]
            a[i] = p if a[i] is None else a[i] + p
        w //= 2
    for i in range(n):
        vb = chains[i][3]
        o[i] = o[i] + jnp.dot(a[i], vb, preferred_element_type=F32)
        diag = jnp.sum(qb[i].astype(F32) * kb[i].astype(F32), axis=-1, keepdims=True)
        o[i] = o[i] + diag * vb.astype(F32)
    return o, states


def _hgrn_kernel(hq_ref, hi_ref, lfh_ref, lfl_ref, kf_ref, lbh_ref, lbl_ref, kb_ref, trif_ref, trir_ref,
                 negqf_ref, negkf_ref, negqr_ref, negkr_ref, lmask_ref, konef_ref, koner_ref,
                 o_ref, sf_ref, sb_ref, of_ref, ob_ref, *, n_ctx_chunks, n_chunks):
    sf_ref[...] = jnp.zeros_like(sf_ref)
    sb_ref[...] = jnp.zeros_like(sb_ref)

    def steps(cf0, cb0, unroll):
        chains, rows = [], []
        for u in range(unroll):
            rf = pl.ds(pl.multiple_of((cf0 + u) * HG_CHUNK, HG_CHUNK), HG_CHUNK)
            rb = pl.ds(pl.multiple_of((cb0 - u) * HG_CHUNK, HG_CHUNK), HG_CHUNK)
            chains.append((hq_ref[0, rf, :], (lfh_ref[0, rf, :], lfl_ref[0, rf, :]), kf_ref[0, rf, :],
                           hi_ref[0, rf, :], trif_ref[...], negqf_ref, negkf_ref, False, 0, konef_ref[...]))
            chains.append((hq_ref[0, rb, :], (lbh_ref[0, rb, :], lbl_ref[0, rb, :]), kb_ref[0, rb, :],
                           hi_ref[0, rb, :], trir_ref[...], negqr_ref, negkr_ref, True, 1, koner_ref[...]))
            rows += [rf, rb]
        outs, (sf, sb) = _hgrn_chunks(chains, [sf_ref[...], sb_ref[...]], lmask_ref)
        for i, r in enumerate(rows):
            if i % 2 == 0:
                of_ref[r, :] = outs[i]
            else:
                ob_ref[r, :] = outs[i]
        sf_ref[...] = sf
        sb_ref[...] = sb

    n_lat = n_chunks - n_ctx_chunks
    u_ctx = math.gcd(HG_UNROLL, n_ctx_chunks)
    u_lat = math.gcd(HG_UNROLL, n_lat)

    def ctx_body(i, carry):
        steps(i * u_ctx, n_ctx_chunks - 1 - i * u_ctx, u_ctx)
        return carry

    def lat_body(i, carry):
        steps(n_ctx_chunks + i * u_lat, n_chunks - 1 - i * u_lat, u_lat)
        return carry

    lax.fori_loop(0, n_ctx_chunks // u_ctx, ctx_body, 0)
    lax.fori_loop(0, n_lat // u_lat, lat_body, 0)
    o_ref[0] = (of_ref[...] + ob_ref[...]).astype(o_ref.dtype)


def _hgrn_level_constants():
    c = HG_CHUNK
    t = np.arange(c)
    xor = t[:, None] ^ t[None, :]
    lmask, negq_f, negq_r = [], [], []
    w = c // 2
    while w >= 1:
        lmask.append(((xor >= w) & (xor < 2 * w)).astype(np.float32))
        upper = (t % (2 * w)) >= w
        negq_f.append(np.where(upper, 0.0, MASKED))
        negq_r.append(np.where(upper, MASKED, 0.0))
        w //= 2

    def rows(m):
        return jnp.asarray(np.broadcast_to(np.stack(m)[:, :, None], (len(m), c, HG_DIM)).astype(np.float32))

    tri_f = (t[None, :] <= t[:, None]).astype(np.float32)
    odd = np.broadcast_to((t % 2 == 1)[:, None], (c, HG_DIM)).astype(np.float32)
    return (jnp.asarray(tri_f, BF16), jnp.asarray(tri_f.T, BF16),
            rows(negq_f), rows(negq_r), rows(negq_r), rows(negq_f), jnp.asarray(np.stack(lmask), BF16),
            jnp.asarray(1.0 - odd, BF16), jnp.asarray(odd, BF16))


def _hgrn(hq, hi, fwd, bwd, n_ctx):
    b, lt, _ = hq.shape
    kern = functools.partial(_hgrn_kernel, n_ctx_chunks=n_ctx // HG_CHUNK, n_chunks=lt // HG_CHUNK)
    seq = pl.BlockSpec((1, lt, HG_DIM), lambda bi, h: (bi, 0, h))
    consts = _hgrn_level_constants()
    return pl.pallas_call(
        kern,
        grid=(b, HG_HEADS),
        in_specs=[seq] * 8
        + [pl.BlockSpec(a.shape, lambda bi, h, nd=a.ndim: (0,) * nd) for a in consts],
        out_specs=seq,
        out_shape=jax.ShapeDtypeStruct((b, lt, HG_WIDTH), BF16),
        scratch_shapes=[pltpu.VMEM((HG_DIM, HG_DIM), F32), pltpu.VMEM((HG_DIM, HG_DIM), F32),
                        pltpu.VMEM((lt, HG_DIM), F32), pltpu.VMEM((lt, HG_DIM), F32)],
        compiler_params=_cparams(("arbitrary", "arbitrary")),
        name="hgrn2",
    )(hq, hi, *fwd, *bwd, *consts)


def _split2(x):
    hi = x.astype(BF16)
    lo = (x - hi.astype(F32)).astype(BF16)
    return hi, lo


def _route(logits, bias):
    biased = jax.nn.sigmoid(logits) + bias
    rows = [biased[e:e + 1] for e in range(N_EXPERTS)]
    n = EXPERTS_PER_GROUP
    best = None
    g_sel = None
    for g in range(N_GROUPS):
        gs = None
        for i in range(n):
            for j2 in range(i + 1, n):
                pair = rows[g * n + i] + rows[g * n + j2]
                gs = pair if gs is None else jnp.maximum(gs, pair)
        if best is None:
            best, g_sel = gs, jnp.zeros(gs.shape, jnp.int32)
        else:
            take = gs > best
            best = jnp.where(take, gs, best)
            g_sel = jnp.where(take, g, g_sel)
    cand = []
    for i in range(n):
        c_i = rows[(N_GROUPS - 1) * n + i]
        for g in range(N_GROUPS - 2, -1, -1):
            c_i = jnp.where(g_sel == g, rows[g * n + i], c_i)
        cand.append(c_i)
    m1, i1 = cand[0], jnp.zeros(best.shape, jnp.int32)
    for i in range(1, n):
        take = cand[i] > m1
        m1 = jnp.where(take, cand[i], m1)
        i1 = jnp.where(take, i, i1)
    m2 = jnp.full(best.shape, -jnp.inf, F32)
    i2 = jnp.zeros(best.shape, jnp.int32)
    for i in range(n):
        take = (i1 != i) & (cand[i] > m2)
        m2 = jnp.where(take, cand[i], m2)
        i2 = jnp.where(take, i, i2)
    lo = jnp.minimum(i1, i2)
    hi = jnp.maximum(i1, i2)
    pair = jnp.where(lo == 0, hi - 1, jnp.where(lo == 1, hi + 1, N_PAIRS - 1))
    return g_sel * N_PAIRS + pair


def _merge_kernel(ona_ref, ohg_ref, og_ref, ga_ref, gh_ref, x0_ref, x_ref, mod_ref, hgg_ref, n2g_ref,
                  wna_ref, whg_ref, wout_ref, wr_ref, rb_ref,
                  xn_ref, h2_ref, bucket_ref, rank_ref, cnt_ref, carry_ref, hprev_ref,
                  *, n_t, n_tot, skip):
    step = pl.program_id(0)
    first_tile = (jnp.minimum(step, n_tot - 1) % n_t + skip) == 0

    @pl.when(step == 0)
    def _():
        carry_ref[...] = jnp.zeros_like(carry_ref)
        hprev_ref[...] = jnp.zeros_like(hprev_ref)

    w_hi, w_lo = _split2(wr_ref[...])
    h_hi, h_lo = _split2(hprev_ref[...])
    logits = (lax.dot_general(w_hi, h_hi, _NT, preferred_element_type=F32)
              + lax.dot_general(w_hi, h_lo, _NT, preferred_element_type=F32)
              + lax.dot_general(w_lo, h_hi, _NT, preferred_element_type=F32))

    ohg = ohg_ref[0].astype(F32)
    og = og_ref[0].astype(F32)
    gain = hgg_ref[...]
    heads = []
    for h in range(HG_HEADS):
        sl = slice(h * HG_DIM, (h + 1) * HG_DIM)
        oh = ohg[:, sl]
        yh = oh * lax.rsqrt(jnp.mean(oh * oh, axis=-1, keepdims=True) + EPS) * gain
        gt = og[:, sl]
        heads.append((yh * (gt * jax.nn.sigmoid(gt))).astype(BF16))
    hn = jnp.concatenate(heads, axis=-1)
    y_na = jnp.dot(ona_ref[0], wna_ref[0], preferred_element_type=F32)
    y_hg = jnp.dot(hn, whg_ref[0], preferred_element_type=F32)

    bucket = _route(logits, rb_ref[...])
    bucket_ref[0] = bucket

    m = jax.nn.sigmoid(ga_ref[0].astype(F32)) * y_na + jax.nn.sigmoid(gh_ref[0].astype(F32)) * y_hg
    y = jnp.dot(m.astype(BF16), wout_ref[0], preferred_element_type=F32)

    t = bucket.shape[1]
    onehot = (lax.broadcasted_iota(jnp.int32, (BUCKET_ROWS, t), 0) == bucket).astype(F32)
    before = (lax.broadcasted_iota(jnp.int32, (t, t), 0)
              < lax.broadcasted_iota(jnp.int32, (t, t), 1)).astype(BF16)
    prefix = jnp.dot(onehot.astype(BF16), before, preferred_element_type=F32)
    carry = carry_ref[...]
    rank = jnp.sum(onehot * (prefix + carry[:, 0:1]), axis=0, keepdims=True)
    rank_ref[0] = rank.astype(jnp.int32)
    live = (step > 0).astype(F32)
    carry = carry + live * jnp.sum(onehot, axis=1, keepdims=True)
    carry_ref[...] = carry
    cnt_ref[...] = carry.astype(jnp.int32)

    xn = jnp.where(first_tile, x0_ref[0], x_ref[0]) + mod_ref[0, 2:3, :] * y
    xn_ref[0] = xn
    yn = xn * lax.rsqrt(jnp.mean(xn * xn, axis=-1, keepdims=True) + EPS) * n2g_ref[...]
    h2 = yn * (1.0 + mod_ref[0, 4:5, :]) + mod_ref[0, 3:4, :]
    h2_ref[...] = h2
    hprev_ref[...] = h2


def _merge(o_na, o_hg, og, ga, gh, stream, mods, hg_gain, n2_gain, w_na_o, w_hg_o, w_out, layer,
           w_router_t, router_bias, ctx_row, skip):
    first, rest, n_first = stream
    b, lt, _ = o_hg.shape
    d = first.shape[-1]
    n_t = lt // ROW_TILE - skip
    n_tot = b * n_t

    def tile(s):
        t = jnp.minimum(s, n_tot - 1)
        return t // n_t, t % n_t

    def rows(width):
        return pl.BlockSpec((1, ROW_TILE, width), lambda s: (tile(s)[0], tile(s)[1] + skip, 0))

    def full(a):
        return pl.BlockSpec(a.shape, lambda s: (0,) * a.ndim)

    def of_layer(a):
        return pl.BlockSpec((1,) + a.shape[1:], lambda s: (layer,) + (0,) * (a.ndim - 1))

    def mod_map(s):
        bi, j = tile(s)
        return (jnp.where(j + skip == 0, ctx_row, bi), 0, 0)

    tok = pl.BlockSpec((1, 1, ROW_TILE), lambda s: (jnp.maximum(s - 1, 0), 0, 0))
    consts = [hg_gain, n2_gain, w_na_o, w_hg_o, w_out, w_router_t, router_bias]
    const_specs = [full(hg_gain), full(n2_gain), of_layer(w_na_o), of_layer(w_hg_o), of_layer(w_out),
                   full(w_router_t), full(router_bias)]
    kern = functools.partial(_merge_kernel, n_t=n_t, n_tot=n_tot, skip=skip)
    return pl.pallas_call(
        kern,
        grid=(n_tot + 1,),
        in_specs=[pl.BlockSpec((1, ROW_TILE, NA_WIDTH), lambda s: tile(s) + (0,)),
                  rows(HG_WIDTH), rows(HG_WIDTH), rows(d), rows(d),
                  pl.BlockSpec((1, ROW_TILE, d), lambda s: (tile(s)[0], 0, 0)),
                  pl.BlockSpec((1, ROW_TILE, d),
                               lambda s: (tile(s)[0], jnp.maximum(tile(s)[1] + skip - n_first, 0), 0)),
                  pl.BlockSpec((1, 6, d), mod_map)] + const_specs,
        out_specs=[pl.BlockSpec((1, ROW_TILE, d), lambda s: tile(s) + (0,)),
                   pl.BlockSpec((ROW_TILE, d), lambda s: (jnp.minimum(s, n_tot - 1), 0)),
                   tok, tok,
                   pl.BlockSpec((BUCKET_ROWS, LANES), lambda s: (0, 0))],
        out_shape=[jax.ShapeDtypeStruct((b, n_t * ROW_TILE, d), F32),
                   jax.ShapeDtypeStruct((b * n_t * ROW_TILE, d), F32),
                   jax.ShapeDtypeStruct((b * n_t, 1, ROW_TILE), jnp.int32),
                   jax.ShapeDtypeStruct((b * n_t, 1, ROW_TILE), jnp.int32),
                   jax.ShapeDtypeStruct((BUCKET_ROWS, LANES), jnp.int32)],
        scratch_shapes=[pltpu.VMEM((BUCKET_ROWS, LANES), F32), pltpu.VMEM((ROW_TILE, d), F32)],
        compiler_params=_cparams(("arbitrary",)),
        name="merge_router",
    )(o_na, o_hg, og, ga, gh, first, rest, mods, *consts)


def _start_row_copies(n, row_copy):
    for r in range(n):
        row_copy(r).start(priority=r % 2)


def _scatter_kernel(dest_ref, h_ref, xs_ref, zero_ref, sem, *, n_token_steps):
    n = h_ref.shape[0]
    step = pl.program_id(0)

    @pl.when(step < n_token_steps)
    def _():
        _start_row_copies(n, lambda r: pltpu.make_async_copy(
            h_ref.at[pl.ds(r, 1)], xs_ref.at[pl.ds(dest_ref[0, 0, r], 1)], sem))

    @pl.when(step >= n_token_steps)
    def _():
        zero_ref[...] = jnp.zeros_like(zero_ref)
        _start_row_copies(n, lambda r: pltpu.make_async_copy(
            zero_ref.at[pl.ds(0, 1)], xs_ref.at[pl.ds(dest_ref[0, 0, r], 1)], sem))

    pltpu.make_async_copy(h_ref, xs_ref.at[pl.ds(0, n)], sem).wait()


def _scatter_rows(h2, dest, pad_pos):
    t, d = h2.shape
    tile = 2 * ROW_TILE if t % (2 * ROW_TILE) == 0 and pad_pos.shape[0] % (2 * ROW_TILE) == 0 else ROW_TILE
    n_t = t // tile
    n_steps = n_t + pad_pos.shape[0] // tile
    kern = functools.partial(_scatter_kernel, n_token_steps=n_t)
    return pl.pallas_call(
        kern,
        grid=(n_steps,),
        in_specs=[pl.BlockSpec((1, 1, tile), lambda i: (i, 0, 0), memory_space=pltpu.SMEM),
                  pl.BlockSpec((tile, d), lambda i: (jnp.minimum(i, n_t - 1), 0))],
        out_specs=pl.BlockSpec(memory_space=pl.ANY),
        out_shape=jax.ShapeDtypeStruct((n_steps * tile, d), F32),
        scratch_shapes=[pltpu.VMEM((SUBLANES, d), F32), pltpu.SemaphoreType.DMA(())],
        compiler_params=_cparams(("arbitrary",)),
        name="scatter_rows",
    )(jnp.concatenate([dest, pad_pos]).reshape(n_steps, 1, tile), h2)


def _final_kernel(dest_ref, dest_next_ref, xn_ref, mod_ref, g_ref, ys_ref, o_ref, buf_ref, sems,
                  *, final_norm):
    n = buf_ref.shape[1]
    step = pl.program_id(0) * pl.num_programs(1) + pl.program_id(1)
    n_steps = pl.num_programs(0) * pl.num_programs(1)

    def gather(d_ref, slot):
        _start_row_copies(n, lambda r: pltpu.make_async_copy(
            ys_ref.at[pl.ds(d_ref[0, 0, r], 1)], buf_ref.at[slot, pl.ds(r, 1)], sems.at[slot]))

    @pl.when(step == 0)
    def _():
        gather(dest_ref, 0)

    @pl.when(step + 1 < n_steps)
    def _():
        gather(dest_next_ref, (step + 1) % 2)

    slot = step % 2
    pltpu.make_async_copy(ys_ref.at[pl.ds(0, n)], buf_ref.at[slot], sems.at[slot]).wait()
    x = xn_ref[0] + mod_ref[0, 5:6, :] * buf_ref[slot]
    if final_norm:
        x = x * lax.rsqrt(jnp.mean(x * x, axis=-1, keepdims=True) + EPS) * g_ref[...]
    o_ref[0] = x


def _gather_residual(xn, mods, ys, dest, gain, ctx_row, final_norm, skip):
    b, lt, d = xn.shape
    n_t = lt // ROW_TILE

    def mod_map(bi, j):
        return (jnp.where(j + skip == 0, ctx_row, bi), 0, 0)

    kern = functools.partial(_final_kernel, final_norm=final_norm)
    last_tile = b * n_t - 1
    dest3 = dest.reshape(b * n_t, 1, ROW_TILE)
    return pl.pallas_call(
        kern,
        grid=(b, n_t),
        in_specs=[pl.BlockSpec((1, 1, ROW_TILE), lambda bi, j: (bi * n_t + j, 0, 0),
                               memory_space=pltpu.SMEM),
                  pl.BlockSpec((1, 1, ROW_TILE), lambda bi, j: (jnp.minimum(bi * n_t + j + 1, last_tile), 0, 0),
                               memory_space=pltpu.SMEM),
                  pl.BlockSpec((1, ROW_TILE, d), lambda bi, j: (bi, j, 0)),
                  pl.BlockSpec((1, 6, d), mod_map),
                  pl.BlockSpec((1, d), lambda bi, j: (0, 0)),
                  pl.BlockSpec(memory_space=pl.ANY)],
        out_specs=pl.BlockSpec((1, ROW_TILE, d), lambda bi, j: (bi, j, 0)),
        out_shape=jax.ShapeDtypeStruct((b, lt, d), F32),
        scratch_shapes=[pltpu.VMEM((2, ROW_TILE, d), F32), pltpu.SemaphoreType.DMA((2,))],
        compiler_params=_cparams(("arbitrary", "arbitrary")),
        name="gather_residual",
    )(dest3, dest3, xn, mods, gain, ys)


def _moe_kernel(ea_ref, eb_ref, nused_ref, xs_ref, wra_ref, wrb_ref,
                wga_ref, wua_ref, wda_ref, wgb_ref, wub_ref, wdb_ref, ys_ref):
    del ea_ref, eb_ref
    i = pl.program_id(0)

    @pl.when(i < nused_ref[0])
    def _():
        x = xs_ref[...]
        s_a = jax.nn.sigmoid(jnp.sum(x * wra_ref[0], axis=-1, keepdims=True))
        s_b = jax.nn.sigmoid(jnp.sum(x * wrb_ref[0], axis=-1, keepdims=True))
        tot = s_a + s_b
        xb = x.astype(BF16)

        gate_a = jnp.dot(xb, wga_ref[0], preferred_element_type=F32)
        up_a = jnp.dot(xb, wua_ref[0], preferred_element_type=F32)
        gate_b = jnp.dot(xb, wgb_ref[0], preferred_element_type=F32)
        hid_a = (gate_a * jax.nn.sigmoid(gate_a) * up_a).astype(BF16)
        up_b = jnp.dot(xb, wub_ref[0], preferred_element_type=F32)
        y_a = jnp.dot(hid_a, wda_ref[0], preferred_element_type=F32)
        hid_b = (gate_b * jax.nn.sigmoid(gate_b) * up_b).astype(BF16)
        y_b = jnp.dot(hid_b, wdb_ref[0], preferred_element_type=F32)
        ys_ref[...] = (s_a / tot) * y_a + (s_b / tot) * y_b

    @pl.when(i >= nused_ref[0])
    def _():
        ys_ref[...] = jnp.zeros_like(ys_ref)


def _moe(xs, tile_ea, tile_eb, n_used, w_router_rows, w_gate, w_up, w_down, layer):
    n_sorted, d = xs.shape
    n_tiles = n_sorted // ROW_TILE
    ff = w_gate.shape[-1]
    base = layer * N_EXPERTS

    def by_a(i, ea, eb, nu):
        return (ea[i], 0, 0)

    def by_b(i, ea, eb, nu):
        return (eb[i], 0, 0)

    def wt_a(i, ea, eb, nu):
        return (base + ea[i], 0, 0)

    def wt_b(i, ea, eb, nu):
        return (base + eb[i], 0, 0)

    grid_spec = pltpu.PrefetchScalarGridSpec(
        num_scalar_prefetch=3,
        grid=(n_tiles,),
        in_specs=[pl.BlockSpec((ROW_TILE, d), lambda i, ea, eb, nu: (i, 0)),
                  pl.BlockSpec((1, 1, d), by_a), pl.BlockSpec((1, 1, d), by_b),
                  pl.BlockSpec((1, d, ff), wt_a), pl.BlockSpec((1, d, ff), wt_a),
                  pl.BlockSpec((1, ff, d), wt_a),
                  pl.BlockSpec((1, d, ff), wt_b), pl.BlockSpec((1, d, ff), wt_b),
                  pl.BlockSpec((1, ff, d), wt_b)],
        out_specs=pl.BlockSpec((ROW_TILE, d), lambda i, ea, eb, nu: (i, 0)),
    )
    return pl.pallas_call(
        _moe_kernel,
        grid_spec=grid_spec,
        out_shape=jax.ShapeDtypeStruct((n_sorted, d), F32),
        compiler_params=_cparams(("arbitrary",)),
        name="moe_pairs",
    )(tile_ea, tile_eb, n_used, xs, w_router_rows, w_router_rows,
      w_gate, w_up, w_down, w_gate, w_up, w_down)


_PAIR_LO = np.array([0, 0, 0, 1, 1, 2], np.int32)
_PAIR_HI = np.array([1, 2, 3, 2, 3, 3], np.int32)


def _sorted_layout(bucket, rank, counts, n_tiles):
    counts = counts[:N_BUCKETS]
    padded = ((counts + ROW_TILE - 1) // ROW_TILE) * ROW_TILE
    ends = jnp.cumsum(padded)
    starts = ends - padded

    def lookup(table, idx):
        hit = jnp.arange(table.shape[0], dtype=jnp.int32)[:, None] == idx[None, :]
        return jnp.sum(jnp.where(hit, table[:, None], 0), axis=0)

    dest = lookup(starts, bucket) + rank
    tile_start = jnp.arange(n_tiles, dtype=jnp.int32) * ROW_TILE
    tile_bucket = jnp.sum((ends[None, :] <= tile_start[:, None]).astype(jnp.int32), axis=1)
    tile_bucket = jnp.minimum(tile_bucket, N_BUCKETS - 1)
    group = tile_bucket // N_PAIRS
    pair = tile_bucket % N_PAIRS
    ea = group * EXPERTS_PER_GROUP + jnp.asarray(_PAIR_LO)[pair]
    eb = group * EXPERTS_PER_GROUP + jnp.asarray(_PAIR_HI)[pair]
    n_used = (ends[-1] // ROW_TILE).astype(jnp.int32).reshape(1)
    seg_start = jnp.concatenate([starts + counts, ends[-1:]])
    seg_len = jnp.concatenate([padded - counts, n_tiles * ROW_TILE - ends[-1:]])
    seg_end = jnp.cumsum(seg_len)
    k = jnp.arange(n_tiles * ROW_TILE - bucket.shape[0], dtype=jnp.int32)
    seg = jnp.sum((seg_end[:, None] <= k[None, :]).astype(jnp.int32), axis=0)
    pad_pos = lookup(seg_start - (seg_end - seg_len), seg) + k
    return (dest.astype(jnp.int32), pad_pos.astype(jnp.int32), ea.astype(jnp.int32), eb.astype(jnp.int32),
            n_used)


def _lower_bounds(raw):
    p = jax.nn.softmax(raw.astype(F32), axis=0)
    return jnp.cumsum(p, axis=0) - p[0:1]


def kernel(x, c, ctx, c_ctx, w_ada, b_ada, norm1_g, w_in, na_rel_bias, hg_lower_fwd, hg_lower_bwd,
           hg_norm_g, w_na_o, w_hg_o, w_out, norm2_g, w_router, router_bias, w_gate, w_up, w_down,
           final_g):
    b, seq, d = x.shape
    n_ctx = ctx.shape[1]
    depth = w_ada.shape[0]
    lt = n_ctx + seq
    assert n_ctx % ROW_TILE == 0 and seq % ROW_TILE == 0 and seq % GRID_W == 0
    assert n_ctx % HG_CHUNK == 0 and seq % HG_CHUNK == 0
    assert seq // GRID_W >= NA_WIN_ROWS + NA_Q_ROWS - 1 and (seq // GRID_W) % NA_Q_ROWS == 0
    assert n_ctx % (NA_Q_ROWS * GRID_W) == 0

    ada_rows = -(-(b + 1) // SUBLANES) * SUBLANES
    ctx_row = b
    cc = jnp.concatenate([c, c_ctx[None, :], jnp.zeros((ada_rows - b - 1, d), F32)], axis=0)
    mods = _ada(cc, w_ada, b_ada).reshape(depth, ada_rows, 6, d)

    lb_f = _lower_bounds(hg_lower_fwd)
    lb_b = _lower_bounds(hg_lower_bwd)
    cos_t, sin_t = _rope_tables(n_ctx, seq)
    w_router_t = jnp.transpose(w_router)
    w_router_rows = w_router_t.reshape(N_EXPERTS, 1, d)
    rb = router_bias.astype(F32).reshape(N_EXPERTS, 1)

    w_in_b, w_na_b, w_hg_b, w_out_b = (w.astype(BF16) for w in (w_in, w_na_o, w_hg_o, w_out))
    ff = w_gate.shape[-1]
    w_gate_b = w_gate.astype(BF16).reshape(depth * N_EXPERTS, d, ff)
    w_up_b = w_up.astype(BF16).reshape(depth * N_EXPERTS, d, ff)
    w_down_b = w_down.astype(BF16).reshape(depth * N_EXPERTS, ff, d)

    assert n_ctx == ROW_TILE
    stream = (ctx, x, 1)
    out = None
    for l in range(depth):
        last = l == depth - 1
        skip = n_ctx // ROW_TILE if last else 0
        n_tiles = b * (lt // ROW_TILE - skip) + N_BUCKETS
        q, k, v, hq, *gates, hi, og, ga, gh = _inproj(
            stream, lt, mods[l], norm1_g[l].reshape(1, d), w_in_b, l, cos_t, sin_t, lb_f[l], lb_b[l],
            ctx_row)
        o_na = _na_attention(q, k, v, na_rel_bias[l], n_ctx, with_ctx=not last)
        o_hg = _hgrn(hq, hi, gates[:3], gates[3:], n_ctx)
        xn, h2, bucket, rank, counts = _merge(
            o_na, o_hg, og, ga, gh, stream, mods[l], hg_norm_g[l].reshape(1, HG_DIM),
            norm2_g[l].reshape(1, d), w_na_b, w_hg_b, w_out_b, l, w_router_t, rb, ctx_row, skip)
        dest, pad_pos, tile_ea, tile_eb, n_used = _sorted_layout(
            bucket.reshape(-1), rank.reshape(-1), counts[:, 0], n_tiles)
        xs = _scatter_rows(h2, dest, pad_pos)
        ys = _moe(xs, tile_ea, tile_eb, n_used, w_router_rows, w_gate_b, w_up_b, w_down_b, l)
        res = _gather_residual(xn, mods[l], ys, dest, final_g.reshape(1, d), ctx_row, last, skip)
        if last:
            out = res
        else:
            stream = (res, res, 0)
    return out
```
